```python
import jax, jax.numpy as jnp
from jax import lax
import numpy as np

D_MODEL = 1024
BATCH = 8
SEQ = 4096
DEPTH = 1

GRID_W = 64
CTX_LEN = 256
RET_HEADS = 8
RET_DK = 64
RET_DV = 64
RET_W = RET_HEADS * RET_DK
RET_VW = RET_HEADS * RET_DV
RET_CHUNK = 128
NA_HEADS = 8
NA_DH = 64
NA_W = NA_HEADS * NA_DH
NA_WIN_H = 8
NA_WIN_W = 16
D_FF = 4 * D_MODEL
ROPE_BASE = 10000.0
EPS = 1e-6
NEG_INF = -1e30
IN_COLS = 2 * RET_W + 2 * RET_VW + 3 * NA_W + 2 * D_MODEL
SPLIT_AT = (RET_W, 2 * RET_W, 2 * RET_W + RET_VW, 2 * RET_W + 2 * RET_VW,
            2 * RET_W + 2 * RET_VW + NA_W, 2 * RET_W + 2 * RET_VW + 2 * NA_W,
            2 * RET_W + 2 * RET_VW + 3 * NA_W)

kernel_name = 'hybrid_retention_natten_dit_block'


def rmsnorm(x, g):
    xf = x.astype(jnp.float32)
    y = xf * lax.rsqrt(jnp.mean(xf * xf, axis=-1, keepdims=True) + EPS)
    return (y * g.astype(jnp.float32)).astype(x.dtype)


def modulate(x, g, shift, scale):
    return rmsnorm(x, g) * (1 + scale) + shift


def to_heads(t, n_heads):
    b, n, _ = t.shape
    return t.reshape(b, n, n_heads, -1).transpose(0, 2, 1, 3)


def from_heads(t):
    b, h, n, d = t.shape
    return t.transpose(0, 2, 1, 3).reshape(b, n, h * d)


def flip_seq(t):
    return jnp.flip(t, axis=2)


def axial_rope_tables(n, dh):
    pos = jnp.arange(n)
    row = (pos // GRID_W).astype(jnp.float32)
    col = (pos % GRID_W).astype(jnp.float32)
    d_axis = dh // 2
    inv = ROPE_BASE ** (-jnp.arange(0, d_axis, 2, dtype=jnp.float32) / d_axis)
    ang = jnp.concatenate([row[:, None] * inv, col[:, None] * inv], axis=-1)
    return jnp.cos(ang), jnp.sin(ang)


def apply_rope(t, cos, sin):
    half = t.shape[-1] // 2
    t1 = t[..., :half].astype(jnp.float32)
    t2 = t[..., half:].astype(jnp.float32)
    return jnp.concatenate([t1 * cos - t2 * sin, t1 * sin + t2 * cos], axis=-1).astype(t.dtype)


def retention_chunkwise(q, k, v, log_g, init_state, inclusive):
    b, h, n, dk = q.shape
    dv = v.shape[-1]
    L = RET_CHUNK
    nc = n // L
    qc = q.astype(jnp.float32).reshape(b, h, nc, L, dk)
    kc = k.astype(jnp.float32).reshape(b, h, nc, L, dk)
    vc = v.astype(jnp.float32).reshape(b, h, nc, L, dv)
    idx = jnp.arange(L, dtype=jnp.float32)
    diff = idx[:, None] - idx[None, :]
    mask = (diff >= 0) if inclusive else (diff > 0)
    decay = jnp.where(mask, jnp.exp(log_g[:, None, None] * jnp.maximum(diff, 0.0)), 0.0)
    scores = jnp.einsum('bhcid,bhcjd->bhcij', qc, kc) * decay[None, :, None]
    o_intra = jnp.einsum('bhcij,bhcje->bhcie', scores, vc)
    k_w = jnp.exp(log_g[:, None] * (L - 1 - idx)[None, :])
    s_chunk = jnp.einsum('bhcjd,hj,bhcje->cbhde', kc, k_w, vc)
    g_chunk = jnp.exp(log_g * L)[None, :, None, None]

    def step(state, s_c):
        return g_chunk * state + s_c, state

    _, r_before = lax.scan(step, init_state, s_chunk)
    q_w = jnp.exp(log_g[:, None] * (idx + 1)[None, :])
    o_cross = jnp.einsum('bhcid,hi,cbhde->bhcie', qc, q_w, r_before)
    return (o_intra + o_cross).reshape(b, h, n, dv)


def retention_final_state(k, v, log_g):
    n = k.shape[2]
    w = jnp.exp(log_g[:, None] * (n - 1 - jnp.arange(n, dtype=jnp.float32))[None, :])
    return jnp.einsum('bhnd,hn,bhne->bhde', k.astype(jnp.float32), w, v.astype(jnp.float32))


def bidirectional_retention(q, k, v, log_g_fwd, log_g_bwd, state_fwd, state_bwd):
    o_f = retention_chunkwise(q, k, v, log_g_fwd, state_fwd, inclusive=True)
    o_b = retention_chunkwise(flip_seq(q), flip_seq(k), flip_seq(v), log_g_bwd, state_bwd, inclusive=False)
    return o_f + flip_seq(o_b)


def retention_readout(o, g):
    o = o * lax.rsqrt(jnp.mean(o * o, axis=-1, keepdims=True) + EPS)
    return from_heads(o).astype(g.dtype) * jax.nn.silu(g)


def neighborhood_attention(q, k, v, k_ctx, v_ctx, rpb):
    b, h, n, d = q.shape
    rows = n // GRID_W
    kh = min(NA_WIN_H, rows)
    qg = (q.astype(jnp.float32) * d ** -0.5).reshape(b, h, rows, GRID_W, d)
    kg = k.astype(jnp.float32).reshape(b, h, rows, GRID_W, d)
    vg = v.astype(jnp.float32).reshape(b, h, rows, GRID_W, d)
    kc = k_ctx.astype(jnp.float32)
    vc = v_ctx.astype(jnp.float32)
    rpb = rpb.astype(jnp.float32)
    col = jnp.arange(GRID_W)
    c0 = jnp.clip(col - NA_WIN_W // 2, 0, GRID_W - NA_WIN_W)
    col_mask = (col[None, :] >= c0[:, None]) & (col[None, :] < c0[:, None] + NA_WIN_W)
    col_idx = jnp.clip(col[None, :] - col[:, None] + NA_WIN_W - 1, 0, 2 * NA_WIN_W - 2)

    def row_block(r):
        r0 = jnp.clip(r - kh // 2, 0, rows - kh)
        q_r = lax.dynamic_index_in_dim(qg, r, axis=2, keepdims=False)
        k_s = lax.dynamic_slice_in_dim(kg, r0, kh, axis=2)
        v_s = lax.dynamic_slice_in_dim(vg, r0, kh, axis=2)
        row_idx = r0 + jnp.arange(kh) - r + NA_WIN_H - 1
        bias = rpb[:, row_idx][:, :, col_idx].transpose(0, 2, 1, 3)
        s_loc = jnp.einsum('bhqd,bhkwd->bhqkw', q_r, k_s) + bias
        s_loc = jnp.where(col_mask[:, None, :], s_loc, NEG_INF)
        s_ctx = jnp.einsum('bhqd,bhkd->bhqk', q_r, kc)
        s = jnp.concatenate([s_loc.reshape(b, h, GRID_W, kh * GRID_W), s_ctx], axis=-1)
        p = jax.nn.softmax(s, axis=-1)
        p_loc = p[..., :kh * GRID_W].reshape(b, h, GRID_W, kh, GRID_W)
        p_ctx = p[..., kh * GRID_W:]
        return (jnp.einsum('bhqkw,bhkwd->bhqd', p_loc, v_s)
                + jnp.einsum('bhqk,bhkd->bhqd', p_ctx, vc))

    out = lax.map(row_block, jnp.arange(rows))
    return out.transpose(1, 2, 0, 3, 4).reshape(b, h, n, d).astype(q.dtype)


def context_attention(q, k, v):
    d = q.shape[-1]
    s = jnp.einsum('bhqd,bhkd->bhqk', q.astype(jnp.float32), k.astype(jnp.float32)) * d ** -0.5
    p = jax.nn.softmax(s, axis=-1)
    return jnp.einsum('bhqk,bhkd->bhqd', p, v.astype(jnp.float32)).astype(q.dtype)


def merge_branches(y_ret, y_na, gates, w_ret_out, w_na_out, w_o):
    g_ret, g_na = jnp.split(jax.nn.sigmoid(gates), 2, axis=-1)
    return (g_ret * (y_ret @ w_ret_out) + g_na * (y_na @ w_na_out)) @ w_o


def squared_relu_mlp(h, w1, w2):
    return jnp.square(jax.nn.relu(h @ w1)) @ w2


def setup_inputs(seed: int = 0) -> dict:
    key = jax.random.key(seed)
    ks = jax.random.split(key, 18)

    def nrm(k, shape, s):
        return jax.random.normal(k, shape, jnp.float32) * s

    base_logit = jnp.asarray(np.log(2.0 ** (5 + np.arange(RET_HEADS)) - 1.0), jnp.float32)
    return {
        'x': nrm(ks[0], (BATCH, SEQ, D_MODEL), 1.0),
        'c': nrm(ks[1], (BATCH, D_MODEL), 1.0),
        'ctx': nrm(ks[2], (BATCH, CTX_LEN, D_MODEL), 1.0),
        'c_ctx': nrm(ks[3], (D_MODEL,), 1.0),
        'w_ada': nrm(ks[4], (DEPTH, D_MODEL, 6 * D_MODEL), 0.5 * D_MODEL ** -0.5),
        'b_ada': nrm(ks[5], (DEPTH, 6 * D_MODEL), 0.02),
        'norm_pre_mix': 1.0 + nrm(ks[6], (DEPTH, D_MODEL), 0.02),
        'norm_post_mix': 1.0 + nrm(ks[7], (DEPTH, D_MODEL), 0.02),
        'norm_pre_ffn': 1.0 + nrm(ks[8], (DEPTH, D_MODEL), 0.02),
        'norm_post_ffn': 1.0 + nrm(ks[9], (DEPTH, D_MODEL), 0.02),
        'w_in': nrm(ks[10], (DEPTH, D_MODEL, IN_COLS), D_MODEL ** -0.5),
        'ret_decay_logit': base_logit[None, None, :] + nrm(ks[11], (DEPTH, 2, RET_HEADS), 0.05),
        'w_ret_out': nrm(ks[12], (DEPTH, RET_VW, D_MODEL), RET_VW ** -0.5),
        'na_rpb': nrm(ks[13], (DEPTH, NA_HEADS, 2 * NA_WIN_H - 1, 2 * NA_WIN_W - 1), 0.02),
        'w_na_out': nrm(ks[14], (DEPTH, NA_W, D_MODEL), NA_W ** -0.5),
        'w_o': nrm(ks[15], (DEPTH, D_MODEL, D_MODEL), D_MODEL ** -0.5),
        'w_ff1': nrm(ks[16], (DEPTH, D_MODEL, D_FF), D_MODEL ** -0.5),
        'w_ff2': nrm(ks[17], (DEPTH, D_FF, D_MODEL), D_FF ** -0.5),
    }


def reference(x, c, ctx, c_ctx, w_ada, b_ada, norm_pre_mix, norm_post_mix, norm_pre_ffn,
              norm_post_ffn, w_in, ret_decay_logit, w_ret_out, na_rpb, w_na_out, w_o, w_ff1, w_ff2):
    n = x.shape[1]
    cos, sin = axial_rope_tables(n, RET_DK)
    k_scale = RET_DK ** -0.5
    x_lat, x_ctx = x, ctx
    for l in range(DEPTH):
        last = l == DEPTH - 1
        mod_lat = (jax.nn.silu(c) @ w_ada[l] + b_ada[l])[:, None, :]
        mod_ctx = jax.nn.silu(c_ctx) @ w_ada[l] + b_ada[l]
        sh1, sc1, gt1, sh2, sc2, gt2 = jnp.split(mod_lat, 6, axis=-1)
        csh1, csc1, cgt1, csh2, csc2, cgt2 = jnp.split(mod_ctx, 6, axis=-1)
        log_g = jax.nn.log_sigmoid(ret_decay_logit[l].astype(jnp.float32))
        lg_f, lg_b = log_g[0], log_g[1]

        h_lat = modulate(x_lat, norm_pre_mix[l], sh1, sc1)
        h_ctx = modulate(x_ctx, norm_pre_mix[l], csh1, csc1)
        rq, rk, rv, rg, nq, nk, nv, gates = jnp.split(h_lat @ w_in[l], SPLIT_AT, axis=-1)
        crq, crk, crv, crg, cnq, cnk, cnv, cgates = jnp.split(h_ctx @ w_in[l], SPLIT_AT, axis=-1)

        crk_h = to_heads(crk, RET_HEADS) * k_scale
        crv_h = to_heads(crv, RET_HEADS)
        s_f = retention_final_state(crk_h, crv_h, lg_f)
        s_b = retention_final_state(flip_seq(crk_h), flip_seq(crv_h), lg_b)
        q_r = apply_rope(to_heads(rq, RET_HEADS), cos, sin)
        k_r = apply_rope(to_heads(rk, RET_HEADS), cos, sin) * k_scale
        o_ret = bidirectional_retention(q_r, k_r, to_heads(rv, RET_HEADS), lg_f, lg_b, s_f, s_b)
        y_ret = retention_readout(o_ret, rg)

        cnk_h = to_heads(cnk, NA_HEADS)
        cnv_h = to_heads(cnv, NA_HEADS)
        o_na = neighborhood_attention(to_heads(nq, NA_HEADS), to_heads(nk, NA_HEADS),
                                      to_heads(nv, NA_HEADS), cnk_h, cnv_h, na_rpb[l])
        y_na = from_heads(o_na)

        y = merge_branches(y_ret, y_na, gates, w_ret_out[l], w_na_out[l], w_o[l])
        x_lat_next = x_lat + gt1 * rmsnorm(y, norm_post_mix[l])

        if not last:
            zero_state = jnp.zeros((x_ctx.shape[0], RET_HEADS, RET_DK, RET_DV), jnp.float32)
            co_ret = bidirectional_retention(to_heads(crq, RET_HEADS), crk_h, crv_h, lg_f, lg_b,
                                             zero_state, zero_state)
            cy_ret = retention_readout(co_ret, crg)
            cy_na = from_heads(context_attention(to_heads(cnq, NA_HEADS), cnk_h, cnv_h))
            cy = merge_branches(cy_ret, cy_na, cgates, w_ret_out[l], w_na_out[l], w_o[l])
            x_ctx = x_ctx + cgt1 * rmsnorm(cy, norm_post_mix[l])
            ch = modulate(x_ctx, norm_pre_ffn[l], csh2, csc2)
            x_ctx = x_ctx + cgt2 * rmsnorm(squared_relu_mlp(ch, w_ff1[l], w_ff2[l]), norm_post_ffn[l])

        x_lat = x_lat_next
        h2 = modulate(x_lat, norm_pre_ffn[l], sh2, sc2)
        x_lat = x_lat + gt2 * rmsnorm(squared_relu_mlp(h2, w_ff1[l], w_ff2[l]), norm_post_ffn[l])
    return x_lat
```

```python
import functools

import jax
import jax.numpy as jnp
import numpy as np
from jax import lax
from jax.experimental import pallas as pl
from jax.experimental.pallas import tpu as pltpu

F32 = jnp.float32
BF16 = jnp.bfloat16

GRID_W = 64
HEADS = 8
HEAD_DIM = 64
MIX_W = HEADS * HEAD_DIM
NA_WIN_H = 8
NA_WIN_W = 16
ROPE_BASE = 10000.0
EPS = 1e-6
NEG_INF = -1e30
K_SCALE = HEAD_DIM ** -0.5

LANES = 128
V7X_VMEM_BYTES = 64 * 2 ** 20
VMEM_LIMIT = V7X_VMEM_BYTES - 8 * 2 ** 20

HEAD_GROUP = 4
GROUP_W = HEAD_GROUP * HEAD_DIM
N_GROUPS = HEADS // HEAD_GROUP
PROJ_TM = 512
MIX_TM = 512
FF_CHUNK = 1024
RET_L = 128
NA_ROWS = 4
NA_KROWS = NA_ROWS + NA_WIN_H - 1
NA_Q = NA_ROWS * GRID_W
NA_K = NA_KROWS * GRID_W

D_MODEL = 1024
GATE_COLS = 2 * D_MODEL
COL_RQ, COL_RK, COL_RV, COL_RG, COL_NQ, COL_NK, COL_NV = (
    GATE_COLS // GROUP_W + i * N_GROUPS for i in range(7))
CCOL_RK, CCOL_RV, CCOL_NK, CCOL_NV = (i * N_GROUPS for i in range(4))


def _cparams(*sem):
    return pltpu.CompilerParams(dimension_semantics=sem, vmem_limit_bytes=VMEM_LIMIT)


def _resident(shape):
    nd = len(shape)
    return pl.BlockSpec(shape, lambda *_: (0,) * nd, pipeline_mode=pl.Buffered(1))


def _rms(x):
    return x * lax.rsqrt(jnp.mean(x * x, axis=-1, keepdims=True) + EPS)


def _silu(x):
    return x * jax.nn.sigmoid(x)


def _ada_kernel(c_ref, w_ref, b_ref, o_ref):
    a = _silu(c_ref[...])
    o_ref[...] = jnp.dot(a, w_ref[...], preferred_element_type=F32,
                         precision=lax.Precision.HIGHEST) + b_ref[...]


def _ada(c_rows, w_ada, b_ada):
    rows, d = c_rows.shape
    cols = w_ada.shape[1]
    tn = d
    return pl.pallas_call(
        _ada_kernel,
        grid=(cols // tn,),
        in_specs=[pl.BlockSpec((rows, d), lambda j: (0, 0)),
                  pl.BlockSpec((d, tn), lambda j: (0, j)),
                  pl.BlockSpec((1, tn), lambda j: (0, j))],
        out_specs=pl.BlockSpec((rows, tn), lambda j: (0, j)),
        out_shape=jax.ShapeDtypeStruct((rows, cols), F32),
        compiler_params=_cparams("arbitrary"),
        name="ada_mod",
    )(c_rows, w_ada, b_ada.reshape(1, cols))


def _rope_slab(t, cos, sin_lo, sin_hi):
    return (t * cos + pltpu.roll(t, 3 * LANES // 4, axis=1) * sin_lo
            + pltpu.roll(t, LANES // 4, axis=1) * sin_hi)


def _proj_kernel(x_ref, sh_ref, sc_ref, g_ref, w_ref, *rest, rope_cols, k_cols):
    if rope_cols is not None:
        cos_ref, slo_ref, shi_ref, o_ref = rest
    else:
        (o_ref,) = rest
    x = x_ref[...]
    h = (_rms(x) * g_ref[...]) * (1.0 + sc_ref[...]) + sh_ref[...]
    h = h.astype(BF16)
    n_out = o_ref.shape[-1]
    for c0 in range(0, n_out, MIX_W):
        r = jnp.dot(h, w_ref[:, c0:c0 + MIX_W], preferred_element_type=F32)
        if rope_cols is not None and rope_cols <= c0 < rope_cols + 2 * MIX_W:
            cos, slo, shi = cos_ref[...], slo_ref[...], shi_ref[...]
            r = jnp.concatenate(
                [_rope_slab(r[:, j:j + LANES], cos, slo, shi) for j in range(0, MIX_W, LANES)], axis=1)
        if c0 == k_cols:
            r = r * K_SCALE
        o_ref[:, c0:c0 + MIX_W] = r.astype(BF16)


def _proj_latent(x, mod4, g, w_bf, rope_tabs):
    b, n, d = x.shape
    cols = w_bf.shape[1]
    tm = PROJ_TM
    mod_spec = lambda k: pl.BlockSpec((None, None, 1, d), lambda i, bb: (bb, k, 0, 0))
    tab_spec = pl.BlockSpec((tm, LANES), lambda i, bb: (i, 0))
    return pl.pallas_call(
        functools.partial(_proj_kernel, rope_cols=COL_RQ * GROUP_W, k_cols=COL_RK * GROUP_W),
        grid=(n // tm, b),
        in_specs=[pl.BlockSpec((None, tm, d), lambda i, bb: (bb, i, 0)),
                  mod_spec(0), mod_spec(1), _resident((1, d)), _resident((d, cols)),
                  tab_spec, tab_spec, tab_spec],
        out_specs=pl.BlockSpec((None, tm, cols), lambda i, bb: (bb, i, 0)),
        out_shape=jax.ShapeDtypeStruct((b, n, cols), BF16),
        compiler_params=_cparams("arbitrary", "arbitrary"),
        name="proj_latent",
    )(x, mod4, mod4, g, w_bf, *rope_tabs)


def _proj_context(ctx_rows, mod4, ctx_row, g, w_bf):
    rows, d = ctx_rows.shape
    cols = w_bf.shape[1]
    tm = min(PROJ_TM, rows)
    mod_spec = lambda k: pl.BlockSpec((None, None, 1, d), lambda i: (ctx_row, k, 0, 0))
    return pl.pallas_call(
        functools.partial(_proj_kernel, rope_cols=None, k_cols=CCOL_RK * GROUP_W),
        grid=(rows // tm,),
        in_specs=[pl.BlockSpec((tm, d), lambda i: (i, 0)),
                  mod_spec(0), mod_spec(1), _resident((1, d)), _resident((d, cols))],
        out_specs=pl.BlockSpec((tm, cols), lambda i: (i, 0)),
        out_shape=jax.ShapeDtypeStruct((rows, cols), BF16),
        compiler_params=_cparams("arbitrary"),
        name="proj_context",
    )(ctx_rows, mod4, mod4, g, w_bf)


def _log_sigmoid(x):
    return jnp.minimum(x, 0.0) - jnp.log1p(jnp.exp(-jnp.abs(x)))


def _ret_kernel(q_ref, k_ref, v_ref, rg_ref, ck_ref, cv_ref, lg_lane_ref, lg_chunk_ref, o_ref,
                acc_ref, dec_ref, sf_ref, sb_ref):
    L = RET_L
    n = q_ref.shape[0]
    n_ctx = ck_ref.shape[0]
    nc, ncc = n // L, n_ctx // L

    lg = _log_sigmoid(lg_lane_ref[...])
    lgf, lgb = lg[0:1, :], lg[1:2, :]
    pos = lax.broadcasted_iota(jnp.int32, (L, 1), 0).astype(F32)
    qw_f = jnp.exp(lgf * (pos + 1.0))
    kw_f = jnp.exp(lgf * (L - 1.0 - pos))
    qw_b = jnp.exp(lgb * (L - pos))
    kw_b = jnp.exp(lgb * pos)
    gl_f = jnp.exp(lgf * float(L))
    gl_b = jnp.exp(lgb * float(L))

    lane_head = lax.broadcasted_iota(jnp.int32, (1, GROUP_W), 1) // HEAD_DIM
    row_head = lax.broadcasted_iota(jnp.int32, (GROUP_W, 1), 0) // HEAD_DIM
    block_diag = row_head == lane_head
    head_mean = jnp.where(block_diag, 1.0 / HEAD_DIM, 0.0).astype(BF16)
    head_sel = [(lane_head == h) for h in range(HEAD_GROUP)]
    head_mask = [m.astype(BF16) for m in head_sel]

    lgc = _log_sigmoid(lg_chunk_ref[...])
    diff = pos - lax.broadcasted_iota(jnp.int32, (1, L), 1).astype(F32)
    for h in range(HEAD_GROUP):
        f = jnp.exp(lgc[0:1, h * L:(h + 1) * L] * jnp.maximum(diff, 0.0))
        bk = jnp.exp(lgc[1:2, h * L:(h + 1) * L] * jnp.maximum(-diff, 0.0))
        dec_ref[h] = jnp.where(diff >= 0.0, f, bk)

    def state_update(s_ref, k, v, kw, gl):
        kd = (k.astype(F32) * kw).astype(BF16)
        contrib = lax.dot_general(kd, v, (((0,), (0,)), ((), ())), preferred_element_type=F32)
        s_ref[...] = s_ref[...] * gl + jnp.where(block_diag, contrib, 0.0)

    sf_ref[...] = jnp.zeros_like(sf_ref)
    sb_ref[...] = jnp.zeros_like(sb_ref)

    def ctx_fwd(c, carry):
        rows = pl.ds(pl.multiple_of(c * L, L), L)
        state_update(sf_ref, ck_ref[rows, :], cv_ref[rows, :], kw_f, gl_f)
        return carry

    def ctx_bwd(i, carry):
        rows = pl.ds(pl.multiple_of((ncc - 1 - i) * L, L), L)
        state_update(sb_ref, ck_ref[rows, :], cv_ref[rows, :], kw_b, gl_b)
        return carry

    lax.fori_loop(0, ncc, ctx_fwd, 0)
    lax.fori_loop(0, ncc, ctx_bwd, 0)

    def fwd_chunk(c, carry):
        rows = pl.ds(pl.multiple_of(c * L, L), L)
        q, k, v = q_ref[rows, :], k_ref[rows, :], v_ref[rows, :]
        o = jnp.dot(q, sf_ref[...].astype(BF16), preferred_element_type=F32) * qw_f
        for h in range(HEAD_GROUP):
            s = lax.dot_general(q * head_mask[h], k, (((1,), (1,)), ((), ())),
                                preferred_element_type=F32)
            s = (s * dec_ref[h]).astype(BF16)
            oh = jnp.dot(s, v, preferred_element_type=F32)
            o = o + jnp.where(head_sel[h], oh, 0.0)
        acc_ref[rows, :] = o
        state_update(sf_ref, k, v, kw_f, gl_f)
        return carry

    lax.fori_loop(0, nc, fwd_chunk, 0)

    def bwd_chunk(i, carry):
        rows = pl.ds(pl.multiple_of((nc - 1 - i) * L, L), L)
        q, k, v = q_ref[rows, :], k_ref[rows, :], v_ref[rows, :]
        o = acc_ref[rows, :] + jnp.dot(q, sb_ref[...].astype(BF16), preferred_element_type=F32) * qw_b
        sq = o * o
        sq_hi = sq.astype(BF16)
        sq_lo = (sq - sq_hi.astype(F32)).astype(BF16)
        ms = (jnp.dot(sq_hi, head_mean, preferred_element_type=F32)
              + jnp.dot(sq_lo, head_mean, preferred_element_type=F32))
        y = o * lax.rsqrt(ms + EPS) * _silu(rg_ref[rows, :].astype(F32))
        o_ref[rows, :] = y.astype(BF16)
        state_update(sb_ref, k, v, kw_b, gl_b)
        return carry

    lax.fori_loop(0, nc, bwd_chunk, 0)


def _retention(proj, cproj, lg_lane, lg_chunk):
    b, n, _ = proj.shape
    n_ctx = cproj.shape[1]
    L = RET_L
    col = lambda c0: pl.BlockSpec((None, n, GROUP_W), lambda bb, g: (bb, 0, c0 + g))
    ccol = lambda c0: pl.BlockSpec((None, n_ctx, GROUP_W), lambda bb, g: (bb, 0, c0 + g))
    return pl.pallas_call(
        _ret_kernel,
        grid=(b, N_GROUPS),
        in_specs=[col(COL_RQ), col(COL_RK), col(COL_RV), col(COL_RG), ccol(CCOL_RK), ccol(CCOL_RV),
                  pl.BlockSpec((2, GROUP_W), lambda bb, g: (0, g)),
                  pl.BlockSpec((2, HEAD_GROUP * L), lambda bb, g: (0, g))],
        out_specs=pl.BlockSpec((None, n, GROUP_W), lambda bb, g: (bb, 0, g)),
        out_shape=jax.ShapeDtypeStruct((b, n, MIX_W), BF16),
        scratch_shapes=[pltpu.VMEM((n, GROUP_W), F32),
                        pltpu.VMEM((HEAD_GROUP, L, L), F32),
                        pltpu.VMEM((GROUP_W, GROUP_W), F32),
                        pltpu.VMEM((GROUP_W, GROUP_W), F32)],
        compiler_params=_cparams("arbitrary", "arbitrary"),
        name="retention",
    )(proj, proj, proj, proj, cproj, cproj, lg_lane, lg_chunk)


def _na_kernel(q_ref, k_ref, v_ref, ck_ref, cv_ref, bias_ref, o_ref):
    n = q_ref.shape[0]
    rows = n // GRID_W
    nblk = rows // NA_ROWS
    lane_head = lax.broadcasted_iota(jnp.int32, (1, GROUP_W), 1) // HEAD_DIM
    head_sel = [(lane_head == h) for h in range(HEAD_GROUP)]
    head_mask = [m.astype(BF16) for m in head_sel]
    nt = (((1,), (1,)), ((), ()))

    def block(qb, carry):
        kb = jnp.clip(qb * NA_ROWS - NA_WIN_H // 2, 0, rows - NA_KROWS)
        pat = jnp.where(qb == 0, 0, jnp.where(qb == nblk - 1, 2, 1))
        q_rows = pl.ds(pl.multiple_of(qb * NA_Q, NA_Q), NA_Q)
        k_rows = pl.ds(pl.multiple_of(kb * GRID_W, GRID_W), NA_K)
        q = q_ref[q_rows, :] * K_SCALE
        kl, vl = k_ref[k_rows, :], v_ref[k_rows, :]
        kc, vc = ck_ref[...], cv_ref[...]
        out = jnp.zeros((NA_Q, GROUP_W), F32)
        for h in range(HEAD_GROUP):
            qh = q * head_mask[h]
            s_loc = lax.dot_general(qh, kl, nt, preferred_element_type=F32) + bias_ref[h, pat]
            s_ctx = lax.dot_general(qh, kc, nt, preferred_element_type=F32)
            m = jnp.maximum(jnp.max(s_loc, axis=1, keepdims=True), jnp.max(s_ctx, axis=1, keepdims=True))
            p_loc = jnp.exp(s_loc - m)
            p_ctx = jnp.exp(s_ctx - m)
            denom = jnp.sum(p_loc, axis=1, keepdims=True) + jnp.sum(p_ctx, axis=1, keepdims=True)
            oh = (jnp.dot(p_loc.astype(BF16), vl, preferred_element_type=F32)
                  + jnp.dot(p_ctx.astype(BF16), vc, preferred_element_type=F32))
            out = out + jnp.where(head_sel[h], oh / denom, 0.0)
        o_ref[q_rows, :] = out.astype(BF16)
        return carry

    lax.fori_loop(0, nblk, block, 0)


def _na_bias_tables(rpb, rows):
    nblk = rows // NA_ROWS
    assert nblk >= 3 and rows >= NA_KROWS and NA_ROWS >= NA_WIN_H // 2
    col = np.arange(GRID_W)
    c0 = np.clip(col - NA_WIN_W // 2, 0, GRID_W - NA_WIN_W)
    col_ok = (col[None, :] >= c0[:, None]) & (col[None, :] < c0[:, None] + NA_WIN_W)
    col_idx = np.clip(col[None, :] - col[:, None] + NA_WIN_W - 1, 0, 2 * NA_WIN_W - 2)
    tabs = []
    for qb in (0, 1, nblk - 1):
        kb = int(np.clip(qb * NA_ROWS - NA_WIN_H // 2, 0, rows - NA_KROWS))
        qr = qb * NA_ROWS + np.arange(NA_ROWS)
        kr = kb + np.arange(NA_KROWS)
        r0 = np.clip(qr - NA_WIN_H // 2, 0, rows - NA_WIN_H)
        row_ok = (kr[None, :] >= r0[:, None]) & (kr[None, :] < r0[:, None] + NA_WIN_H)
        row_idx = np.clip(kr[None, :] - qr[:, None] + NA_WIN_H - 1, 0, 2 * NA_WIN_H - 2)
        ok = row_ok[:, None, :, None] & col_ok[None, :, None, :]
        ri = np.broadcast_to(row_idx[:, None, :, None], ok.shape)
        ci = np.broadcast_to(col_idx[None, :, None, :], ok.shape)
        t = jnp.where(ok[None], rpb[:, ri, ci], NEG_INF)
        tabs.append(t.reshape(rpb.shape[0], NA_Q, NA_K))
    return jnp.stack(tabs, axis=1)


def _neighborhood(proj, cproj, bias):
    b, n, _ = proj.shape
    n_ctx = cproj.shape[1]
    col = lambda c0: pl.BlockSpec((None, n, GROUP_W), lambda g, bb: (bb, 0, c0 + g))
    ccol = lambda c0: pl.BlockSpec((None, n_ctx, GROUP_W), lambda g, bb: (bb, 0, c0 + g))
    return pl.pallas_call(
        _na_kernel,
        grid=(N_GROUPS, b),
        in_specs=[col(COL_NQ), col(COL_NK), col(COL_NV), ccol(CCOL_NK), ccol(CCOL_NV),
                  pl.BlockSpec((HEAD_GROUP, 3, NA_Q, NA_K), lambda g, bb: (g, 0, 0, 0))],
        out_specs=pl.BlockSpec((None, n, GROUP_W), lambda g, bb: (bb, 0, g)),
        out_shape=jax.ShapeDtypeStruct((b, n, MIX_W), BF16),
        compiler_params=_cparams("arbitrary", "arbitrary"),
        name="neighborhood",
    )(proj, proj, proj, cproj, cproj, bias)


def _mix_ffn_kernel(x_ref, yr_ref, yn_ref, gr_ref, gn_ref, gt1_ref, sh2_ref, sc2_ref, gt2_ref,
                    g_post_mix_ref, g_pre_ffn_ref, g_post_ffn_ref,
                    w_ret_ref, w_na_ref, w_o_ref, w1_ref, w2_ref, o_ref):
    a = jnp.dot(yr_ref[...], w_ret_ref[...], preferred_element_type=F32)
    bb = jnp.dot(yn_ref[...], w_na_ref[...], preferred_element_type=F32)
    y = (jax.nn.sigmoid(gr_ref[...].astype(F32)) * a + jax.nn.sigmoid(gn_ref[...].astype(F32)) * bb)
    y = jnp.dot(y.astype(BF16), w_o_ref[...], preferred_element_type=F32)
    x1 = x_ref[...] + gt1_ref[...] * (_rms(y) * g_post_mix_ref[...])
    h2 = ((_rms(x1) * g_pre_ffn_ref[...]) * (1.0 + sc2_ref[...]) + sh2_ref[...]).astype(BF16)
    d_ff = w1_ref.shape[1]
    f = jnp.zeros(x1.shape, F32)
    for c0 in range(0, d_ff, FF_CHUNK):
        u = jnp.dot(h2, w1_ref[:, c0:c0 + FF_CHUNK], preferred_element_type=F32)
        u = jnp.square(jnp.maximum(u, 0.0)).astype(BF16)
        f = f + jnp.dot(u, w2_ref[c0:c0 + FF_CHUNK, :], preferred_element_type=F32)
    o_ref[...] = x1 + gt2_ref[...] * (_rms(f) * g_post_ffn_ref[...])


def _mix_ffn(x, y_ret, y_na, proj, mod4, g_post_mix, g_pre_ffn, g_post_ffn,
             w_ret, w_na, w_o, w1, w2):
    b, n, d = x.shape
    tm = MIX_TM
    tok = lambda w: pl.BlockSpec((None, tm, w), lambda i, bb: (bb, i, 0))
    mod_spec = lambda k: pl.BlockSpec((None, None, 1, d), lambda i, bb: (bb, k, 0, 0))
    return pl.pallas_call(
        _mix_ffn_kernel,
        grid=(n // tm, b),
        in_specs=[tok(d), tok(MIX_W), tok(MIX_W),
                  pl.BlockSpec((None, tm, d), lambda i, bb: (bb, i, 0)),
                  pl.BlockSpec((None, tm, d), lambda i, bb: (bb, i, 1)),
                  mod_spec(2), mod_spec(3), mod_spec(4), mod_spec(5),
                  _resident((1, d)), _resident((1, d)), _resident((1, d)),
                  _resident(w_ret.shape), _resident(w_na.shape), _resident(w_o.shape),
                  _resident(w1.shape), _resident(w2.shape)],
        out_specs=tok(d),
        out_shape=jax.ShapeDtypeStruct((b, n, d), F32),
        compiler_params=_cparams("arbitrary", "arbitrary"),
        name="mix_ffn",
    )(x, y_ret, y_na, proj, proj, mod4, mod4, mod4, mod4, g_post_mix, g_pre_ffn, g_post_ffn,
      w_ret, w_na, w_o, w1, w2)


def _rope_tables(n):
    pos = jnp.arange(n)
    row = (pos // GRID_W).astype(F32)
    colp = (pos % GRID_W).astype(F32)
    d_axis = HEAD_DIM // 2
    inv = ROPE_BASE ** (-jnp.arange(0, d_axis, 2, dtype=F32) / d_axis)
    ang = jnp.concatenate([row[:, None] * inv, colp[:, None] * inv], axis=-1)
    cos, sin = jnp.cos(ang), jnp.sin(ang)
    zero = jnp.zeros_like(sin)
    reps = LANES // HEAD_DIM
    cos_t = jnp.tile(jnp.concatenate([cos, cos], axis=-1), (1, reps))
    sin_lo = jnp.tile(jnp.concatenate([-sin, zero], axis=-1), (1, reps))
    sin_hi = jnp.tile(jnp.concatenate([zero, sin], axis=-1), (1, reps))
    return cos_t, sin_lo, sin_hi


def kernel(x, c, ctx, c_ctx, w_ada, b_ada, norm_pre_mix, norm_post_mix, norm_pre_ffn, norm_post_ffn,
           w_in, ret_decay_logit, w_ret_out, na_rpb, w_na_out, w_o, w_ff1, w_ff2):
    b, n, d = x.shape
    n_ctx = ctx.shape[1]
    assert w_ada.shape[0] == 1, "single-layer block"
    assert n % (GRID_W * NA_ROWS) == 0 and n % PROJ_TM == 0 and n % MIX_TM == 0 and n % RET_L == 0
    assert n_ctx % RET_L == 0 and d == D_MODEL and 7 * MIX_W + GATE_COLS == w_in.shape[2]

    pad = (-(b + 1)) % 8
    c_rows = jnp.concatenate([c, c_ctx[None, :], jnp.zeros((pad, d), F32)], axis=0)
    mod = _ada(c_rows, w_ada[0], b_ada[0])
    mod4 = mod.reshape(c_rows.shape[0], 6, 1, d)

    w_in_bf = w_in[0].astype(BF16)
    w_ctx_bf = jnp.concatenate([w_in_bf[:, MIX_W:3 * MIX_W], w_in_bf[:, 5 * MIX_W:7 * MIX_W]], axis=1)
    w_in_bf = jnp.concatenate([w_in_bf[:, 7 * MIX_W:], w_in_bf[:, :7 * MIX_W]], axis=1)
    g_pre_mix = norm_pre_mix[0].reshape(1, d)

    proj = _proj_latent(x, mod4, g_pre_mix, w_in_bf, _rope_tables(n))
    cproj = _proj_context(ctx.reshape(b * n_ctx, d), mod4, b, g_pre_mix, w_ctx_bf)
    cproj = cproj.reshape(b, n_ctx, 4 * MIX_W)

    logit = ret_decay_logit[0].astype(F32)
    y_ret = _retention(proj, cproj, jnp.repeat(logit, HEAD_DIM, axis=1), jnp.repeat(logit, RET_L, axis=1))

    bias = _na_bias_tables(na_rpb[0].astype(F32), n // GRID_W)
    y_na = _neighborhood(proj, cproj, bias)

    return _mix_ffn(x, y_ret, y_na, proj, mod4,
                    norm_post_mix[0].reshape(1, d), norm_pre_ffn[0].reshape(1, d),
                    norm_post_ffn[0].reshape(1, d),
                    w_ret_out[0].astype(BF16), w_na_out[0].astype(BF16), w_o[0].astype(BF16),
                    w_ff1[0].astype(BF16), w_ff2[0].astype(BF16))
```

```python
import functools
import math

import jax
import jax.numpy as jnp
import numpy as np
from jax import lax
from jax.experimental import pallas as pl
from jax.experimental.pallas import tpu as pltpu

F32 = jnp.float32
BF16 = jnp.bfloat16

D_MODEL = 1024
GRID_W = 64
HEADS = 8
HEAD_DIM = 64
MIX_W = HEADS * HEAD_DIM
NA_WIN_H = 8
NA_WIN_W = 16
ROPE_BASE = 10000.0
EPS = 1e-6
NEG_INF = -1e30
K_SCALE = HEAD_DIM ** -0.5
LOG2E = math.log2(math.e)

LANES = 128
V7X_VMEM_BYTES = 64 * 2 ** 20
VMEM_LIMIT = V7X_VMEM_BYTES - 8 * 2 ** 20

HEAD_GROUP = 4
GROUP_W = HEAD_GROUP * HEAD_DIM
N_GROUPS = HEADS // HEAD_GROUP
PROJ_TM = 512
MIX_TM = 512
FF_CHUNK = 1024
RET_L = 256
NA_ROWS = 4
NA_KROWS = 12
NA_Q = NA_ROWS * GRID_W
NA_K = NA_KROWS * GRID_W
RPB_ROWS = 2 * NA_WIN_H - 1
RPB_COLS = 2 * NA_WIN_W - 1

GATE_COLS = 2 * D_MODEL
COL_RQ, COL_RK, COL_RV, COL_RG, COL_NQ, COL_NK = (GATE_COLS // GROUP_W + i * N_GROUPS for i in range(6))
PROJ_COLS = GATE_COLS + 6 * MIX_W
CCOL_RK, CCOL_RV, CCOL_NK, CCOL_NV = (i * N_GROUPS for i in range(4))

_NT = (((1,), (1,)), ((), ()))
_TN = (((0,), (0,)), ((), ()))


def _cparams(*sem):
    return pltpu.CompilerParams(dimension_semantics=sem, vmem_limit_bytes=VMEM_LIMIT)


def _resident(shape):
    nd = len(shape)
    return pl.BlockSpec(shape, lambda *_: (0,) * nd, pipeline_mode=pl.Buffered(1))


def _rms(x):
    return x * lax.rsqrt(jnp.mean(x * x, axis=-1, keepdims=True) + EPS)


def _silu(x):
    return x * jax.nn.sigmoid(x)


def _head_masks():
    lane_head = lax.broadcasted_iota(jnp.int32, (1, GROUP_W), 1) // HEAD_DIM
    sel = [lane_head == h for h in range(HEAD_GROUP)]
    return sel, [m.astype(BF16) for m in sel]


def _ada_kernel(c_ref, w_ref, b_ref, o_ref):
    a = _silu(c_ref[...])
    o_ref[...] = jnp.dot(a, w_ref[...], preferred_element_type=F32,
                         precision=lax.Precision.HIGHEST) + b_ref[...]


def _ada(c_rows, w_ada, b_ada):
    rows, d = c_rows.shape
    cols = w_ada.shape[1]
    tn = d
    return pl.pallas_call(
        _ada_kernel,
        grid=(cols // tn,),
        in_specs=[pl.BlockSpec((rows, d), lambda j: (0, 0)),
                  pl.BlockSpec((d, tn), lambda j: (0, j)),
                  pl.BlockSpec((1, tn), lambda j: (0, j))],
        out_specs=pl.BlockSpec((rows, tn), lambda j: (0, j)),
        out_shape=jax.ShapeDtypeStruct((rows, cols), F32),
        compiler_params=_cparams("arbitrary"),
        name="ada_mod",
    )(c_rows, w_ada, b_ada.reshape(1, cols))


def _rope_slab(t, cos, sin_lo, sin_hi):
    return (t * cos + pltpu.roll(t, 3 * LANES // 4, axis=1) * sin_lo
            + pltpu.roll(t, LANES // 4, axis=1) * sin_hi)


def _modulated_norm(x_ref, sh_ref, sc_ref, g_ref):
    return ((_rms(x_ref[...]) * g_ref[...]) * (1.0 + sc_ref[...]) + sh_ref[...]).astype(BF16)


def _proj_latent_kernel(x_ref, sh_ref, sc_ref, g_ref, w_ref, wvt_ref, cos_ref, slo_ref, shi_ref,
                        o_ref, vt_ref):
    h = _modulated_norm(x_ref, sh_ref, sc_ref, g_ref)
    rope_lo, rope_hi = COL_RQ * GROUP_W, COL_RV * GROUP_W
    col_scale = {COL_RK * GROUP_W: K_SCALE, COL_NQ * GROUP_W: K_SCALE * LOG2E}
    for c0 in range(0, PROJ_COLS, MIX_W):
        r = jnp.dot(h, w_ref[:, c0:c0 + MIX_W], preferred_element_type=F32)
        if rope_lo <= c0 < rope_hi:
            cos, slo, shi = cos_ref[...], slo_ref[...], shi_ref[...]
            r = jnp.concatenate(
                [_rope_slab(r[:, j:j + LANES], cos, slo, shi) for j in range(0, MIX_W, LANES)], axis=1)
        if c0 in col_scale:
            r = r * col_scale[c0]
        o_ref[:, c0:c0 + MIX_W] = r.astype(BF16)
    vt_ref[...] = lax.dot_general(wvt_ref[...], h, _NT, preferred_element_type=F32).astype(BF16)


def _proj_latent(x, mod4, g, w_bf, wvt_bf, rope_tabs):
    b, n, d = x.shape
    tm = PROJ_TM
    mod_spec = lambda k: pl.BlockSpec((None, None, 1, d), lambda i, bb: (bb, k, 0, 0))
    tab_spec = pl.BlockSpec((tm, LANES), lambda i, bb: (i, 0))
    return pl.pallas_call(
        _proj_latent_kernel,
        grid=(n // tm, b),
        in_specs=[pl.BlockSpec((None, tm, d), lambda i, bb: (bb, i, 0)),
                  mod_spec(0), mod_spec(1), _resident((1, d)), _resident(w_bf.shape),
                  _resident(wvt_bf.shape), tab_spec, tab_spec, tab_spec],
        out_specs=[pl.BlockSpec((None, tm, PROJ_COLS), lambda i, bb: (bb, i, 0)),
                   pl.BlockSpec((None, MIX_W, tm), lambda i, bb: (bb, 0, i))],
        out_shape=[jax.ShapeDtypeStruct((b, n, PROJ_COLS), BF16),
                   jax.ShapeDtypeStruct((b, MIX_W, n), BF16)],
        compiler_params=_cparams("arbitrary", "arbitrary"),
        name="proj_latent",
    )(x, mod4, mod4, g, w_bf, wvt_bf, *rope_tabs)


def _proj_context_kernel(x_ref, sh_ref, sc_ref, g_ref, w_ref, o_ref):
    h = _modulated_norm(x_ref, sh_ref, sc_ref, g_ref)
    for c0 in range(0, o_ref.shape[-1], MIX_W):
        r = jnp.dot(h, w_ref[:, c0:c0 + MIX_W], preferred_element_type=F32)
        if c0 == CCOL_RK * GROUP_W:
            r = r * K_SCALE
        o_ref[:, c0:c0 + MIX_W] = r.astype(BF16)


def _proj_context(ctx_rows, mod4, ctx_row, g, w_bf):
    rows, d = ctx_rows.shape
    cols = w_bf.shape[1]
    tm = min(PROJ_TM, rows)
    mod_spec = lambda k: pl.BlockSpec((None, None, 1, d), lambda i: (ctx_row, k, 0, 0))
    return pl.pallas_call(
        _proj_context_kernel,
        grid=(rows // tm,),
        in_specs=[pl.BlockSpec((tm, d), lambda i: (i, 0)),
                  mod_spec(0), mod_spec(1), _resident((1, d)), _resident((d, cols))],
        out_specs=pl.BlockSpec((tm, cols), lambda i: (i, 0)),
        out_shape=jax.ShapeDtypeStruct((rows, cols), BF16),
        compiler_params=_cparams("arbitrary"),
        name="proj_context",
    )(ctx_rows, mod4, mod4, g, w_bf)


def _log_sigmoid(x):
    return jnp.minimum(x, 0.0) - jnp.log1p(jnp.exp(-jnp.abs(x)))


def _ret_kernel(q_ref, k_ref, v_ref, rg_ref, ck_ref, cv_ref, lg_lane_ref, lg_chunk_ref, o_ref,
                acc_ref, dec_ref, sf_ref, sb_ref):
    L = RET_L
    n = q_ref.shape[0]
    n_ctx = ck_ref.shape[0]
    nc, ncc = n // L, n_ctx // L

    lg = _log_sigmoid(lg_lane_ref[...])
    lgf, lgb = lg[0:1, :], lg[1:2, :]
    pos = lax.broadcasted_iota(jnp.int32, (L, 1), 0).astype(F32)
    qw_f = jnp.exp(lgf * (pos + 1.0))
    kw_f = jnp.exp(lgf * (L - 1.0 - pos))
    qw_b = jnp.exp(lgb * (L - pos))
    kw_b = jnp.exp(lgb * pos)
    gl_f = jnp.exp(lgf * float(L))
    gl_b = jnp.exp(lgb * float(L))

    lane_head = lax.broadcasted_iota(jnp.int32, (1, GROUP_W), 1) // HEAD_DIM
    row_head = lax.broadcasted_iota(jnp.int32, (GROUP_W, 1), 0) // HEAD_DIM
    block_diag = row_head == lane_head
    head_mean = jnp.where(block_diag, 1.0 / HEAD_DIM, 0.0).astype(BF16)
    _, head_mask = _head_masks()

    lgc = _log_sigmoid(lg_chunk_ref[...])
    diff = pos - lax.broadcasted_iota(jnp.int32, (1, L), 1).astype(F32)
    for h in range(HEAD_GROUP):
        f = jnp.exp(lgc[0:1, h * L:(h + 1) * L] * jnp.maximum(diff, 0.0))
        bk = jnp.exp(lgc[1:2, h * L:(h + 1) * L] * jnp.maximum(-diff, 0.0))
        dec_ref[h * L:(h + 1) * L, :] = jnp.where(diff >= 0.0, f, bk)

    def state_update(s_ref, k, v, kw, gl):
        kd = (k.astype(F32) * kw).astype(BF16)
        contrib = lax.dot_general(kd, v, _TN, preferred_element_type=F32)
        s_ref[...] = s_ref[...] * gl + jnp.where(block_diag, contrib, 0.0)

    sf_ref[...] = jnp.zeros_like(sf_ref)
    sb_ref[...] = jnp.zeros_like(sb_ref)

    def ctx_fwd(c, carry):
        rows = pl.ds(pl.multiple_of(c * L, L), L)
        state_update(sf_ref, ck_ref[rows, :], cv_ref[rows, :], kw_f, gl_f)
        return carry

    def ctx_bwd(i, carry):
        rows = pl.ds(pl.multiple_of((ncc - 1 - i) * L, L), L)
        state_update(sb_ref, ck_ref[rows, :], cv_ref[rows, :], kw_b, gl_b)
        return carry

    lax.fori_loop(0, ncc, ctx_fwd, 0)
    lax.fori_loop(0, ncc, ctx_bwd, 0)

    def fwd_chunk(c, carry):
        rows = pl.ds(pl.multiple_of(c * L, L), L)
        q, k, v = q_ref[rows, :], k_ref[rows, :], v_ref[rows, :]
        q_heads = jnp.concatenate([q * head_mask[h] for h in range(HEAD_GROUP)], axis=0)
        s = lax.dot_general(q_heads, k, _NT, preferred_element_type=F32)
        s = (s * dec_ref[...]).astype(BF16)
        s_cat = jnp.concatenate([s[h * L:(h + 1) * L, :] for h in range(HEAD_GROUP)], axis=1)
        v_heads = jnp.concatenate([v * head_mask[h] for h in range(HEAD_GROUP)], axis=0)
        o = jnp.dot(s_cat, v_heads, preferred_element_type=F32)
        o = o + jnp.dot(q, sf_ref[...].astype(BF16), preferred_element_type=F32) * qw_f
        acc_ref[rows, :] = o
        state_update(sf_ref, k, v, kw_f, gl_f)
        return carry

    lax.fori_loop(0, nc, fwd_chunk, 0)

    def bwd_chunk(i, carry):
        rows = pl.ds(pl.multiple_of((nc - 1 - i) * L, L), L)
        q, k, v = q_ref[rows, :], k_ref[rows, :], v_ref[rows, :]
        o = acc_ref[rows, :] + jnp.dot(q, sb_ref[...].astype(BF16), preferred_element_type=F32) * qw_b
        sq = o * o
        sq_hi = sq.astype(BF16)
        sq_lo = (sq - sq_hi.astype(F32)).astype(BF16)
        ms = (jnp.dot(sq_hi, head_mean, preferred_element_type=F32)
              + jnp.dot(sq_lo, head_mean, preferred_element_type=F32))
        y = o * lax.rsqrt(ms + EPS) * _silu(rg_ref[rows, :].astype(F32))
        o_ref[rows, :] = y.astype(BF16)
        state_update(sb_ref, k, v, kw_b, gl_b)
        return carry

    lax.fori_loop(0, nc, bwd_chunk, 0)


def _retention(proj, cproj, lg_lane, lg_chunk):
    b, n, _ = proj.shape
    n_ctx = cproj.shape[1]
    L = RET_L
    col = lambda c0: pl.BlockSpec((None, n, GROUP_W), lambda bb, g: (bb, 0, c0 + g))
    ccol = lambda c0: pl.BlockSpec((None, n_ctx, GROUP_W), lambda bb, g: (bb, 0, c0 + g))
    return pl.pallas_call(
        _ret_kernel,
        grid=(b, N_GROUPS),
        in_specs=[col(COL_RQ), col(COL_RK), col(COL_RV), col(COL_RG), ccol(CCOL_RK), ccol(CCOL_RV),
                  pl.BlockSpec((2, GROUP_W), lambda bb, g: (0, g)),
                  pl.BlockSpec((2, HEAD_GROUP * L), lambda bb, g: (0, g))],
        out_specs=pl.BlockSpec((None, n, GROUP_W), lambda bb, g: (bb, 0, g)),
        out_shape=jax.ShapeDtypeStruct((b, n, MIX_W), BF16),
        scratch_shapes=[pltpu.VMEM((n, GROUP_W), F32),
                        pltpu.VMEM((HEAD_GROUP * L, L), F32),
                        pltpu.VMEM((GROUP_W, GROUP_W), F32),
                        pltpu.VMEM((GROUP_W, GROUP_W), F32)],
        compiler_params=_cparams("arbitrary", "arbitrary"),
        name="retention",
    )(proj, proj, proj, proj, cproj, cproj, lg_lane, lg_chunk)


def _na_patterns(rows):
    nblk = rows // NA_ROWS
    assert rows % NA_ROWS == 0 and nblk >= 3 and rows >= NA_KROWS
    pats = []
    for qb in (0, 1, nblk - 1):
        kb = _na_key_start(qb, rows)
        qr = qb * NA_ROWS + np.arange(NA_ROWS)
        r0 = np.clip(qr - NA_WIN_H // 2, 0, rows - NA_WIN_H)
        assert np.all(r0 >= kb) and np.all(r0 + NA_WIN_H <= kb + NA_KROWS)
        pats.append((tuple(int(v) for v in r0 - kb), int(kb - qb * NA_ROWS + NA_WIN_H - 1)))
    for qb in range(1, nblk - 1):
        kb = _na_key_start(qb, rows)
        qr = qb * NA_ROWS + np.arange(NA_ROWS)
        r0 = np.clip(qr - NA_WIN_H // 2, 0, rows - NA_WIN_H)
        assert (tuple(int(v) for v in r0 - kb), int(kb - qb * NA_ROWS + NA_WIN_H - 1)) == pats[1]
    return pats


def _na_key_start(qb, rows):
    assert (NA_WIN_H // 2) % NA_ROWS == 0 and NA_KROWS % NA_ROWS == 0
    return min(max(qb * NA_ROWS - NA_WIN_H // 2, 0), rows - NA_KROWS)


def _na_slot_range(pats):
    lo = min(off - (NA_ROWS - 1) for _, off in pats)
    hi = max(off + NA_KROWS - 1 for _, off in pats)
    return lo, hi


def _na_kernel(q_ref, k_ref, vt_ref, ck_ref, cv_ref, tz_ref, o_ref, bias_ref, cvt_ref, *, pats):
    n = q_ref.shape[0]
    rows = n // GRID_W
    nblk = rows // NA_ROWS
    _, head_mask = _head_masks()
    _, dr_hi = _na_slot_range(pats)

    @pl.when(pl.program_id(1) == 0)
    def _build_bias():
        key = lax.broadcasted_iota(jnp.int32, (NA_K, NA_Q), 0)
        qry = lax.broadcasted_iota(jnp.int32, (NA_K, NA_Q), 1)
        j, kc = key // GRID_W, key % GRID_W
        i, qc = qry // GRID_W, qry % GRID_W
        c0 = jnp.clip(qc - NA_WIN_W // 2, 0, GRID_W - NA_WIN_W)
        col_ok = (kc >= c0) & (kc < c0 + NA_WIN_W)
        for p, (r0, off) in enumerate(pats):
            lo = jnp.zeros_like(i)
            for ii in range(NA_ROWS):
                lo = jnp.where(i == ii, r0[ii], lo)
            ok = col_ok & (j >= lo) & (j < lo + NA_WIN_H)
            for h in range(HEAD_GROUP):
                t = tz_ref[h]
                for jj in range(NA_KROWS):
                    slot = dr_hi - (jj + off)
                    tile = t[:, slot * GRID_W:slot * GRID_W + NA_Q] * LOG2E
                    rs = slice(jj * GRID_W, (jj + 1) * GRID_W)
                    bias_ref[h, p, rs, :] = jnp.where(ok[rs, :], tile, NEG_INF)

    cvt_ref[...] = cv_ref[...].astype(F32).T.astype(BF16)

    def block(qb, carry):
        kb = jnp.clip(qb * NA_ROWS - NA_WIN_H // 2, 0, rows - NA_KROWS)
        pat =jnp.where(qb == 0, 0, jnp.where(qb == nblk - 1, 2, 1))
        q_rows = pl.ds(pl.multiple_of(qb * NA_Q, NA_Q), NA_Q)
        k_tok = pl.ds(pl.multiple_of(kb * GRID_W, NA_Q), NA_K)
        q = q_ref[q_rows, :]
        kl, kc = k_ref[k_tok, :], ck_ref[...]
        outs = []
        for h in range(HEAD_GROUP):
            qh = q * head_mask[h]
            s_loc = lax.dot_general(kl, qh, _NT, preferred_element_type=F32) + bias_ref[h, pat]
            s_ctx = lax.dot_general(kc, qh, _NT, preferred_element_type=F32)
            m = jnp.maximum(jnp.max(s_loc, axis=0, keepdims=True), jnp.max(s_ctx, axis=0, keepdims=True))
            p_loc = jnp.exp2(s_loc - m)
            p_ctx = jnp.exp2(s_ctx - m)
            denom = jnp.sum(p_loc, axis=0, keepdims=True) + jnp.sum(p_ctx, axis=0, keepdims=True)
            hs = slice(h * HEAD_DIM, (h + 1) * HEAD_DIM)
            oh = (jnp.dot(vt_ref[hs, k_tok], p_loc.astype(BF16), preferred_element_type=F32)
                  + jnp.dot(cvt_ref[hs, :], p_ctx.astype(BF16), preferred_element_type=F32))
            outs.append(oh / denom)
        o_ref[q_rows, :] = jnp.concatenate(outs, axis=0).T.astype(BF16)
        return carry

    lax.fori_loop(0, nblk, block, 0)


def _na_toeplitz(rpb, pats):
    dr_lo, dr_hi = _na_slot_range(pats)
    slots = dr_hi - dr_lo + 1
    slots += (-slots) % (LANES // GRID_W)
    row_sel = np.zeros((RPB_ROWS, slots), np.float32)
    for s in range(slots):
        if 0 <= dr_hi - s < RPB_ROWS:
            row_sel[dr_hi - s, s] = 1.0
    col = np.arange(GRID_W)
    dc = col[:, None] - col[None, :] + NA_WIN_W - 1
    col_sel = (dc[None, :, :] == np.arange(RPB_COLS)[:, None, None]).astype(np.float32)
    t = jnp.einsum('hab,as,bkq->hksq', rpb, row_sel, col_sel, precision=lax.Precision.HIGHEST)
    return t.reshape(rpb.shape[0], GRID_W, slots * GRID_W)


def _neighborhood(proj, vt, cproj, tz, pats):
    b, n, _ = proj.shape
    n_ctx = cproj.shape[1]
    col = lambda c0: pl.BlockSpec((None, n, GROUP_W), lambda g, bb: (bb, 0, c0 + g))
    ccol = lambda c0: pl.BlockSpec((None, n_ctx, GROUP_W), lambda g, bb: (bb, 0, c0 + g))
    return pl.pallas_call(
        functools.partial(_na_kernel, pats=pats),
        grid=(N_GROUPS, b),
        in_specs=[col(COL_NQ), col(COL_NK),
                  pl.BlockSpec((None, GROUP_W, n), lambda g, bb: (bb, g, 0)),
                  ccol(CCOL_NK), ccol(CCOL_NV),
                  pl.BlockSpec((HEAD_GROUP,) + tz.shape[1:], lambda g, bb: (g, 0, 0))],
        out_specs=pl.BlockSpec((None, n, GROUP_W), lambda g, bb: (bb, 0, g)),
        out_shape=jax.ShapeDtypeStruct((b, n, MIX_W), BF16),
        scratch_shapes=[pltpu.VMEM((HEAD_GROUP, len(pats), NA_K, NA_Q), F32),
                        pltpu.VMEM((GROUP_W, n_ctx), BF16)],
        compiler_params=_cparams("arbitrary", "arbitrary"),
        name="neighborhood",
    )(proj, proj, vt, cproj, cproj, tz)


def _mix_ffn_kernel(x_ref, yr_ref, yn_ref, gr_ref, gn_ref, gt1_ref, sh2_ref, sc2_ref, gt2_ref,
                    g_post_mix_ref, g_pre_ffn_ref, g_post_ffn_ref,
                    w_ret_ref, w_na_ref, w_o_ref, w1_ref, w2_ref, o_ref):
    a = jnp.dot(yr_ref[...], w_ret_ref[...], preferred_element_type=F32)
    bb = jnp.dot(yn_ref[...], w_na_ref[...], preferred_element_type=F32)
    y = (jax.nn.sigmoid(gr_ref[...].astype(F32)) * a + jax.nn.sigmoid(gn_ref[...].astype(F32)) * bb)
    y = jnp.dot(y.astype(BF16), w_o_ref[...], preferred_element_type=F32)
    x1 = x_ref[...] + gt1_ref[...] * (_rms(y) * g_post_mix_ref[...])
    h2 = ((_rms(x1) * g_pre_ffn_ref[...]) * (1.0 + sc2_ref[...]) + sh2_ref[...]).astype(BF16)
    d_ff = w1_ref.shape[1]
    f = jnp.zeros(x1.shape, F32)
    for c0 in range(0, d_ff, FF_CHUNK):
        u = jnp.dot(h2, w1_ref[:, c0:c0 + FF_CHUNK], preferred_element_type=F32)
        u = jnp.square(jnp.maximum(u, 0.0)).astype(BF16)
        f = f + jnp.dot(u, w2_ref[c0:c0 + FF_CHUNK, :], preferred_element_type=F32)
    o_ref[...] = x1 + gt2_ref[...] * (_rms(f) * g_post_ffn_ref[...])


def _mix_ffn(x, y_ret, y_na, proj, mod4, g_post_mix, g_pre_ffn, g_post_ffn,
             w_ret, w_na, w_o, w1, w2):
    b, n, d = x.shape
    tm = MIX_TM
    tok = lambda w: pl.BlockSpec((None, tm, w), lambda i, bb: (bb, i, 0))
    mod_spec = lambda k: pl.BlockSpec((None, None, 1, d), lambda i, bb: (bb, k, 0, 0))
    return pl.pallas_call(
        _mix_ffn_kernel,
        grid=(n // tm, b),
        in_specs=[tok(d), tok(MIX_W), tok(MIX_W),
                  pl.BlockSpec((None, tm, d), lambda i, bb: (bb, i, 0)),
                  pl.BlockSpec((None, tm, d), lambda i, bb: (bb, i, 1)),
                  mod_spec(2), mod_spec(3), mod_spec(4), mod_spec(5),
                  _resident((1, d)), _resident((1, d)), _resident((1, d)),
                  _resident(w_ret.shape), _resident(w_na.shape), _resident(w_o.shape),
                  _resident(w1.shape), _resident(w2.shape)],
        out_specs=tok(d),
        out_shape=jax.ShapeDtypeStruct((b, n, d), F32),
        compiler_params=_cparams("arbitrary", "arbitrary"),
        name="mix_ffn",
    )(x, y_ret, y_na, proj, proj, mod4, mod4, mod4, mod4, g_post_mix, g_pre_ffn, g_post_ffn,
      w_ret, w_na, w_o, w1, w2)


def _rope_tables(n):
    pos = jnp.arange(n)
    row = (pos // GRID_W).astype(F32)
    colp = (pos % GRID_W).astype(F32)
    d_axis = HEAD_DIM // 2
    inv = ROPE_BASE ** (-jnp.arange(0, d_axis, 2, dtype=F32) / d_axis)
    ang = jnp.concatenate([row[:, None] * inv, colp[:, None] * inv], axis=-1)
    cos, sin = jnp.cos(ang), jnp.sin(ang)
    zero = jnp.zeros_like(sin)
    reps = LANES // HEAD_DIM
    cos_t = jnp.tile(jnp.concatenate([cos, cos], axis=-1), (1, reps))
    sin_lo = jnp.tile(jnp.concatenate([-sin, zero], axis=-1), (1, reps))
    sin_hi = jnp.tile(jnp.concatenate([zero, sin], axis=-1), (1, reps))
    return cos_t, sin_lo, sin_hi


def kernel(x, c, ctx, c_ctx, w_ada, b_ada, norm_pre_mix, norm_post_mix, norm_pre_ffn, norm_post_ffn,
           w_in, ret_decay_logit, w_ret_out, na_rpb, w_na_out, w_o, w_ff1, w_ff2):
    b, n, d = x.shape
    n_ctx = ctx.shape[1]
    assert w_ada.shape[0] == 1, "single-layer block"
    assert n % NA_Q == 0 and n % PROJ_TM == 0 and n % MIX_TM == 0 and n % RET_L == 0
    assert n_ctx % RET_L == 0 and d == D_MODEL and 7 * MIX_W + GATE_COLS == w_in.shape[2]

    pad = (-(b + 1)) % 8
    c_rows = jnp.concatenate([c, c_ctx[None, :], jnp.zeros((pad, d), F32)], axis=0)
    mod = _ada(c_rows, w_ada[0], b_ada[0])
    mod4 = mod.reshape(c_rows.shape[0], 6, 1, d)

    w_all = w_in[0].astype(BF16)
    w_ctx_bf = jnp.concatenate([w_all[:, MIX_W:3 * MIX_W], w_all[:, 5 * MIX_W:7 * MIX_W]], axis=1)
    w_lat_bf = jnp.concatenate([w_all[:, 7 * MIX_W:], w_all[:, :6 * MIX_W]], axis=1)
    w_nvt_bf = w_all[:, 6 * MIX_W:7 * MIX_W].T
    g_pre_mix = norm_pre_mix[0].reshape(1, d)

    proj, nvt = _proj_latent(x, mod4, g_pre_mix, w_lat_bf, w_nvt_bf, _rope_tables(n))
    cproj = _proj_context(ctx.reshape(b * n_ctx, d), mod4, b, g_pre_mix, w_ctx_bf)
    cproj = cproj.reshape(b, n_ctx, 4 * MIX_W)

    logit = ret_decay_logit[0].astype(F32)
    y_ret = _retention(proj, cproj, jnp.repeat(logit, HEAD_DIM, axis=1), jnp.repeat(logit, RET_L, axis=1))

    pats = _na_patterns(n // GRID_W)
    y_na = _neighborhood(proj, nvt, cproj, _na_toeplitz(na_rpb[0].astype(F32), pats), pats)

    return _mix_ffn(x, y_ret, y_na, proj, mod4,
                    norm_post_mix[0].reshape(1, d), norm_pre_ffn[0].reshape(1, d),
                    norm_post_ffn[0].reshape(1, d),
                    w_ret_out[0].astype(BF16), w_na_out[0].astype(BF16), w_o[0].astype(BF16),
                    w_ff1[0].astype(BF16), w_ff2[0].astype(BF16))
```

```python
import functools
import math

import jax
import jax.numpy as jnp
import numpy as np
from jax import lax
from jax.experimental import pallas as pl
from jax.experimental.pallas import tpu as pltpu

F32 = jnp.float32
BF16 = jnp.bfloat16

D_MODEL = 1024
GRID_W = 64
HEADS = 8
HEAD_DIM = 64
MIX_W = HEADS * HEAD_DIM
NA_WIN_H = 8
NA_WIN_W = 16
ROPE_BASE = 10000.0
EPS = 1e-6
NEG_INF = -1e30
K_SCALE = HEAD_DIM ** -0.5
LOG2E = math.log2(math.e)

LANES = 128
V7X_VMEM_BYTES = 64 * 2 ** 20
VMEM_LIMIT = V7X_VMEM_BYTES - 8 * 2 ** 20

HEAD_GROUP = 4
GROUP_W = HEAD_GROUP * HEAD_DIM
N_GROUPS = HEADS // HEAD_GROUP
PROJ_TM = 512
MIX_TM = 512
FF_CHUNK = 1024
RET_L = 256
NA_ROWS = 4
NA_KROWS = 12
NA_Q = NA_ROWS * GRID_W
NA_K = NA_KROWS * GRID_W
RPB_ROWS = 2 * NA_WIN_H - 1
RPB_COLS = 2 * NA_WIN_W - 1

GATE_COLS = 2 * D_MODEL
COL_RQ, COL_RK, COL_RV, COL_RG, COL_NQ, COL_NK = (GATE_COLS // GROUP_W + i * N_GROUPS for i in range(6))
PROJ_COLS = GATE_COLS + 6 * MIX_W
CCOL_RK, CCOL_RV, CCOL_NK, CCOL_NV = (i * N_GROUPS for i in range(4))

_NT = (((1,), (1,)), ((), ()))
_TN = (((0,), (0,)), ((), ()))


def _cparams(*sem):
    return pltpu.CompilerParams(dimension_semantics=sem, vmem_limit_bytes=VMEM_LIMIT)


def _resident(shape):
    nd = len(shape)
    return pl.BlockSpec(shape, lambda *_: (0,) * nd, pipeline_mode=pl.Buffered(1))


def _rms(x):
    return x * lax.rsqrt(jnp.mean(x * x, axis=-1, keepdims=True) + EPS)


def _silu(x):
    return x * jax.nn.sigmoid(x)


def _head_masks():
    lane_head = lax.broadcasted_iota(jnp.int32, (1, GROUP_W), 1) // HEAD_DIM
    sel = [lane_head == h for h in range(HEAD_GROUP)]
    return sel, [m.astype(BF16) for m in sel]


def _ada_kernel(c_ref, w_ref, b_ref, o_ref):
    a = _silu(c_ref[...])
    o_ref[...] = jnp.dot(a, w_ref[...], preferred_element_type=F32,
                         precision=lax.Precision.HIGHEST) + b_ref[...]


def _ada(c_rows, w_ada, b_ada):
    rows, d = c_rows.shape
    cols = w_ada.shape[1]
    tn = d
    return pl.pallas_call(
        _ada_kernel,
        grid=(cols // tn,),
        in_specs=[pl.BlockSpec((rows, d), lambda j: (0, 0)),
                  pl.BlockSpec((d, tn), lambda j: (0, j)),
                  pl.BlockSpec((1, tn), lambda j: (0, j))],
        out_specs=pl.BlockSpec((rows, tn), lambda j: (0, j)),
        out_shape=jax.ShapeDtypeStruct((rows, cols), F32),
        compiler_params=_cparams("arbitrary"),
        name="ada_mod",
    )(c_rows, w_ada, b_ada.reshape(1, cols))


def _rope_slab(t, cos, sin_lo, sin_hi):
    return (t * cos + pltpu.roll(t, 3 * LANES // 4, axis=1) * sin_lo
            + pltpu.roll(t, LANES // 4, axis=1) * sin_hi)


def _modulated_norm(x_ref, sh_ref, sc_ref, g_ref):
    return ((_rms(x_ref[...]) * g_ref[...]) * (1.0 + sc_ref[...]) + sh_ref[...]).astype(BF16)


def _proj_latent_kernel(x_ref, sh_ref, sc_ref, g_ref, w_ref, wvt_ref, cos_ref, slo_ref, shi_ref,
                        o_ref, vt_ref):
    h = _modulated_norm(x_ref, sh_ref, sc_ref, g_ref)
    rope_lo, rope_hi = COL_RQ * GROUP_W, COL_RV * GROUP_W
    col_scale = {COL_RK * GROUP_W: K_SCALE, COL_NQ * GROUP_W: K_SCALE * LOG2E}
    for c0 in range(0, PROJ_COLS, MIX_W):
        r = jnp.dot(h, w_ref[:, c0:c0 + MIX_W], preferred_element_type=F32)
        if rope_lo <= c0 < rope_hi:
            cos, slo, shi = cos_ref[...], slo_ref[...], shi_ref[...]
            r = jnp.concatenate(
                [_rope_slab(r[:, j:j + LANES], cos, slo, shi) for j in range(0, MIX_W, LANES)], axis=1)
        if c0 in col_scale:
            r = r * col_scale[c0]
        o_ref[:, c0:c0 + MIX_W] = r.astype(BF16)
    vt_ref[...] = lax.dot_general(wvt_ref[...], h, _NT, preferred_element_type=F32).astype(BF16)


def _proj_latent(x, mod4, g, w_bf, wvt_bf, rope_tabs):
    b, n, d = x.shape
    tm = PROJ_TM
    mod_spec = lambda k: pl.BlockSpec((None, None, 1, d), lambda i, bb: (bb, k, 0, 0))
    tab_spec = pl.BlockSpec((tm, LANES), lambda i, bb: (i, 0))
    return pl.pallas_call(
        _proj_latent_kernel,
        grid=(n // tm, b),
        in_specs=[pl.BlockSpec((None, tm, d), lambda i, bb: (bb, i, 0)),
                  mod_spec(0), mod_spec(1), _resident((1, d)), _resident(w_bf.shape),
                  _resident(wvt_bf.shape), tab_spec, tab_spec, tab_spec],
        out_specs=[pl.BlockSpec((None, tm, PROJ_COLS), lambda i, bb: (bb, i, 0)),
                   pl.BlockSpec((None, MIX_W, tm), lambda i, bb: (bb, 0, i))],
        out_shape=[jax.ShapeDtypeStruct((b, n, PROJ_COLS), BF16),
                   jax.ShapeDtypeStruct((b, MIX_W, n), BF16)],
        compiler_params=_cparams("arbitrary", "arbitrary"),
        name="proj_latent",
    )(x, mod4, mod4, g, w_bf, wvt_bf, *rope_tabs)


def _proj_context_kernel(x_ref, sh_ref, sc_ref, g_ref, w_ref, o_ref):
    h = _modulated_norm(x_ref, sh_ref, sc_ref, g_ref)
    for c0 in range(0, o_ref.shape[-1], MIX_W):
        r = jnp.dot(h, w_ref[:, c0:c0 + MIX_W], preferred_element_type=F32)
        if c0 == CCOL_RK * GROUP_W:
            r = r * K_SCALE
        o_ref[:, c0:c0 + MIX_W] = r.astype(BF16)


def _proj_context(ctx_rows, mod4, ctx_row, g, w_bf):
    rows, d = ctx_rows.shape
    cols = w_bf.shape[1]
    tm = min(PROJ_TM, rows)
    mod_spec = lambda k: pl.BlockSpec((None, None, 1, d), lambda i: (ctx_row, k, 0, 0))
    return pl.pallas_call(
        _proj_context_kernel,
        grid=(rows // tm,),
        in_specs=[pl.BlockSpec((tm, d), lambda i: (i, 0)),
                  mod_spec(0), mod_spec(1), _resident((1, d)), _resident((d, cols))],
        out_specs=pl.BlockSpec((tm, cols), lambda i: (i, 0)),
        out_shape=jax.ShapeDtypeStruct((rows, cols), BF16),
        compiler_params=_cparams("arbitrary"),
        name="proj_context",
    )(ctx_rows, mod4, mod4, g, w_bf)


def _log_sigmoid(x):
    return jnp.minimum(x, 0.0) - jnp.log1p(jnp.exp(-jnp.abs(x)))


def _ret_kernel(q_ref, k_ref, v_ref, rg_ref, ck_ref, cv_ref, lg_lane_ref, lg_chunk_ref, o_ref,
                acc_ref, dec_ref, sf_ref, sb_ref):
    L = RET_L
    n = q_ref.shape[0]
    n_ctx = ck_ref.shape[0]
    nc, ncc = n // L, n_ctx // L

    lg = _log_sigmoid(lg_lane_ref[...])
    lgf, lgb = lg[0:1, :], lg[1:2, :]
    pos = lax.broadcasted_iota(jnp.int32, (L, 1), 0).astype(F32)
    qw_f = jnp.exp(lgf * (pos + 1.0))
    kw_f = jnp.exp(lgf * (L - 1.0 - pos))
    qw_b = jnp.exp(lgb * (L - pos))
    kw_b = jnp.exp(lgb * pos)
    gl_f = jnp.exp(lgf * float(L))
    gl_b = jnp.exp(lgb * float(L))

    lane_head = lax.broadcasted_iota(jnp.int32, (1, GROUP_W), 1) // HEAD_DIM
    row_head = lax.broadcasted_iota(jnp.int32, (GROUP_W, 1), 0) // HEAD_DIM
    block_diag = row_head == lane_head
    head_mean = jnp.where(block_diag, 1.0 / HEAD_DIM, 0.0).astype(BF16)
    _, head_mask = _head_masks()

    lgc = _log_sigmoid(lg_chunk_ref[...])
    diff = pos - lax.broadcasted_iota(jnp.int32, (1, L), 1).astype(F32)
    for h in range(HEAD_GROUP):
        f = jnp.exp(lgc[0:1, h * L:(h + 1) * L] * jnp.maximum(diff, 0.0))
        bk = jnp.exp(lgc[1:2, h * L:(h + 1) * L] * jnp.maximum(-diff, 0.0))
        dec_ref[h * L:(h + 1) * L, :] = jnp.where(diff >= 0.0, f, bk)

    def state_update(s_ref, k, v, kw, gl):
        kd = (k.astype(F32) * kw).astype(BF16)
        contrib = lax.dot_general(kd, v, _TN, preferred_element_type=F32)
        s_ref[...] = s_ref[...] * gl + jnp.where(block_diag, contrib, 0.0)

    sf_ref[...] = jnp.zeros_like(sf_ref)
    sb_ref[...] = jnp.zeros_like(sb_ref)

    def ctx_fwd(c, carry):
        rows = pl.ds(pl.multiple_of(c * L, L), L)
        state_update(sf_ref, ck_ref[rows, :], cv_ref[rows, :], kw_f, gl_f)
        return carry

    def ctx_bwd(i, carry):
        rows = pl.ds(pl.multiple_of((ncc - 1 - i) * L, L), L)
        state_update(sb_ref, ck_ref[rows, :], cv_ref[rows, :], kw_b, gl_b)
        return carry

    lax.fori_loop(0, ncc, ctx_fwd, 0)
    lax.fori_loop(0, ncc, ctx_bwd, 0)

    def fwd_chunk(c, carry):
        rows = pl.ds(pl.multiple_of(c * L, L), L)
        q, k, v = q_ref[rows, :], k_ref[rows, :], v_ref[rows, :]
        q_heads = jnp.concatenate([q * head_mask[h] for h in range(HEAD_GROUP)], axis=0)
        s = lax.dot_general(q_heads, k, _NT, preferred_element_type=F32)
        s = (s * dec_ref[...]).astype(BF16)
        s_cat = jnp.concatenate([s[h * L:(h + 1) * L, :] for h in range(HEAD_GROUP)], axis=1)
        v_heads = jnp.concatenate([v * head_mask[h] for h in range(HEAD_GROUP)], axis=0)
        o = jnp.dot(s_cat, v_heads, preferred_element_type=F32)
        o = o + jnp.dot(q, sf_ref[...].astype(BF16), preferred_element_type=F32) * qw_f
        acc_ref[rows, :] = o
        state_update(sf_ref, k, v, kw_f, gl_f)
        return carry

    lax.fori_loop(0, nc, fwd_chunk, 0, unroll=2)

    def bwd_chunk(i, carry):
        rows = pl.ds(pl.multiple_of((nc - 1 - i) * L, L), L)
        q, k, v = q_ref[rows, :], k_ref[rows, :], v_ref[rows, :]
        o = acc_ref[rows, :] + jnp.dot(q, sb_ref[...].astype(BF16), preferred_element_type=F32) * qw_b
        sq = o * o
        sq_hi = sq.astype(BF16)
        sq_lo = (sq - sq_hi.astype(F32)).astype(BF16)
        ms = (jnp.dot(sq_hi, head_mean, preferred_element_type=F32)
              + jnp.dot(sq_lo, head_mean, preferred_element_type=F32))
        y = o * lax.rsqrt(ms + EPS) * _silu(rg_ref[rows, :].astype(F32))
        o_ref[rows, :] = y.astype(BF16)
        state_update(sb_ref, k, v, kw_b, gl_b)
        return carry

    lax.fori_loop(0, nc, bwd_chunk, 0, unroll=2)


def _retention(proj, cproj, lg_lane, lg_chunk):
    b, n, _ = proj.shape
    n_ctx = cproj.shape[1]
    L = RET_L
    col = lambda c0: pl.BlockSpec((None, n, GROUP_W), lambda bb, g: (bb, 0, c0 + g))
    ccol = lambda c0: pl.BlockSpec((None, n_ctx, GROUP_W), lambda bb, g: (bb, 0, c0 + g))
    return pl.pallas_call(
        _ret_kernel,
        grid=(b, N_GROUPS),
        in_specs=[col(COL_RQ), col(COL_RK), col(COL_RV), col(COL_RG), ccol(CCOL_RK), ccol(CCOL_RV),
                  pl.BlockSpec((2, GROUP_W), lambda bb, g: (0, g)),
                  pl.BlockSpec((2, HEAD_GROUP * L), lambda bb, g: (0, g))],
        out_specs=pl.BlockSpec((None, n, GROUP_W), lambda bb, g: (bb, 0, g)),
        out_shape=jax.ShapeDtypeStruct((b, n, MIX_W), BF16),
        scratch_shapes=[pltpu.VMEM((n, GROUP_W), F32),
                        pltpu.VMEM((HEAD_GROUP * L, L), F32),
                        pltpu.VMEM((GROUP_W, GROUP_W), F32),
                        pltpu.VMEM((GROUP_W, GROUP_W), F32)],
        compiler_params=_cparams("arbitrary", "arbitrary"),
        name="retention",
    )(proj, proj, proj, proj, cproj, cproj, lg_lane, lg_chunk)


def _na_patterns(rows):
    nblk = rows // NA_ROWS
    assert rows % NA_ROWS == 0 and nblk >= 3 and rows >= NA_KROWS
    pats = []
    for qb in (0, 1, nblk - 1):
        kb = _na_key_start(qb, rows)
        qr = qb * NA_ROWS + np.arange(NA_ROWS)
        r0 = np.clip(qr - NA_WIN_H // 2, 0, rows - NA_WIN_H)
        assert np.all(r0 >= kb) and np.all(r0 + NA_WIN_H <= kb + NA_KROWS)
        pats.append((tuple(int(v) for v in r0 - kb), int(kb - qb * NA_ROWS + NA_WIN_H - 1)))
    for qb in range(1, nblk - 1):
        kb = _na_key_start(qb, rows)
        qr = qb * NA_ROWS + np.arange(NA_ROWS)
        r0 = np.clip(qr - NA_WIN_H // 2, 0, rows - NA_WIN_H)
        assert (tuple(int(v) for v in r0 - kb), int(kb - qb * NA_ROWS + NA_WIN_H - 1)) == pats[1]
    return pats


def _na_key_start(qb, rows):
    assert (NA_WIN_H // 2) % NA_ROWS == 0 and NA_KROWS % NA_ROWS == 0
    return min(max(qb * NA_ROWS - NA_WIN_H // 2, 0), rows - NA_KROWS)


def _na_slot_range(pats):
    lo = min(off - (NA_ROWS - 1) for _, off in pats)
    hi = max(off + NA_KROWS - 1 for _, off in pats)
    return lo, hi


def _na_kernel(q_ref, k_ref, vt_ref, ck_ref, cv_ref, tz_ref, o_ref,
               bias_ref, cvt_ref, s0_ref, s1_ref, m0_ref, m1_ref, *, pats):
    n = q_ref.shape[0]
    n_ctx = ck_ref.shape[0]
    rows = n // GRID_W
    nblk = rows // NA_ROWS
    _, head_mask = _head_masks()
    _, dr_hi = _na_slot_range(pats)
    n_keys = NA_K + n_ctx

    @pl.when(pl.program_id(1) == 0)
    def _build_bias():
        key = lax.broadcasted_iota(jnp.int32, (NA_K, NA_Q), 0)
        qry = lax.broadcasted_iota(jnp.int32, (NA_K, NA_Q), 1)
        j, kc = key // GRID_W, key % GRID_W
        i, qc = qry // GRID_W, qry % GRID_W
        c0 = jnp.clip(qc - NA_WIN_W // 2, 0, GRID_W - NA_WIN_W)
        col_ok = (kc >= c0) & (kc < c0 + NA_WIN_W)
        for p, (r0, off) in enumerate(pats):
            lo = jnp.zeros_like(i)
            for ii in range(NA_ROWS):
                lo = jnp.where(i == ii, r0[ii], lo)
            ok = col_ok & (j >= lo) & (j < lo + NA_WIN_H)
            for h in range(HEAD_GROUP):
                t = tz_ref[h]
                for jj in range(NA_KROWS):
                    slot = dr_hi - (jj + off)
                    tile = t[:, slot * GRID_W:slot * GRID_W + NA_Q] * LOG2E
                    rs = slice(jj * GRID_W, (jj + 1) * GRID_W)
                    bias_ref[p, rs, h * NA_Q:(h + 1) * NA_Q] = jnp.where(ok[rs, :], tile, NEG_INF)

    cvt_ref[...] = cv_ref[...].astype(F32).T.astype(BF16)
    ones_rows = jnp.ones((2 * 8, n_keys), BF16)

    def block_rows(qb):
        kb = jnp.clip(qb * NA_ROWS - NA_WIN_H // 2, 0, rows - NA_KROWS)
        q_rows = pl.ds(pl.multiple_of(qb * NA_Q, NA_Q), NA_Q)
        k_tok = pl.ds(pl.multiple_of(kb * GRID_W, NA_Q), NA_K)
        return q_rows, k_tok

    def scores(qb, s_ref, m_ref):
        q_rows, k_tok = block_rows(qb)
        pat = jnp.where(qb == 0, 0, jnp.where(qb == nblk - 1, 2, 1))
        q = q_ref[q_rows, :]
        q_heads = jnp.concatenate([q * head_mask[h] for h in range(HEAD_GROUP)], axis=0)
        s_loc = lax.dot_general(k_ref[k_tok, :], q_heads, _NT, preferred_element_type=F32) + bias_ref[pat]
        s_ctx = lax.dot_general(ck_ref[...], q_heads, _NT, preferred_element_type=F32)
        s_ref[0:NA_K, :] = s_loc
        s_ref[NA_K:n_keys, :] = s_ctx
        m_ref[...] = jnp.maximum(jnp.max(s_loc, axis=0, keepdims=True), jnp.max(s_ctx, axis=0, keepdims=True))

    def attend(qb, s_ref, m_ref):
        q_rows, k_tok = block_rows(qb)
        p = jnp.exp2(s_ref[...] - m_ref[...]).astype(BF16)
        outs = []
        for h in range(HEAD_GROUP):
            hs = slice(h * HEAD_DIM, (h + 1) * HEAD_DIM)
            v_aug = jnp.concatenate(
                [jnp.concatenate([vt_ref[hs, k_tok], cvt_ref[hs, :]], axis=1), ones_rows], axis=0)
            oh = jnp.dot(v_aug, p[:, h * NA_Q:(h + 1) * NA_Q], preferred_element_type=F32)
            outs.append(oh[0:HEAD_DIM, :] / oh[HEAD_DIM:HEAD_DIM + 1, :])
        o_ref[q_rows, :] = jnp.concatenate(outs, axis=0).T.astype(BF16)

    assert nblk % 2 == 0
    scores(0, s0_ref, m0_ref)

    def pair(t, carry):
        qb = 2 * t
        scores(qb + 1, s1_ref, m1_ref)
        attend(qb, s0_ref, m0_ref)
        scores(jnp.minimum(qb + 2, nblk - 1), s0_ref, m0_ref)
        attend(qb + 1, s1_ref, m1_ref)
        return carry

    lax.fori_loop(0, nblk // 2, pair, 0)


def _na_toeplitz(rpb, pats):
    dr_lo, dr_hi = _na_slot_range(pats)
    slots = dr_hi - dr_lo + 1
    slots += (-slots) % (LANES // GRID_W)
    row_sel = np.zeros((RPB_ROWS, slots), np.float32)
    for s in range(slots):
        if 0 <= dr_hi - s < RPB_ROWS:
            row_sel[dr_hi - s, s] = 1.0
    col = np.arange(GRID_W)
    dc = col[:, None] - col[None, :] + NA_WIN_W - 1
    col_sel = (dc[None, :, :] == np.arange(RPB_COLS)[:, None, None]).astype(np.float32)
    t = jnp.einsum('hab,as,bkq->hksq', rpb, row_sel, col_sel, precision=lax.Precision.HIGHEST)
    return t.reshape(rpb.shape[0], GRID_W, slots * GRID_W)


def _neighborhood(proj, vt, cproj, tz, pats):
    b, n, _ = proj.shape
    n_ctx = cproj.shape[1]
    col = lambda c0: pl.BlockSpec((None, n, GROUP_W), lambda g, bb: (bb, 0, c0 + g))
    ccol = lambda c0: pl.BlockSpec((None, n_ctx, GROUP_W), lambda g, bb: (bb, 0, c0 + g))
    return pl.pallas_call(
        functools.partial(_na_kernel, pats=pats),
        grid=(N_GROUPS, b),
        in_specs=[col(COL_NQ), col(COL_NK),
                  pl.BlockSpec((None, GROUP_W, n), lambda g, bb: (bb, g, 0)),
                  ccol(CCOL_NK), ccol(CCOL_NV),
                  pl.BlockSpec((HEAD_GROUP,) + tz.shape[1:], lambda g, bb: (g, 0, 0))],
        out_specs=pl.BlockSpec((None, n, GROUP_W), lambda g, bb: (bb, 0, g)),
        out_shape=jax.ShapeDtypeStruct((b, n, MIX_W), BF16),
        scratch_shapes=[pltpu.VMEM((len(pats), NA_K, HEAD_GROUP * NA_Q), F32),
                        pltpu.VMEM((GROUP_W, n_ctx), BF16),
                        pltpu.VMEM((NA_K + n_ctx, HEAD_GROUP * NA_Q), F32),
                        pltpu.VMEM((NA_K + n_ctx, HEAD_GROUP * NA_Q), F32),
                        pltpu.VMEM((1, HEAD_GROUP * NA_Q), F32),
                        pltpu.VMEM((1, HEAD_GROUP * NA_Q), F32)],
        compiler_params=_cparams("arbitrary", "arbitrary"),
        name="neighborhood",
    )(proj, proj, vt, cproj, cproj, tz)


def _mix_ffn_kernel(x_ref, yr_ref, yn_ref, gr_ref, gn_ref, gt1_ref, sh2_ref, sc2_ref, gt2_ref,
                    g_post_mix_ref, g_pre_ffn_ref, g_post_ffn_ref,
                    w_ret_ref, w_na_ref, w_o_ref, w1_ref, w2_ref, o_ref):
    a = jnp.dot(yr_ref[...], w_ret_ref[...], preferred_element_type=F32)
    bb = jnp.dot(yn_ref[...], w_na_ref[...], preferred_element_type=F32)
    y = (jax.nn.sigmoid(gr_ref[...].astype(F32)) * a + jax.nn.sigmoid(gn_ref[...].astype(F32)) * bb)
    y = jnp.dot(y.astype(BF16), w_o_ref[...], preferred_element_type=F32)
    x1 = x_ref[...] + gt1_ref[...] * (_rms(y) * g_post_mix_ref[...])
    h2 = ((_rms(x1) * g_pre_ffn_ref[...]) * (1.0 + sc2_ref[...]) + sh2_ref[...]).astype(BF16)
    d_ff = w1_ref.shape[1]
    f = jnp.zeros(x1.shape, F32)
    for c0 in range(0, d_ff, FF_CHUNK):
        u = jnp.dot(h2, w1_ref[:, c0:c0 + FF_CHUNK], preferred_element_type=F32)
        u = jnp.square(jnp.maximum(u, 0.0)).astype(BF16)
        f = f + jnp.dot(u, w2_ref[c0:c0 + FF_CHUNK, :], preferred_element_type=F32)
    o_ref[...] = x1 + gt2_ref[...] * (_rms(f) * g_post_ffn_ref[...])


def _mix_ffn(x, y_ret, y_na, proj, mod4, g_post_mix, g_pre_ffn, g_post_ffn,
             w_ret, w_na, w_o, w1, w2):
    b, n, d = x.shape
    tm = MIX_TM
    tok = lambda w: pl.BlockSpec((None, tm, w), lambda i, bb: (bb, i, 0))
    mod_spec = lambda k: pl.BlockSpec((None, None, 1, d), lambda i, bb: (bb, k, 0, 0))
    return pl.pallas_call(
        _mix_ffn_kernel,
        grid=(n // tm, b),
        in_specs=[tok(d), tok(MIX_W), tok(MIX_W),
                  pl.BlockSpec((None, tm, d), lambda i, bb: (bb, i, 0)),
                  pl.BlockSpec((None, tm, d), lambda i, bb: (bb, i, 1)),
                  mod_spec(2), mod_spec(3), mod_spec(4), mod_spec(5),
                  _resident((1, d)), _resident((1, d)), _resident((1, d)),
                  _resident(w_ret.shape), _resident(w_na.shape), _resident(w_o.shape),
                  _resident(w1.shape), _resident(w2.shape)],
        out_specs=tok(d),
        out_shape=jax.ShapeDtypeStruct((b, n, d), F32),
        compiler_params=_cparams("arbitrary", "arbitrary"),
        name="mix_ffn",
    )(x, y_ret, y_na, proj, proj, mod4, mod4, mod4, mod4, g_post_mix, g_pre_ffn, g_post_ffn,
      w_ret, w_na, w_o, w1, w2)


def _rope_tables(n):
    pos = jnp.arange(n)
    row = (pos // GRID_W).astype(F32)
    colp = (pos % GRID_W).astype(F32)
    d_axis = HEAD_DIM // 2
    inv = ROPE_BASE ** (-jnp.arange(0, d_axis, 2, dtype=F32) / d_axis)
    ang = jnp.concatenate([row[:, None] * inv, colp[:, None] * inv], axis=-1)
    cos, sin = jnp.cos(ang), jnp.sin(ang)
    zero = jnp.zeros_like(sin)
    reps = LANES // HEAD_DIM
    cos_t = jnp.tile(jnp.concatenate([cos, cos], axis=-1), (1, reps))
    sin_lo = jnp.tile(jnp.concatenate([-sin, zero], axis=-1), (1, reps))
    sin_hi = jnp.tile(jnp.concatenate([zero, sin], axis=-1), (1, reps))
    return cos_t, sin_lo, sin_hi


def kernel(x, c, ctx, c_ctx, w_ada, b_ada, norm_pre_mix, norm_post_mix, norm_pre_ffn, norm_post_ffn,
           w_in, ret_decay_logit, w_ret_out, na_rpb, w_na_out, w_o, w_ff1, w_ff2):
    b, n, d = x.shape
    n_ctx = ctx.shape[1]
    assert w_ada.shape[0] == 1, "single-layer block"
    assert n % NA_Q == 0 and n % PROJ_TM == 0 and n % MIX_TM == 0 and n % RET_L == 0
    assert n_ctx % RET_L == 0 and d == D_MODEL and 7 * MIX_W + GATE_COLS == w_in.shape[2]

    pad = (-(b + 1)) % 8
    c_rows = jnp.concatenate([c, c_ctx[None, :], jnp.zeros((pad, d), F32)], axis=0)
    mod = _ada(c_rows, w_ada[0], b_ada[0])
    mod4 = mod.reshape(c_rows.shape[0], 6, 1, d)

    w_all = w_in[0].astype(BF16)
    w_ctx_bf = jnp.concatenate([w_all[:, MIX_W:3 * MIX_W], w_all[:, 5 * MIX_W:7 * MIX_W]], axis=1)
    w_lat_bf = jnp.concatenate([w_all[:, 7 * MIX_W:], w_all[:, :6 * MIX_W]], axis=1)
    w_nvt_bf = w_all[:, 6 * MIX_W:7 * MIX_W].T
    g_pre_mix = norm_pre_mix[0].reshape(1, d)

    proj, nvt = _proj_latent(x, mod4, g_pre_mix, w_lat_bf, w_nvt_bf, _rope_tables(n))
    cproj = _proj_context(ctx.reshape(b * n_ctx, d), mod4, b, g_pre_mix, w_ctx_bf)
    cproj = cproj.reshape(b, n_ctx, 4 * MIX_W)

    logit = ret_decay_logit[0].astype(F32)
    y_ret = _retention(proj, cproj, jnp.repeat(logit, HEAD_DIM, axis=1), jnp.repeat(logit, RET_L, axis=1))

    pats = _na_patterns(n // GRID_W)
    y_na = _neighborhood(proj, nvt, cproj, _na_toeplitz(na_rpb[0].astype(F32), pats), pats)

    return _mix_ffn(x, y_ret, y_na, proj, mod4,
                    norm_post_mix[0].reshape(1, d), norm_pre_ffn[0].reshape(1, d),
                    norm_post_ffn[0].reshape(1, d),
                    w_ret_out[0].astype(BF16), w_na_out[0].astype(BF16), w_o[0].astype(BF16),
                    w_ff1[0].astype(BF16), w_ff2[0].astype(BF16))
```

```python
import functools
import math

import jax
import jax.numpy as jnp
import numpy as np
from jax import lax
from jax.experimental import pallas as pl
from jax.experimental.pallas import tpu as pltpu

F32 = jnp.float32
BF16 = jnp.bfloat16

D_MODEL = 1024
GRID_W = 64
HEADS = 8
HEAD_DIM = 64
MIX_W = HEADS * HEAD_DIM
NA_WIN_H = 8
NA_WIN_W = 16
ROPE_BASE = 10000.0
EPS = 1e-6
NEG_INF = -1e30
K_SCALE = HEAD_DIM ** -0.5
LOG2E = math.log2(math.e)

LANES = 128
V7X_VMEM_BYTES = 64 * 2 ** 20
VMEM_LIMIT = V7X_VMEM_BYTES - 8 * 2 ** 20

HEAD_GROUP = 4
GROUP_W = HEAD_GROUP * HEAD_DIM
N_GROUPS = HEADS // HEAD_GROUP
PROJ_TM = 1024
PROJ_SPLIT = 2
MIX_TM = 512
MIX_SPLIT = 1
FF_CHUNK = 1024
RET_L = 256
NA_ROWS = 4
NA_KROWS = 12
NA_Q = NA_ROWS * GRID_W
NA_K = NA_KROWS * GRID_W
RPB_ROWS = 2 * NA_WIN_H - 1
RPB_COLS = 2 * NA_WIN_W - 1

GATE_COLS = 2 * D_MODEL
COL_RQ, COL_RK, COL_RV, COL_RG, COL_NQ, COL_NK = (GATE_COLS // GROUP_W + i * N_GROUPS for i in range(6))
PROJ_COLS = GATE_COLS + 6 * MIX_W
CCOL_RK, CCOL_RV, CCOL_NK, CCOL_NV = (i * N_GROUPS for i in range(4))

_NT = (((1,), (1,)), ((), ()))
_TN = (((0,), (0,)), ((), ()))


def _cparams(*sem):
    return pltpu.CompilerParams(dimension_semantics=sem, vmem_limit_bytes=VMEM_LIMIT)


def _resident(shape):
    nd = len(shape)
    return pl.BlockSpec(shape, lambda *_: (0,) * nd, pipeline_mode=pl.Buffered(1))


def _rms(x):
    return x * lax.rsqrt(jnp.mean(x * x, axis=-1, keepdims=True) + EPS)


def _silu(x):
    return x * jax.nn.sigmoid(x)


def _head_masks():
    lane_head = lax.broadcasted_iota(jnp.int32, (1, GROUP_W), 1) // HEAD_DIM
    sel = [lane_head == h for h in range(HEAD_GROUP)]
    return sel, [m.astype(BF16) for m in sel]


def _ada_kernel(c_ref, w_ref, b_ref, o_ref):
    a = _silu(c_ref[...])
    o_ref[...] = jnp.dot(a, w_ref[...], preferred_element_type=F32,
                         precision=lax.Precision.HIGHEST) + b_ref[...]


def _ada(c_rows, w_ada, b_ada):
    rows, d = c_rows.shape
    cols = w_ada.shape[1]
    tn = d
    return pl.pallas_call(
        _ada_kernel,
        grid=(cols // tn,),
        in_specs=[pl.BlockSpec((rows, d), lambda j: (0, 0)),
                  pl.BlockSpec((d, tn), lambda j: (0, j)),
                  pl.BlockSpec((1, tn), lambda j: (0, j))],
        out_specs=pl.BlockSpec((rows, tn), lambda j: (0, j)),
        out_shape=jax.ShapeDtypeStruct((rows, cols), F32),
        compiler_params=_cparams("arbitrary"),
        name="ada_mod",
    )(c_rows, w_ada, b_ada.reshape(1, cols))


def _rope_slab(t, cos, sin_lo, sin_hi):
    return (t * cos + pltpu.roll(t, 3 * LANES // 4, axis=1) * sin_lo
            + pltpu.roll(t, LANES // 4, axis=1) * sin_hi)


def _modulated_norm(x_ref, sh_ref, sc_ref, g_ref):
    return ((_rms(x_ref[...]) * g_ref[...]) * (1.0 + sc_ref[...]) + sh_ref[...]).astype(BF16)


def _proj_latent_kernel(x_ref, sh_ref, sc_ref, g_ref, w_ref, wvt_ref, cos_ref, slo_ref, shi_ref,
                        o_ref, vt_ref):
    rope_lo, rope_hi = COL_RQ * GROUP_W, COL_RV * GROUP_W
    col_scale = {COL_RK * GROUP_W: K_SCALE, COL_NQ * GROUP_W: K_SCALE * LOG2E}
    tm = x_ref.shape[0]
    for r0 in range(0, tm, tm // PROJ_SPLIT):
        rows = slice(r0, r0 + tm // PROJ_SPLIT)
        x = x_ref[rows, :]
        h = ((_rms(x) * g_ref[...]) * (1.0 + sc_ref[...]) + sh_ref[...]).astype(BF16)
        for c0 in range(0, PROJ_COLS, MIX_W):
            r = jnp.dot(h, w_ref[:, c0:c0 + MIX_W], preferred_element_type=F32)
            if rope_lo <= c0 < rope_hi:
                cos, slo, shi = cos_ref[rows, :], slo_ref[rows, :], shi_ref[rows, :]
                r = jnp.concatenate(
                    [_rope_slab(r[:, j:j + LANES], cos, slo, shi) for j in range(0, MIX_W, LANES)], axis=1)
            if c0 in col_scale:
                r = r * col_scale[c0]
            o_ref[rows, c0:c0 + MIX_W] = r.astype(BF16)
        vt_ref[:, rows] = lax.dot_general(wvt_ref[...], h, _NT, preferred_element_type=F32).astype(BF16)


def _proj_latent(x, mod4, g, w_bf, wvt_bf, rope_tabs):
    b, n, d = x.shape
    tm = PROJ_TM
    mod_spec = lambda k: pl.BlockSpec((None, None, 1, d), lambda i, bb: (bb, k, 0, 0))
    tab_spec = pl.BlockSpec((tm, LANES), lambda i, bb: (i, 0))
    return pl.pallas_call(
        _proj_latent_kernel,
        grid=(n // tm, b),
        in_specs=[pl.BlockSpec((None, tm, d), lambda i, bb: (bb, i, 0)),
                  mod_spec(0), mod_spec(1), _resident((1, d)), _resident(w_bf.shape),
                  _resident(wvt_bf.shape), tab_spec, tab_spec, tab_spec],
        out_specs=[pl.BlockSpec((None, tm, PROJ_COLS), lambda i, bb: (bb, i, 0)),
                   pl.BlockSpec((None, MIX_W, tm), lambda i, bb: (bb, 0, i))],
        out_shape=[jax.ShapeDtypeStruct((b, n, PROJ_COLS), BF16),
                   jax.ShapeDtypeStruct((b, MIX_W, n), BF16)],
        compiler_params=_cparams("arbitrary", "arbitrary"),
        name="proj_latent",
    )(x, mod4, mod4, g, w_bf, wvt_bf, *rope_tabs)


def _proj_context_kernel(x_ref, sh_ref, sc_ref, g_ref, w_ref, o_ref):
    h = _modulated_norm(x_ref, sh_ref, sc_ref, g_ref)
    for c0 in range(0, o_ref.shape[-1], MIX_W):
        r = jnp.dot(h, w_ref[:, c0:c0 + MIX_W], preferred_element_type=F32)
        if c0 == CCOL_RK * GROUP_W:
            r = r * K_SCALE
        o_ref[:, c0:c0 + MIX_W] = r.astype(BF16)


def _proj_context(ctx_rows, mod4, ctx_row, g, w_bf):
    rows, d = ctx_rows.shape
    cols = w_bf.shape[1]
    tm = min(PROJ_TM, rows)
    mod_spec = lambda k: pl.BlockSpec((None, None, 1, d), lambda i: (ctx_row, k, 0, 0))
    return pl.pallas_call(
        _proj_context_kernel,
        grid=(rows // tm,),
        in_specs=[pl.BlockSpec((tm, d), lambda i: (i, 0)),
                  mod_spec(0), mod_spec(1), _resident((1, d)), _resident((d, cols))],
        out_specs=pl.BlockSpec((tm, cols), lambda i: (i, 0)),
        out_shape=jax.ShapeDtypeStruct((rows, cols), BF16),
        compiler_params=_cparams("arbitrary"),
        name="proj_context",
    )(ctx_rows, mod4, mod4, g, w_bf)


def _log_sigmoid(x):
    return jnp.minimum(x, 0.0) - jnp.log1p(jnp.exp(-jnp.abs(x)))


def _ret_kernel(q_ref, k_ref, v_ref, rg_ref, ck_ref, cv_ref, lg_lane_ref, lg_chunk_ref, o_ref,
                acc_ref, dec_ref, sf_ref, sb_ref):
    L = RET_L
    n = q_ref.shape[0]
    n_ctx = ck_ref.shape[0]
    nc, ncc = n // L, n_ctx // L

    lg = _log_sigmoid(lg_lane_ref[...])
    lgf, lgb = lg[0:1, :], lg[1:2, :]
    pos = lax.broadcasted_iota(jnp.int32, (L, 1), 0).astype(F32)
    qw_f = jnp.exp(lgf * (pos + 1.0))
    kw_f = jnp.exp(lgf * (L - 1.0 - pos))
    qw_b = jnp.exp(lgb * (L - pos))
    kw_b = jnp.exp(lgb * pos)
    gl_f = jnp.exp(lgf * float(L))
    gl_b = jnp.exp(lgb * float(L))

    lane_head = lax.broadcasted_iota(jnp.int32, (1, GROUP_W), 1) // HEAD_DIM
    row_head = lax.broadcasted_iota(jnp.int32, (GROUP_W, 1), 0) // HEAD_DIM
    block_diag = row_head == lane_head
    head_mean = jnp.where(block_diag, 1.0 / HEAD_DIM, 0.0).astype(BF16)
    _, head_mask = _head_masks()

    lgc = _log_sigmoid(lg_chunk_ref[...])
    diff = pos - lax.broadcasted_iota(jnp.int32, (1, L), 1).astype(F32)
    for h in range(HEAD_GROUP):
        f = jnp.exp(lgc[0:1, h * L:(h + 1) * L] * jnp.maximum(diff, 0.0))
        bk = jnp.exp(lgc[1:2, h * L:(h + 1) * L] * jnp.maximum(-diff, 0.0))
        dec_ref[h * L:(h + 1) * L, :] = jnp.where(diff >= 0.0, f, bk)

    def state_update(s_ref, k, v, kw, gl):
        kd = (k.astype(F32) * kw).astype(BF16)
        contrib = lax.dot_general(kd, v, _TN, preferred_element_type=F32)
        s_ref[...] = s_ref[...] * gl + jnp.where(block_diag, contrib, 0.0)

    sf_ref[...] = jnp.zeros_like(sf_ref)
    sb_ref[...] = jnp.zeros_like(sb_ref)

    def ctx_fwd(c, carry):
        rows = pl.ds(pl.multiple_of(c * L, L), L)
        state_update(sf_ref, ck_ref[rows, :], cv_ref[rows, :], kw_f, gl_f)
        return carry

    def ctx_bwd(i, carry):
        rows = pl.ds(pl.multiple_of((ncc - 1 - i) * L, L), L)
        state_update(sb_ref, ck_ref[rows, :], cv_ref[rows, :], kw_b, gl_b)
        return carry

    lax.fori_loop(0, ncc, ctx_fwd, 0)
    lax.fori_loop(0, ncc, ctx_bwd, 0)

    def fwd_chunk(c, carry):
        rows = pl.ds(pl.multiple_of(c * L, L), L)
        q, k, v = q_ref[rows, :], k_ref[rows, :], v_ref[rows, :]
        q_heads = jnp.concatenate([q * head_mask[h] for h in range(HEAD_GROUP)], axis=0)
        s = lax.dot_general(q_heads, k, _NT, preferred_element_type=F32)
        s = (s * dec_ref[...]).astype(BF16)
        s_cat = jnp.concatenate([s[h * L:(h + 1) * L, :] for h in range(HEAD_GROUP)], axis=1)
        v_heads = jnp.concatenate([v * head_mask[h] for h in range(HEAD_GROUP)], axis=0)
        o = jnp.dot(s_cat, v_heads, preferred_element_type=F32)
        o = o + jnp.dot(q, sf_ref[...].astype(BF16), preferred_element_type=F32) * qw_f
        acc_ref[rows, :] = o
        state_update(sf_ref, k, v, kw_f, gl_f)
        return carry

    lax.fori_loop(0, nc, fwd_chunk, 0, unroll=8)

    def bwd_chunk(i, carry):
        rows = pl.ds(pl.multiple_of((nc - 1 - i) * L, L), L)
        q, k, v = q_ref[rows, :], k_ref[rows, :], v_ref[rows, :]
        o = acc_ref[rows, :] + jnp.dot(q, sb_ref[...].astype(BF16), preferred_element_type=F32) * qw_b
        sq = o * o
        sq_hi = sq.astype(BF16)
        sq_lo = (sq - sq_hi.astype(F32)).astype(BF16)
        ms = (jnp.dot(sq_hi, head_mean, preferred_element_type=F32)
              + jnp.dot(sq_lo, head_mean, preferred_element_type=F32))
        y = o * lax.rsqrt(ms + EPS) * _silu(rg_ref[rows, :].astype(F32))
        o_ref[rows, :] = y.astype(BF16)
        state_update(sb_ref, k, v, kw_b, gl_b)
        return carry

    lax.fori_loop(0, nc, bwd_chunk, 0, unroll=8)


def _retention(proj, cproj, lg_lane, lg_chunk):
    b, n, _ = proj.shape
    n_ctx = cproj.shape[1]
    L = RET_L
    col = lambda c0: pl.BlockSpec((None, n, GROUP_W), lambda bb, g: (bb, 0, c0 + g))
    ccol = lambda c0: pl.BlockSpec((None, n_ctx, GROUP_W), lambda bb, g: (bb, 0, c0 + g))
    return pl.pallas_call(
        _ret_kernel,
        grid=(b, N_GROUPS),
        in_specs=[col(COL_RQ), col(COL_RK), col(COL_RV), col(COL_RG), ccol(CCOL_RK), ccol(CCOL_RV),
                  pl.BlockSpec((2, GROUP_W), lambda bb, g: (0, g)),
                  pl.BlockSpec((2, HEAD_GROUP * L), lambda bb, g: (0, g))],
        out_specs=pl.BlockSpec((None, n, GROUP_W), lambda bb, g: (bb, 0, g)),
        out_shape=jax.ShapeDtypeStruct((b, n, MIX_W), BF16),
        scratch_shapes=[pltpu.VMEM((n, GROUP_W), F32),
                        pltpu.VMEM((HEAD_GROUP * L, L), F32),
                        pltpu.VMEM((GROUP_W, GROUP_W), F32),
                        pltpu.VMEM((GROUP_W, GROUP_W), F32)],
        compiler_params=_cparams("arbitrary", "arbitrary"),
        name="retention",
    )(proj, proj, proj, proj, cproj, cproj, lg_lane, lg_chunk)


def _na_patterns(rows):
    nblk = rows // NA_ROWS
    assert rows % NA_ROWS == 0 and nblk >= 3 and rows >= NA_KROWS
    pats = []
    for qb in (0, 1, nblk - 1):
        kb = _na_key_start(qb, rows)
        qr = qb * NA_ROWS + np.arange(NA_ROWS)
        r0 = np.clip(qr - NA_WIN_H // 2, 0, rows - NA_WIN_H)
        assert np.all(r0 >= kb) and np.all(r0 + NA_WIN_H <= kb + NA_KROWS)
        pats.append((tuple(int(v) for v in r0 - kb), int(kb - qb * NA_ROWS + NA_WIN_H - 1)))
    for qb in range(1, nblk - 1):
        kb = _na_key_start(qb, rows)
        qr = qb * NA_ROWS + np.arange(NA_ROWS)
        r0 = np.clip(qr - NA_WIN_H // 2, 0, rows - NA_WIN_H)
        assert (tuple(int(v) for v in r0 - kb), int(kb - qb * NA_ROWS + NA_WIN_H - 1)) == pats[1]
    return pats


def _na_key_start(qb, rows):
    assert (NA_WIN_H // 2) % NA_ROWS == 0 and NA_KROWS % NA_ROWS == 0
    return min(max(qb * NA_ROWS - NA_WIN_H // 2, 0), rows - NA_KROWS)


def _na_slot_range(pats):
    lo = min(off - (NA_ROWS - 1) for _, off in pats)
    hi = max(off + NA_KROWS - 1 for _, off in pats)
    return lo, hi


def _na_kernel(q_ref, k_ref, vt_ref, ck_ref, cv_ref, tz_ref, o_ref,
               bias_ref, cvt_ref, s0_ref, s1_ref, m0_ref, m1_ref, *, pats):
    n = q_ref.shape[0]
    n_ctx = ck_ref.shape[0]
    rows = n // GRID_W
    nblk = rows // NA_ROWS
    _, head_mask = _head_masks()
    _, dr_hi = _na_slot_range(pats)
    n_keys = NA_K + n_ctx

    @pl.when(pl.program_id(1) == 0)
    def _build_bias():
        key = lax.broadcasted_iota(jnp.int32, (NA_K, NA_Q), 0)
        qry = lax.broadcasted_iota(jnp.int32, (NA_K, NA_Q), 1)
        j, kc = key // GRID_W, key % GRID_W
        i, qc = qry // GRID_W, qry % GRID_W
        c0 = jnp.clip(qc - NA_WIN_W // 2, 0, GRID_W - NA_WIN_W)
        col_ok = (kc >= c0) & (kc < c0 + NA_WIN_W)
        for p, (r0, off) in enumerate(pats):
            lo = jnp.zeros_like(i)
            for ii in range(NA_ROWS):
                lo = jnp.where(i == ii, r0[ii], lo)
            ok = col_ok & (j >= lo) & (j < lo + NA_WIN_H)
            for h in range(HEAD_GROUP):
                t = tz_ref[h]
                for jj in range(NA_KROWS):
                    slot = dr_hi - (jj + off)
                    tile = t[:, slot * GRID_W:slot * GRID_W + NA_Q] * LOG2E
                    rs = slice(jj * GRID_W, (jj + 1) * GRID_W)
                    bias_ref[p, rs, h * NA_Q:(h + 1) * NA_Q] = jnp.where(ok[rs, :], tile, NEG_INF)

    cvt_ref[...] = cv_ref[...].astype(F32).T.astype(BF16)
    ones_rows = jnp.ones((2 * 8, n_keys), BF16)

    def block_rows(qb):
        kb = jnp.clip(qb * NA_ROWS - NA_WIN_H // 2, 0, rows - NA_KROWS)
        q_rows = pl.ds(pl.multiple_of(qb * NA_Q, NA_Q), NA_Q)
        k_tok = pl.ds(pl.multiple_of(kb * GRID_W, NA_Q), NA_K)
        return q_rows, k_tok

    def scores(qb, s_ref, m_ref):
        q_rows, k_tok = block_rows(qb)
        pat = jnp.where(qb == 0, 0, jnp.where(qb == nblk - 1, 2, 1))
        q = q_ref[q_rows, :]
        q_heads = jnp.concatenate([q * head_mask[h] for h in range(HEAD_GROUP)], axis=0)
        s_ref[0:NA_K, :] = (lax.dot_general(k_ref[k_tok, :], q_heads, _NT, preferred_element_type=F32)
                            + bias_ref[pat])
        s_ref[NA_K:n_keys, :] = lax.dot_general(ck_ref[...], q_heads, _NT, preferred_element_type=F32)
        m_ref[...] = jnp.max(s_ref[...], axis=0, keepdims=True)

    def attend(qb, s_ref, m_ref):
        q_rows, k_tok = block_rows(qb)
        p = jnp.exp2(s_ref[...] - m_ref[...]).astype(BF16)
        outs = []
        for h in range(HEAD_GROUP):
            hs = slice(h * HEAD_DIM, (h + 1) * HEAD_DIM)
            v_aug = jnp.concatenate(
                [jnp.concatenate([vt_ref[hs, k_tok], cvt_ref[hs, :]], axis=1), ones_rows], axis=0)
            oh = jnp.dot(v_aug, p[:, h * NA_Q:(h + 1) * NA_Q], preferred_element_type=F32)
            outs.append(oh[0:HEAD_DIM, :] / oh[HEAD_DIM:HEAD_DIM + 1, :])
        o_ref[q_rows, :] = jnp.concatenate(outs, axis=0).T.astype(BF16)

    assert nblk % 2 == 0
    scores(0, s0_ref, m0_ref)

    def pair(t, carry):
        qb = 2 * t
        scores(qb + 1, s1_ref, m1_ref)
        attend(qb, s0_ref, m0_ref)
        scores(jnp.minimum(qb + 2, nblk - 1), s0_ref, m0_ref)
        attend(qb + 1, s1_ref, m1_ref)
        return carry

    lax.fori_loop(0, nblk // 2, pair, 0, unroll=2)


def _na_toeplitz(rpb, pats):
    dr_lo, dr_hi = _na_slot_range(pats)
    slots = dr_hi - dr_lo + 1
    slots += (-slots) % (LANES // GRID_W)
    row_sel = np.zeros((RPB_ROWS, slots), np.float32)
    for s in range(slots):
        if 0 <= dr_hi - s < RPB_ROWS:
            row_sel[dr_hi - s, s] = 1.0
    col = np.arange(GRID_W)
    dc = col[:, None] - col[None, :] + NA_WIN_W - 1
    col_sel = (dc[None, :, :] == np.arange(RPB_COLS)[:, None, None]).astype(np.float32)
    t = jnp.einsum('hab,as,bkq->hksq', rpb, row_sel, col_sel, precision=lax.Precision.HIGHEST)
    return t.reshape(rpb.shape[0], GRID_W, slots * GRID_W)


def _neighborhood(proj, vt, cproj, tz, pats):
    b, n, _ = proj.shape
    n_ctx = cproj.shape[1]
    col = lambda c0: pl.BlockSpec((None, n, GROUP_W), lambda g, bb: (bb, 0, c0 + g))
    ccol = lambda c0: pl.BlockSpec((None, n_ctx, GROUP_W), lambda g, bb: (bb, 0, c0 + g))
    return pl.pallas_call(
        functools.partial(_na_kernel, pats=pats),
        grid=(N_GROUPS, b),
        in_specs=[col(COL_NQ), col(COL_NK),
                  pl.BlockSpec((None, GROUP_W, n), lambda g, bb: (bb, g, 0)),
                  ccol(CCOL_NK), ccol(CCOL_NV),
                  pl.BlockSpec((HEAD_GROUP,) + tz.shape[1:], lambda g, bb: (g, 0, 0))],
        out_specs=pl.BlockSpec((None, n, GROUP_W), lambda g, bb: (bb, 0, g)),
        out_shape=jax.ShapeDtypeStruct((b, n, MIX_W), BF16),
        scratch_shapes=[pltpu.VMEM((len(pats), NA_K, HEAD_GROUP * NA_Q), F32),
                        pltpu.VMEM((GROUP_W, n_ctx), BF16),
                        pltpu.VMEM((NA_K + n_ctx, HEAD_GROUP * NA_Q), F32),
                        pltpu.VMEM((NA_K + n_ctx, HEAD_GROUP * NA_Q), F32),
                        pltpu.VMEM((1, HEAD_GROUP * NA_Q), F32),
                        pltpu.VMEM((1, HEAD_GROUP * NA_Q), F32)],
        compiler_params=_cparams("arbitrary", "arbitrary"),
        name="neighborhood",
    )(proj, proj, vt, cproj, cproj, tz)


def _mix_ffn_kernel(x_ref, yr_ref, yn_ref, gr_ref, gn_ref, gt1_ref, sh2_ref, sc2_ref, gt2_ref,
                    g_post_mix_ref, g_pre_ffn_ref, g_post_ffn_ref,
                    w_ret_ref, w_na_ref, w_o_ref, w1_ref, w2_ref, o_ref):
    tm = x_ref.shape[0]
    d_ff = w1_ref.shape[1]
    for r0 in range(0, tm, tm // MIX_SPLIT):
        rows = slice(r0, r0 + tm // MIX_SPLIT)
        a = jnp.dot(yr_ref[rows, :], w_ret_ref[...], preferred_element_type=F32)
        bb = jnp.dot(yn_ref[rows, :], w_na_ref[...], preferred_element_type=F32)
        y = (jax.nn.sigmoid(gr_ref[rows, :].astype(F32)) * a
             + jax.nn.sigmoid(gn_ref[rows, :].astype(F32)) * bb)
        y = jnp.dot(y.astype(BF16), w_o_ref[...], preferred_element_type=F32)
        x1 = x_ref[rows, :] + gt1_ref[...] * (_rms(y) * g_post_mix_ref[...])
        h2 = ((_rms(x1) * g_pre_ffn_ref[...]) * (1.0 + sc2_ref[...]) + sh2_ref[...]).astype(BF16)
        f = jnp.zeros(x1.shape, F32)
        for c0 in range(0, d_ff, FF_CHUNK):
            u = jnp.dot(h2, w1_ref[:, c0:c0 + FF_CHUNK], preferred_element_type=F32)
            u = jnp.square(jnp.maximum(u, 0.0)).astype(BF16)
            f = f + jnp.dot(u, w2_ref[c0:c0 + FF_CHUNK, :], preferred_element_type=F32)
        o_ref[rows, :] = x1 + gt2_ref[...] * (_rms(f) * g_post_ffn_ref[...])


def _mix_ffn(x, y_ret, y_na, proj, mod4, g_post_mix, g_pre_ffn, g_post_ffn,
             w_ret, w_na, w_o, w1, w2):
    b, n, d = x.shape
    tm = MIX_TM
    tok = lambda w: pl.BlockSpec((None, tm, w), lambda i, bb: (bb, i, 0))
    mod_spec = lambda k: pl.BlockSpec((None, None, 1, d), lambda i, bb: (bb, k, 0, 0))
    return pl.pallas_call(
        _mix_ffn_kernel,
        grid=(n // tm, b),
        in_specs=[tok(d), tok(MIX_W), tok(MIX_W),
                  pl.BlockSpec((None, tm, d), lambda i, bb: (bb, i, 0)),
                  pl.BlockSpec((None, tm, d), lambda i, bb: (bb, i, 1)),
                  mod_spec(2), mod_spec(3), mod_spec(4), mod_spec(5),
                  _resident((1, d)), _resident((1, d)), _resident((1, d)),
                  _resident(w_ret.shape), _resident(w_na.shape), _resident(w_o.shape),
                  _resident(w1.shape), _resident(w2.shape)],
        out_specs=tok(d),
        out_shape=jax.ShapeDtypeStruct((b, n, d), F32),
        compiler_params=_cparams("arbitrary", "arbitrary"),
        name="mix_ffn",
    )(x, y_ret, y_na, proj, proj, mod4, mod4, mod4, mod4, g_post_mix, g_pre_ffn, g_post_ffn,
      w_ret, w_na, w_o, w1, w2)


def _rope_tables(n):
    pos = jnp.arange(n)
    row = (pos // GRID_W).astype(F32)
    colp = (pos % GRID_W).astype(F32)
    d_axis = HEAD_DIM // 2
    inv = ROPE_BASE ** (-jnp.arange(0, d_axis, 2, dtype=F32) / d_axis)
    ang = jnp.concatenate([row[:, None] * inv, colp[:, None] * inv], axis=-1)
    cos, sin = jnp.cos(ang), jnp.sin(ang)
    zero = jnp.zeros_like(sin)
    reps = LANES // HEAD_DIM
    cos_t = jnp.tile(jnp.concatenate([cos, cos], axis=-1), (1, reps))
    sin_lo = jnp.tile(jnp.concatenate([-sin, zero], axis=-1), (1, reps))
    sin_hi = jnp.tile(jnp.concatenate([zero, sin], axis=-1), (1, reps))
    return cos_t, sin_lo, sin_hi


def kernel(x, c, ctx, c_ctx, w_ada, b_ada, norm_pre_mix, norm_post_mix, norm_pre_ffn, norm_post_ffn,
           w_in, ret_decay_logit, w_ret_out, na_rpb, w_na_out, w_o, w_ff1, w_ff2):
    b, n, d = x.shape
    n_ctx = ctx.shape[1]
    assert w_ada.shape[0] == 1, "single-layer block"
    assert n % NA_Q == 0 and n % PROJ_TM == 0 and n % MIX_TM == 0 and n % RET_L == 0
    assert n_ctx % RET_L == 0 and d == D_MODEL and 7 * MIX_W + GATE_COLS == w_in.shape[2]

    pad = (-(b + 1)) % 8
    c_rows = jnp.concatenate([c, c_ctx[None, :], jnp.zeros((pad, d), F32)], axis=0)
    mod = _ada(c_rows, w_ada[0], b_ada[0])
    mod4 = mod.reshape(c_rows.shape[0], 6, 1, d)

    w_all = w_in[0].astype(BF16)
    w_ctx_bf = jnp.concatenate([w_all[:, MIX_W:3 * MIX_W], w_all[:, 5 * MIX_W:7 * MIX_W]], axis=1)
    w_lat_bf = jnp.concatenate([w_all[:, 7 * MIX_W:], w_all[:, :6 * MIX_W]], axis=1)
    w_nvt_bf = w_all[:, 6 * MIX_W:7 * MIX_W].T
    g_pre_mix = norm_pre_mix[0].reshape(1, d)

    proj, nvt = _proj_latent(x, mod4, g_pre_mix, w_lat_bf, w_nvt_bf, _rope_tables(n))
    cproj = _proj_context(ctx.reshape(b * n_ctx, d), mod4, b, g_pre_mix, w_ctx_bf)
    cproj = cproj.reshape(b, n_ctx, 4 * MIX_W)

    logit = ret_decay_logit[0].astype(F32)
    y_ret = _retention(proj, cproj, jnp.repeat(logit, HEAD_DIM, axis=1), jnp.repeat(logit, RET_L, axis=1))

    pats = _na_patterns(n // GRID_W)
    y_na = _neighborhood(proj, nvt, cproj, _na_toeplitz(na_rpb[0].astype(F32), pats), pats)

    return _mix_ffn(x, y_ret, y_na, proj, mod4,
                    norm_post_mix[0].reshape(1, d), norm_pre_ffn[0].reshape(1, d),
                    norm_post_ffn[0].reshape(1, d),
                    w_ret_out[0].astype(BF16), w_na_out[0].astype(BF16), w_o[0].astype(BF16),
                    w_ff1[0].astype(BF16), w_ff2[0].astype(BF16))
```

```python
import functools
import math

import jax
import jax.numpy as jnp
import numpy as np
from jax import lax
from jax.experimental import pallas as pl
from jax.experimental.pallas import tpu as pltpu

F32 = jnp.float32
BF16 = jnp.bfloat16

D_MODEL = 1024
GRID_W = 64
HEADS = 8
HEAD_DIM = 64
MIX_W = HEADS * HEAD_DIM
NA_WIN_H = 8
NA_WIN_W = 16
ROPE_BASE = 10000.0
EPS = 1e-6
NEG_INF = -1e30
K_SCALE = HEAD_DIM ** -0.5
LOG2E = math.log2(math.e)

LANES = 128
V7X_VMEM_BYTES = 64 * 2 ** 20
VMEM_LIMIT = V7X_VMEM_BYTES - 8 * 2 ** 20

HEAD_GROUP = 4
GROUP_W = HEAD_GROUP * HEAD_DIM
N_GROUPS = HEADS // HEAD_GROUP
PROJ_TM = 1024
PROJ_SPLIT = 2
MIX_TM = 512
MIX_SPLIT = 1
FF_CHUNK = 1024
RET_L = 256
NA_ROWS = 4
NA_KROWS = 12
NA_Q = NA_ROWS * GRID_W
NA_K = NA_KROWS * GRID_W
RPB_ROWS = 2 * NA_WIN_H - 1
RPB_COLS = 2 * NA_WIN_W - 1

GATE_COLS = 2 * D_MODEL
COL_RQ, COL_RK, COL_RV, COL_RG, COL_NQ, COL_NK = (GATE_COLS // GROUP_W + i * N_GROUPS for i in range(6))
PROJ_COLS = GATE_COLS + 6 * MIX_W
CCOL_RK, CCOL_RV, CCOL_NK, CCOL_NV = (i * N_GROUPS for i in range(4))

_NT = (((1,), (1,)), ((), ()))
_TN = (((0,), (0,)), ((), ()))


def _cparams(*sem):
    return pltpu.CompilerParams(dimension_semantics=sem, vmem_limit_bytes=VMEM_LIMIT)


def _resident(shape):
    nd = len(shape)
    return pl.BlockSpec(shape, lambda *_: (0,) * nd, pipeline_mode=pl.Buffered(1))


def _rms(x):
    return x * lax.rsqrt(jnp.mean(x * x, axis=-1, keepdims=True) + EPS)


def _silu(x):
    return x * jax.nn.sigmoid(x)


def _head_masks():
    lane_head = lax.broadcasted_iota(jnp.int32, (1, GROUP_W), 1) // HEAD_DIM
    sel = [lane_head == h for h in range(HEAD_GROUP)]
    return sel, [m.astype(BF16) for m in sel]


def _ada_kernel(c_ref, w_ref, b_ref, o_ref):
    a = _silu(c_ref[...])
    o_ref[...] = jnp.dot(a, w_ref[...], preferred_element_type=F32,
                         precision=lax.Precision.HIGHEST) + b_ref[...]


def _ada(c_rows, w_ada, b_ada):
    rows, d = c_rows.shape
    cols = w_ada.shape[1]
    tn = d
    return pl.pallas_call(
        _ada_kernel,
        grid=(cols // tn,),
        in_specs=[pl.BlockSpec((rows, d), lambda j: (0, 0)),
                  pl.BlockSpec((d, tn), lambda j: (0, j)),
                  pl.BlockSpec((1, tn), lambda j: (0, j))],
        out_specs=pl.BlockSpec((rows, tn), lambda j: (0, j)),
        out_shape=jax.ShapeDtypeStruct((rows, cols), F32),
        compiler_params=_cparams("arbitrary"),
        name="ada_mod",
    )(c_rows, w_ada, b_ada.reshape(1, cols))


def _rope_slab(t, cos, sin_lo, sin_hi):
    return (t * cos + pltpu.roll(t, 3 * LANES // 4, axis=1) * sin_lo
            + pltpu.roll(t, LANES // 4, axis=1) * sin_hi)


def _modulated_norm(x_ref, sh_ref, sc_ref, g_ref):
    return ((_rms(x_ref[...]) * g_ref[...]) * (1.0 + sc_ref[...]) + sh_ref[...]).astype(BF16)


def _proj_latent_kernel(x_ref, sh_ref, sc_ref, g_ref, w_ref, wvt_ref, cos_ref, slo_ref, shi_ref,
                        o_ref, vt_ref):
    rope_lo, rope_hi = COL_RQ * GROUP_W, COL_RV * GROUP_W
    col_scale = {COL_RK * GROUP_W: K_SCALE, COL_NQ * GROUP_W: K_SCALE * LOG2E}
    tm = x_ref.shape[0]
    for r0 in range(0, tm, tm // PROJ_SPLIT):
        rows = slice(r0, r0 + tm // PROJ_SPLIT)
        x = x_ref[rows, :]
        h = ((_rms(x) * g_ref[...]) * (1.0 + sc_ref[...]) + sh_ref[...]).astype(BF16)
        for c0 in range(0, PROJ_COLS, MIX_W):
            r = jnp.dot(h, w_ref[:, c0:c0 + MIX_W], preferred_element_type=F32)
            if rope_lo <= c0 < rope_hi:
                cos, slo, shi = cos_ref[rows, :], slo_ref[rows, :], shi_ref[rows, :]
                r = jnp.concatenate(
                    [_rope_slab(r[:, j:j + LANES], cos, slo, shi) for j in range(0, MIX_W, LANES)], axis=1)
            if c0 in col_scale:
                r = r * col_scale[c0]
            o_ref[rows, c0:c0 + MIX_W] = r.astype(BF16)
        vt_ref[:, rows] = lax.dot_general(wvt_ref[...], h, _NT, preferred_element_type=F32).astype(BF16)


def _proj_latent(x, mod4, g, w_bf, wvt_bf, rope_tabs):
    b, n, d = x.shape
    tm = PROJ_TM
    mod_spec = lambda k: pl.BlockSpec((None, None, 1, d), lambda i, bb: (bb, k, 0, 0))
    tab_spec = pl.BlockSpec((tm, LANES), lambda i, bb: (i, 0))
    return pl.pallas_call(
        _proj_latent_kernel,
        grid=(n // tm, b),
        in_specs=[pl.BlockSpec((None, tm, d), lambda i, bb: (bb, i, 0)),
                  mod_spec(0), mod_spec(1), _resident((1, d)), _resident(w_bf.shape),
                  _resident(wvt_bf.shape), tab_spec, tab_spec, tab_spec],
        out_specs=[pl.BlockSpec((None, tm, PROJ_COLS), lambda i, bb: (bb, i, 0)),
                   pl.BlockSpec((None, MIX_W, tm), lambda i, bb: (bb, 0, i))],
        out_shape=[jax.ShapeDtypeStruct((b, n, PROJ_COLS), BF16),
                   jax.ShapeDtypeStruct((b, MIX_W, n), BF16)],
        compiler_params=_cparams("arbitrary", "arbitrary"),
        name="proj_latent",
    )(x, mod4, mod4, g, w_bf, wvt_bf, *rope_tabs)


def _proj_context_kernel(x_ref, sh_ref, sc_ref, g_ref, w_ref, o_ref):
    h = _modulated_norm(x_ref, sh_ref, sc_ref, g_ref)
    for c0 in range(0, o_ref.shape[-1], MIX_W):
        r = jnp.dot(h, w_ref[:, c0:c0 + MIX_W], preferred_element_type=F32)
        if c0 == CCOL_RK * GROUP_W:
            r = r * K_SCALE
        o_ref[:, c0:c0 + MIX_W] = r.astype(BF16)


def _proj_context(ctx_rows, mod4, ctx_row, g, w_bf):
    rows, d = ctx_rows.shape
    cols = w_bf.shape[1]
    tm = min(PROJ_TM, rows)
    mod_spec = lambda k: pl.BlockSpec((None, None, 1, d), lambda i: (ctx_row, k, 0, 0))
    return pl.pallas_call(
        _proj_context_kernel,
        grid=(rows // tm,),
        in_specs=[pl.BlockSpec((tm, d), lambda i: (i, 0)),
                  mod_spec(0), mod_spec(1), _resident((1, d)), _resident((d, cols))],
        out_specs=pl.BlockSpec((tm, cols), lambda i: (i, 0)),
        out_shape=jax.ShapeDtypeStruct((rows, cols), BF16),
        compiler_params=_cparams("arbitrary"),
        name="proj_context",
    )(ctx_rows, mod4, mod4, g, w_bf)


def _log_sigmoid(x):
    return jnp.minimum(x, 0.0) - jnp.log1p(jnp.exp(-jnp.abs(x)))


def _ret_kernel(q_ref, k_ref, v_ref, rg_ref, ck_ref, cv_ref, lg_lane_ref, lg_chunk_ref, o_ref,
                acc_ref, dec_ref, sf_ref, sb_ref):
    L = RET_L
    n = q_ref.shape[0]
    n_ctx = ck_ref.shape[0]
    nc, ncc = n // L, n_ctx // L

    lg = _log_sigmoid(lg_lane_ref[...])
    lgf, lgb = lg[0:1, :], lg[1:2, :]
    pos = lax.broadcasted_iota(jnp.int32, (L, 1), 0).astype(F32)
    qw_f = jnp.exp(lgf * (pos + 1.0))
    kw_f = jnp.exp(lgf * (L - 1.0 - pos))
    qw_b = jnp.exp(lgb * (L - pos))
    kw_b = jnp.exp(lgb * pos)
    gl_f = jnp.exp(lgf * float(L))
    gl_b = jnp.exp(lgb * float(L))

    lane_head = lax.broadcasted_iota(jnp.int32, (1, GROUP_W), 1) // HEAD_DIM
    row_head = lax.broadcasted_iota(jnp.int32, (GROUP_W, 1), 0) // HEAD_DIM
    block_diag = row_head == lane_head
    head_mean = jnp.where(block_diag, 1.0 / HEAD_DIM, 0.0).astype(BF16)
    _, head_mask = _head_masks()

    lgc = _log_sigmoid(lg_chunk_ref[...])
    diff = pos - lax.broadcasted_iota(jnp.int32, (1, L), 1).astype(F32)
    for h in range(HEAD_GROUP):
        f = jnp.exp(lgc[0:1, h * L:(h + 1) * L] * jnp.maximum(diff, 0.0))
        bk = jnp.exp(lgc[1:2, h * L:(h + 1) * L] * jnp.maximum(-diff, 0.0))
        dec_ref[h * L:(h + 1) * L, :] = jnp.where(diff >= 0.0, f, bk)

    def state_update(s_ref, k, v, kw, gl):
        kd = (k.astype(F32) * kw).astype(BF16)
        contrib = lax.dot_general(kd, v, _TN, preferred_element_type=F32)
        s_ref[...] = s_ref[...] * gl + jnp.where(block_diag, contrib, 0.0)

    sf_ref[...] = jnp.zeros_like(sf_ref)
    sb_ref[...] = jnp.zeros_like(sb_ref)

    def ctx_fwd(c, carry):
        rows = pl.ds(pl.multiple_of(c * L, L), L)
        state_update(sf_ref, ck_ref[rows, :], cv_ref[rows, :], kw_f, gl_f)
        return carry

    def ctx_bwd(i, carry):
        rows = pl.ds(pl.multiple_of((ncc - 1 - i) * L, L), L)
        state_update(sb_ref, ck_ref[rows, :], cv_ref[rows, :], kw_b, gl_b)
        return carry

    lax.fori_loop(0, ncc, ctx_fwd, 0)
    lax.fori_loop(0, ncc, ctx_bwd, 0)

    def fwd_chunk(c, carry):
        rows = pl.ds(pl.multiple_of(c * L, L), L)
        q, k, v = q_ref[rows, :], k_ref[rows, :], v_ref[rows, :]
        q_heads = jnp.concatenate([q * head_mask[h] for h in range(HEAD_GROUP)], axis=0)
        s = lax.dot_general(q_heads, k, _NT, preferred_element_type=F32)
        s = (s * dec_ref[...]).astype(BF16)
        s_cat = jnp.concatenate([s[h * L:(h + 1) * L, :] for h in range(HEAD_GROUP)], axis=1)
        v_heads = jnp.concatenate([v * head_mask[h] for h in range(HEAD_GROUP)], axis=0)
        o = jnp.dot(s_cat, v_heads, preferred_element_type=F32)
        o = o + jnp.dot(q, sf_ref[...].astype(BF16), preferred_element_type=F32) * qw_f
        acc_ref[rows, :] = o
        state_update(sf_ref, k, v, kw_f, gl_f)
        return carry

    lax.fori_loop(0, nc, fwd_chunk, 0, unroll=16)

    def bwd_chunk(i, carry):
        rows = pl.ds(pl.multiple_of((nc - 1 - i) * L, L), L)
        q, k, v = q_ref[rows, :], k_ref[rows, :], v_ref[rows, :]
        o = acc_ref[rows, :] + jnp.dot(q, sb_ref[...].astype(BF16), preferred_element_type=F32) * qw_b
        sq = o * o
        sq_hi = sq.astype(BF16)
        sq_lo = (sq - sq_hi.astype(F32)).astype(BF16)
        ms = (jnp.dot(sq_hi, head_mean, preferred_element_type=F32)
              + jnp.dot(sq_lo, head_mean, preferred_element_type=F32))
        y = o * lax.rsqrt(ms + EPS) * _silu(rg_ref[rows, :].astype(F32))
        o_ref[rows, :] = y.astype(BF16)
        state_update(sb_ref, k, v, kw_b, gl_b)
        return carry

    lax.fori_loop(0, nc, bwd_chunk, 0, unroll=16)


def _retention(proj, cproj, lg_lane, lg_chunk):
    b, n, _ = proj.shape
    n_ctx = cproj.shape[1]
    L = RET_L
    col = lambda c0: pl.BlockSpec((None, n, GROUP_W), lambda bb, g: (bb, 0, c0 + g))
    ccol = lambda c0: pl.BlockSpec((None, n_ctx, GROUP_W), lambda bb, g: (bb, 0, c0 + g))
    return pl.pallas_call(
        _ret_kernel,
        grid=(b, N_GROUPS),
        in_specs=[col(COL_RQ), col(COL_RK), col(COL_RV), col(COL_RG), ccol(CCOL_RK), ccol(CCOL_RV),
                  pl.BlockSpec((2, GROUP_W), lambda bb, g: (0, g)),
                  pl.BlockSpec((2, HEAD_GROUP * L), lambda bb, g: (0, g))],
        out_specs=pl.BlockSpec((None, n, GROUP_W), lambda bb, g: (bb, 0, g)),
        out_shape=jax.ShapeDtypeStruct((b, n, MIX_W), BF16),
        scratch_shapes=[pltpu.VMEM((n, GROUP_W), F32),
                        pltpu.VMEM((HEAD_GROUP * L, L), F32),
                        pltpu.VMEM((GROUP_W, GROUP_W), F32),
                        pltpu.VMEM((GROUP_W, GROUP_W), F32)],
        compiler_params=_cparams("arbitrary", "arbitrary"),
        name="retention",
    )(proj, proj, proj, proj, cproj, cproj, lg_lane, lg_chunk)


def _na_patterns(rows):
    nblk = rows // NA_ROWS
    assert rows % NA_ROWS == 0 and nblk >= 3 and rows >= NA_KROWS
    pats = []
    for qb in (0, 1, nblk - 1):
        kb = _na_key_start(qb, rows)
        qr = qb * NA_ROWS + np.arange(NA_ROWS)
        r0 = np.clip(qr - NA_WIN_H // 2, 0, rows - NA_WIN_H)
        assert np.all(r0 >= kb) and np.all(r0 + NA_WIN_H <= kb + NA_KROWS)
        pats.append((tuple(int(v) for v in r0 - kb), int(kb - qb * NA_ROWS + NA_WIN_H - 1)))
    for qb in range(1, nblk - 1):
        kb = _na_key_start(qb, rows)
        qr = qb * NA_ROWS + np.arange(NA_ROWS)
        r0 = np.clip(qr - NA_WIN_H // 2, 0, rows - NA_WIN_H)
        assert (tuple(int(v) for v in r0 - kb), int(kb - qb * NA_ROWS + NA_WIN_H - 1)) == pats[1]
    return pats


def _na_key_start(qb, rows):
    assert (NA_WIN_H // 2) % NA_ROWS == 0 and NA_KROWS % NA_ROWS == 0
    return min(max(qb * NA_ROWS - NA_WIN_H // 2, 0), rows - NA_KROWS)


def _na_slot_range(pats):
    lo = min(off - (NA_ROWS - 1) for _, off in pats)
    hi = max(off + NA_KROWS - 1 for _, off in pats)
    return lo, hi


def _na_kernel(q_ref, k_ref, vt_ref, ck_ref, cv_ref, tz_ref, o_ref,
               bias_ref, cvt_ref, s0_ref, s1_ref, m0_ref, m1_ref, *, pats):
    n = q_ref.shape[0]
    n_ctx = ck_ref.shape[0]
    rows = n // GRID_W
    nblk = rows // NA_ROWS
    _, head_mask = _head_masks()
    _, dr_hi = _na_slot_range(pats)
    n_keys = NA_K + n_ctx

    @pl.when(pl.program_id(1) == 0)
    def _build_bias():
        key = lax.broadcasted_iota(jnp.int32, (NA_K, NA_Q), 0)
        qry = lax.broadcasted_iota(jnp.int32, (NA_K, NA_Q), 1)
        j, kc = key // GRID_W, key % GRID_W
        i, qc = qry // GRID_W, qry % GRID_W
        c0 = jnp.clip(qc - NA_WIN_W // 2, 0, GRID_W - NA_WIN_W)
        col_ok = (kc >= c0) & (kc < c0 + NA_WIN_W)
        for p, (r0, off) in enumerate(pats):
            lo = jnp.zeros_like(i)
            for ii in range(NA_ROWS):
                lo = jnp.where(i == ii, r0[ii], lo)
            ok = col_ok & (j >= lo) & (j < lo + NA_WIN_H)
            for h in range(HEAD_GROUP):
                t = tz_ref[h]
                for jj in range(NA_KROWS):
                    slot = dr_hi - (jj + off)
                    tile = t[:, slot * GRID_W:slot * GRID_W + NA_Q] * LOG2E
                    rs = slice(jj * GRID_W, (jj + 1) * GRID_W)
                    bias_ref[p, rs, h * NA_Q:(h + 1) * NA_Q] = jnp.where(ok[rs, :], tile, NEG_INF)

    cvt_ref[...] = cv_ref[...].astype(F32).T.astype(BF16)
    ones_rows = jnp.ones((2 * 8, n_keys), BF16)

    def block_rows(qb):
        kb = jnp.clip(qb * NA_ROWS - NA_WIN_H // 2, 0, rows - NA_KROWS)
        q_rows = pl.ds(pl.multiple_of(qb * NA_Q, NA_Q), NA_Q)
        k_tok = pl.ds(pl.multiple_of(kb * GRID_W, NA_Q), NA_K)
        return q_rows, k_tok

    def scores(qb, s_ref, m_ref):
        q_rows, k_tok = block_rows(qb)
        pat = jnp.where(qb == 0, 0, jnp.where(qb == nblk - 1, 2, 1))
        q = q_ref[q_rows, :]
        q_heads = jnp.concatenate([q * head_mask[h] for h in range(HEAD_GROUP)], axis=0)
        s_ref[0:NA_K, :] = (lax.dot_general(k_ref[k_tok, :], q_heads, _NT, preferred_element_type=F32)
                            + bias_ref[pat])
        s_ref[NA_K:n_keys, :] = lax.dot_general(ck_ref[...], q_heads, _NT, preferred_element_type=F32)
        m_ref[...] = jnp.max(s_ref[...], axis=0, keepdims=True)

    def attend(qb, s_ref, m_ref):
        q_rows, k_tok = block_rows(qb)
        p = jnp.exp2(s_ref[...] - m_ref[...]).astype(BF16)
        outs = []
        for h in range(HEAD_GROUP):
            hs = slice(h * HEAD_DIM, (h + 1) * HEAD_DIM)
            v_aug = jnp.concatenate(
                [jnp.concatenate([vt_ref[hs, k_tok], cvt_ref[hs, :]], axis=1), ones_rows], axis=0)
            oh = jnp.dot(v_aug, p[:, h * NA_Q:(h + 1) * NA_Q], preferred_element_type=F32)
            outs.append(oh[0:HEAD_DIM, :] / oh[HEAD_DIM:HEAD_DIM + 1, :])
        o_ref[q_rows, :] = jnp.concatenate(outs, axis=0).T.astype(BF16)

    assert nblk % 2 == 0
    scores(0, s0_ref, m0_ref)

    def pair(t, carry):
        qb = 2 * t
        scores(qb + 1, s1_ref, m1_ref)
        attend(qb, s0_ref, m0_ref)
        scores(jnp.minimum(qb + 2, nblk - 1), s0_ref, m0_ref)
        attend(qb + 1, s1_ref, m1_ref)
        return carry

    lax.fori_loop(0, nblk // 2, pair, 0, unroll=4)


def _na_toeplitz(rpb, pats):
    dr_lo, dr_hi = _na_slot_range(pats)
    slots = dr_hi - dr_lo + 1
    slots += (-slots) % (LANES // GRID_W)
    row_sel = np.zeros((RPB_ROWS, slots), np.float32)
    for s in range(slots):
        if 0 <= dr_hi - s < RPB_ROWS:
            row_sel[dr_hi - s, s] = 1.0
    col = np.arange(GRID_W)
    dc = col[:, None] - col[None, :] + NA_WIN_W - 1
    col_sel = (dc[None, :, :] == np.arange(RPB_COLS)[:, None, None]).astype(np.float32)
    t = jnp.einsum('hab,as,bkq->hksq', rpb, row_sel, col_sel, precision=lax.Precision.HIGHEST)
    return t.reshape(rpb.shape[0], GRID_W, slots * GRID_W)


def _neighborhood(proj, vt, cproj, tz, pats):
    b, n, _ = proj.shape
    n_ctx = cproj.shape[1]
    col = lambda c0: pl.BlockSpec((None, n, GROUP_W), lambda g, bb: (bb, 0, c0 + g))
    ccol = lambda c0: pl.BlockSpec((None, n_ctx, GROUP_W), lambda g, bb: (bb, 0, c0 + g))
    return pl.pallas_call(
        functools.partial(_na_kernel, pats=pats),
        grid=(N_GROUPS, b),
        in_specs=[col(COL_NQ), col(COL_NK),
                  pl.BlockSpec((None, GROUP_W, n), lambda g, bb: (bb, g, 0)),
                  ccol(CCOL_NK), ccol(CCOL_NV),
                  pl.BlockSpec((HEAD_GROUP,) + tz.shape[1:], lambda g, bb: (g, 0, 0))],
        out_specs=pl.BlockSpec((None, n, GROUP_W), lambda g, bb: (bb, 0, g)),
        out_shape=jax.ShapeDtypeStruct((b, n, MIX_W), BF16),
        scratch_shapes=[pltpu.VMEM((len(pats), NA_K, HEAD_GROUP * NA_Q), F32),
                        pltpu.VMEM((GROUP_W, n_ctx), BF16),
                        pltpu.VMEM((NA_K + n_ctx, HEAD_GROUP * NA_Q), F32),
                        pltpu.VMEM((NA_K + n_ctx, HEAD_GROUP * NA_Q), F32),
                        pltpu.VMEM((1, HEAD_GROUP * NA_Q), F32),
                        pltpu.VMEM((1, HEAD_GROUP * NA_Q), F32)],
        compiler_params=_cparams("arbitrary", "arbitrary"),
        name="neighborhood",
    )(proj, proj, vt, cproj, cproj, tz)


def _mix_ffn_kernel(x_ref, yr_ref, yn_ref, gr_ref, gn_ref, gt1_ref, sh2_ref, sc2_ref, gt2_ref,
                    g_post_mix_ref, g_pre_ffn_ref, g_post_ffn_ref,
                    w_ret_ref, w_na_ref, w_o_ref, w1_ref, w2_ref, o_ref):
    tm = x_ref.shape[0]
    d_ff = w1_ref.shape[1]
    for r0 in range(0, tm, tm // MIX_SPLIT):
        rows = slice(r0, r0 + tm // MIX_SPLIT)
        a = jnp.dot(yr_ref[rows, :], w_ret_ref[...], preferred_element_type=F32)
        bb = jnp.dot(yn_ref[rows, :], w_na_ref[...], preferred_element_type=F32)
        y = (jax.nn.sigmoid(gr_ref[rows, :].astype(F32)) * a
             + jax.nn.sigmoid(gn_ref[rows, :].astype(F32)) * bb)
        y = jnp.dot(y.astype(BF16), w_o_ref[...], preferred_element_type=F32)
        x1 = x_ref[rows, :] + gt1_ref[...] * (_rms(y) * g_post_mix_ref[...])
        h2 = ((_rms(x1) * g_pre_ffn_ref[...]) * (1.0 + sc2_ref[...]) + sh2_ref[...]).astype(BF16)
        f = jnp.zeros(x1.shape, F32)
        for c0 in range(0, d_ff, FF_CHUNK):
            u = jnp.dot(h2, w1_ref[:, c0:c0 + FF_CHUNK], preferred_element_type=F32)
            u = jnp.square(jnp.maximum(u, 0.0)).astype(BF16)
            f = f + jnp.dot(u, w2_ref[c0:c0 + FF_CHUNK, :], preferred_element_type=F32)
        o_ref[rows, :] = x1 + gt2_ref[...] * (_rms(f) * g_post_ffn_ref[...])


def _mix_ffn(x, y_ret, y_na, proj, mod4, g_post_mix, g_pre_ffn, g_post_ffn,
             w_ret, w_na, w_o, w1, w2):
    b, n, d = x.shape
    tm = MIX_TM
    tok = lambda w: pl.BlockSpec((None, tm, w), lambda i, bb: (bb, i, 0))
    mod_spec = lambda k: pl.BlockSpec((None, None, 1, d), lambda i, bb: (bb, k, 0, 0))
    return pl.pallas_call(
        _mix_ffn_kernel,
        grid=(n // tm, b),
        in_specs=[tok(d), tok(MIX_W), tok(MIX_W),
                  pl.BlockSpec((None, tm, d), lambda i, bb: (bb, i, 0)),
                  pl.BlockSpec((None, tm, d), lambda i, bb: (bb, i, 1)),
                  mod_spec(2), mod_spec(3), mod_spec(4), mod_spec(5),
                  _resident((1, d)), _resident((1, d)), _resident((1, d)),
                  _resident(w_ret.shape), _resident(w_na.shape), _resident(w_o.shape),
                  _resident(w1.shape), _resident(w2.shape)],
        out_specs=tok(d),
        out_shape=jax.ShapeDtypeStruct((b, n, d), F32),
        compiler_params=_cparams("arbitrary", "arbitrary"),
        name="mix_ffn",
    )(x, y_ret, y_na, proj, proj, mod4, mod4, mod4, mod4, g_post_mix, g_pre_ffn, g_post_ffn,
      w_ret, w_na, w_o, w1, w2)


def _rope_tables(n):
    pos = np.arange(n)
    row = (pos // GRID_W).astype(np.float64)
    colp = (pos % GRID_W).astype(np.float64)
    d_axis = HEAD_DIM // 2
    inv = ROPE_BASE ** (-np.arange(0, d_axis, 2, dtype=np.float64) / d_axis)
    ang = np.concatenate([row[:, None] * inv, colp[:, None] * inv], axis=-1)
    cos, sin = np.cos(ang), np.sin(ang)
    zero = np.zeros_like(sin)
    reps = LANES // HEAD_DIM
    cos_t = np.tile(np.concatenate([cos, cos], axis=-1), (1, reps))
    sin_lo = np.tile(np.concatenate([-sin, zero], axis=-1), (1, reps))
    sin_hi = np.tile(np.concatenate([zero, sin], axis=-1), (1, reps))
    return tuple(jnp.asarray(t, F32) for t in (cos_t, sin_lo, sin_hi))


def kernel(x, c, ctx, c_ctx, w_ada, b_ada, norm_pre_mix, norm_post_mix, norm_pre_ffn, norm_post_ffn,
           w_in, ret_decay_logit, w_ret_out, na_rpb, w_na_out, w_o, w_ff1, w_ff2):
    b, n, d = x.shape
    n_ctx = ctx.shape[1]
    assert w_ada.shape[0] == 1, "single-layer block"
    assert n % NA_Q == 0 and n % PROJ_TM == 0 and n % MIX_TM == 0 and n % RET_L == 0
    assert n_ctx % RET_L == 0 and d == D_MODEL and 7 * MIX_W + GATE_COLS == w_in.shape[2]

    pad = (-(b + 1)) % 8
    c_rows = jnp.concatenate([c, c_ctx[None, :], jnp.zeros((pad, d), F32)], axis=0)
    mod = _ada(c_rows, w_ada[0], b_ada[0])
    mod4 = mod.reshape(c_rows.shape[0], 6, 1, d)

    w_all = w_in[0].astype(BF16)
    w_ctx_bf = jnp.concatenate([w_all[:, MIX_W:3 * MIX_W], w_all[:, 5 * MIX_W:7 * MIX_W]], axis=1)
    w_lat_bf = jnp.concatenate([w_all[:, 7 * MIX_W:], w_all[:, :6 * MIX_W]], axis=1)
    w_nvt_bf = w_all[:, 6 * MIX_W:7 * MIX_W].T
    g_pre_mix = norm_pre_mix[0].reshape(1, d)

    proj, nvt = _proj_latent(x, mod4, g_pre_mix, w_lat_bf, w_nvt_bf, _rope_tables(n))
    cproj = _proj_context(ctx.reshape(b * n_ctx, d), mod4, b, g_pre_mix, w_ctx_bf)
    cproj = cproj.reshape(b, n_ctx, 4 * MIX_W)

    logit = ret_decay_logit[0].astype(F32)
    y_ret = _retention(proj, cproj, jnp.repeat(logit, HEAD_DIM, axis=1), jnp.repeat(logit, RET_L, axis=1))

    pats = _na_patterns(n // GRID_W)
    y_na = _neighborhood(proj, nvt, cproj, _na_toeplitz(na_rpb[0].astype(F32), pats), pats)

    return _mix_ffn(x, y_ret, y_na, proj, mod4,
                    norm_post_mix[0].reshape(1, d), norm_pre_ffn[0].reshape(1, d),
                    norm_post_ffn[0].reshape(1, d),
                    w_ret_out[0].astype(BF16), w_na_out[0].astype(BF16), w_o[0].astype(BF16),
                    w_ff1[0].astype(BF16), w_ff2[0].astype(BF16))
```

```python
import functools
import math

import jax
import jax.numpy as jnp
import numpy as np
from jax import lax
from jax.experimental import pallas as pl
from jax.experimental.pallas import tpu as pltpu

F32 = jnp.float32
BF16 = jnp.bfloat16

D_MODEL = 1024
GRID_W = 64
HEADS = 8
HEAD_DIM = 64
MIX_W = HEADS * HEAD_DIM
NA_WIN_H = 8
NA_WIN_W = 16
ROPE_BASE = 10000.0
EPS = 1e-6
NEG_INF = -1e30
K_SCALE = HEAD_DIM ** -0.5
LOG2E = math.log2(math.e)

LANES = 128
V7X_VMEM_BYTES = 64 * 2 ** 20
VMEM_LIMIT = V7X_VMEM_BYTES - 8 * 2 ** 20

HEAD_GROUP = 4
GROUP_W = HEAD_GROUP * HEAD_DIM
N_GROUPS = HEADS // HEAD_GROUP
PROJ_TM = 1024
PROJ_SPLIT = 2
MIX_TM = 512
MIX_STAGE_ELEMS = 512 * 1024
FF_CHUNK = 1024
RET_L = 256
NA_ROWS = 4
NA_KROWS = 12
NA_Q = NA_ROWS * GRID_W
NA_K = NA_KROWS * GRID_W
RPB_ROWS = 2 * NA_WIN_H - 1
RPB_COLS = 2 * NA_WIN_W - 1

GATE_COLS = 2 * D_MODEL
COL_RQ, COL_RK, COL_RV, COL_RG, COL_NQ, COL_NK = (GATE_COLS // GROUP_W + i * N_GROUPS for i in range(6))
PROJ_COLS = GATE_COLS + 6 * MIX_W
CCOL_RK, CCOL_RV, CCOL_NK, CCOL_NV = (i * N_GROUPS for i in range(4))

_NT = (((1,), (1,)), ((), ()))
_TN = (((0,), (0,)), ((), ()))


def _cparams(*sem):
    return pltpu.CompilerParams(dimension_semantics=sem, vmem_limit_bytes=VMEM_LIMIT)


def _resident(shape):
    nd = len(shape)
    return pl.BlockSpec(shape, lambda *_: (0,) * nd, pipeline_mode=pl.Buffered(1))


def _rms(x):
    return x * lax.rsqrt(jnp.mean(x * x, axis=-1, keepdims=True) + EPS)


def _silu(x):
    return x * jax.nn.sigmoid(x)


def _head_masks():
    lane_head = lax.broadcasted_iota(jnp.int32, (1, GROUP_W), 1) // HEAD_DIM
    sel = [lane_head == h for h in range(HEAD_GROUP)]
    return sel, [m.astype(BF16) for m in sel]


def _ada_kernel(c_ref, w_ref, b_ref, o_ref):
    a = _silu(c_ref[...])
    o_ref[...] = jnp.dot(a, w_ref[...], preferred_element_type=F32,
                         precision=lax.Precision.HIGHEST) + b_ref[...]


def _ada(c_rows, w_ada, b_ada):
    rows, d = c_rows.shape
    cols = w_ada.shape[1]
    tn = d
    return pl.pallas_call(
        _ada_kernel,
        grid=(cols // tn,),
        in_specs=[pl.BlockSpec((rows, d), lambda j: (0, 0)),
                  pl.BlockSpec((d, tn), lambda j: (0, j)),
                  pl.BlockSpec((1, tn), lambda j: (0, j))],
        out_specs=pl.BlockSpec((rows, tn), lambda j: (0, j)),
        out_shape=jax.ShapeDtypeStruct((rows, cols), F32),
        compiler_params=_cparams("arbitrary"),
        name="ada_mod",
    )(c_rows, w_ada, b_ada.reshape(1, cols))


def _rope_slab(t, cos, sin_lo, sin_hi):
    return (t * cos + pltpu.roll(t, 3 * LANES // 4, axis=1) * sin_lo
            + pltpu.roll(t, LANES // 4, axis=1) * sin_hi)


def _modulated_norm(x_ref, sh_ref, sc_ref, g_ref):
    return ((_rms(x_ref[...]) * g_ref[...]) * (1.0 + sc_ref[...]) + sh_ref[...]).astype(BF16)


def _proj_latent_kernel(x_ref, sh_ref, sc_ref, g_ref, w_ref, wvt_ref, cos_ref, slo_ref, shi_ref,
                        o_ref, vt_ref):
    rope_lo, rope_hi = COL_RQ * GROUP_W, COL_RV * GROUP_W
    col_scale = {COL_RK * GROUP_W: K_SCALE, COL_NQ * GROUP_W: K_SCALE * LOG2E}
    tm = x_ref.shape[0]
    for r0 in range(0, tm, tm // PROJ_SPLIT):
        rows = slice(r0, r0 + tm // PROJ_SPLIT)
        x = x_ref[rows, :]
        h = ((_rms(x) * g_ref[...]) * (1.0 + sc_ref[...]) + sh_ref[...]).astype(BF16)
        for c0 in range(0, PROJ_COLS, MIX_W):
            r = jnp.dot(h, w_ref[:, c0:c0 + MIX_W], preferred_element_type=F32)
            if rope_lo <= c0 < rope_hi:
                cos, slo, shi = cos_ref[rows, :], slo_ref[rows, :], shi_ref[rows, :]
                r = jnp.concatenate(
                    [_rope_slab(r[:, j:j + LANES], cos, slo, shi) for j in range(0, MIX_W, LANES)], axis=1)
            if c0 in col_scale:
                r = r * col_scale[c0]
            o_ref[rows, c0:c0 + MIX_W] = r.astype(BF16)
        vt_ref[:, rows] = lax.dot_general(wvt_ref[...], h, _NT, preferred_element_type=F32).astype(BF16)


def _proj_latent(x, mod4, g, w_bf, wvt_bf, rope_tabs):
    b, n, d = x.shape
    tm = PROJ_TM
    mod_spec = lambda k: pl.BlockSpec((None, None, 1, d), lambda i, bb: (bb, k, 0, 0))
    tab_spec = pl.BlockSpec((tm, LANES), lambda i, bb: (i, 0))
    return pl.pallas_call(
        _proj_latent_kernel,
        grid=(n // tm, b),
        in_specs=[pl.BlockSpec((None, tm, d), lambda i, bb: (bb, i, 0)),
                  mod_spec(0), mod_spec(1), _resident((1, d)), _resident(w_bf.shape),
                  _resident(wvt_bf.shape), tab_spec, tab_spec, tab_spec],
        out_specs=[pl.BlockSpec((None, tm, PROJ_COLS), lambda i, bb: (bb, i, 0)),
                   pl.BlockSpec((None, MIX_W, tm), lambda i, bb: (bb, 0, i))],
        out_shape=[jax.ShapeDtypeStruct((b, n, PROJ_COLS), BF16),
                   jax.ShapeDtypeStruct((b, MIX_W, n), BF16)],
        compiler_params=_cparams("arbitrary", "arbitrary"),
        name="proj_latent",
    )(x, mod4, mod4, g, w_bf, wvt_bf, *rope_tabs)


def _proj_context_kernel(x_ref, sh_ref, sc_ref, g_ref, w_ref, o_ref):
    h = _modulated_norm(x_ref, sh_ref, sc_ref, g_ref)
    for c0 in range(0, o_ref.shape[-1], MIX_W):
        r = jnp.dot(h, w_ref[:, c0:c0 + MIX_W], preferred_element_type=F32)
        if c0 == CCOL_RK * GROUP_W:
            r = r * K_SCALE
        o_ref[:, c0:c0 + MIX_W] = r.astype(BF16)


def _proj_context(ctx_rows, mod4, ctx_row, g, w_bf):
    rows, d = ctx_rows.shape
    cols = w_bf.shape[1]
    tm = min(PROJ_TM, rows)
    mod_spec = lambda k: pl.BlockSpec((None, None, 1, d), lambda i: (ctx_row, k, 0, 0))
    return pl.pallas_call(
        _proj_context_kernel,
        grid=(rows // tm,),
        in_specs=[pl.BlockSpec((tm, d), lambda i: (i, 0)),
                  mod_spec(0), mod_spec(1), _resident((1, d)), _resident((d, cols))],
        out_specs=pl.BlockSpec((tm, cols), lambda i: (i, 0)),
        out_shape=jax.ShapeDtypeStruct((rows, cols), BF16),
        compiler_params=_cparams("arbitrary"),
        name="proj_context",
    )(ctx_rows, mod4, mod4, g, w_bf)


def _log_sigmoid(x):
    return jnp.minimum(x, 0.0) - jnp.log1p(jnp.exp(-jnp.abs(x)))


def _ret_kernel(q_ref, k_ref, v_ref, rg_ref, ck_ref, cv_ref, lg_lane_ref, lg_chunk_ref, o_ref,
                acc_ref, dec_ref, sf_ref, sb_ref):
    L = RET_L
    n = q_ref.shape[0]
    n_ctx = ck_ref.shape[0]
    nc, ncc = n // L, n_ctx // L

    lg = _log_sigmoid(lg_lane_ref[...])
    lgf, lgb = lg[0:1, :], lg[1:2, :]
    pos = lax.broadcasted_iota(jnp.int32, (L, 1), 0).astype(F32)
    qw_f = jnp.exp(lgf * (pos + 1.0))
    kw_f = jnp.exp(lgf * (L - 1.0 - pos))
    qw_b = jnp.exp(lgb * (L - pos))
    kw_b = jnp.exp(lgb * pos)
    gl_f = jnp.exp(lgf * float(L))
    gl_b = jnp.exp(lgb * float(L))

    lane_head = lax.broadcasted_iota(jnp.int32, (1, GROUP_W), 1) // HEAD_DIM
    row_head = lax.broadcasted_iota(jnp.int32, (GROUP_W, 1), 0) // HEAD_DIM
    block_diag = row_head == lane_head
    head_mean = jnp.where(block_diag, 1.0 / HEAD_DIM, 0.0).astype(BF16)
    _, head_mask = _head_masks()

    lgc = _log_sigmoid(lg_chunk_ref[...])
    diff = pos - lax.broadcasted_iota(jnp.int32, (1, L), 1).astype(F32)
    for h in range(HEAD_GROUP):
        f = jnp.exp(lgc[0:1, h * L:(h + 1) * L] * jnp.maximum(diff, 0.0))
        bk = jnp.exp(lgc[1:2, h * L:(h + 1) * L] * jnp.maximum(-diff, 0.0))
        dec_ref[h * L:(h + 1) * L, :] = jnp.where(diff >= 0.0, f, bk)

    def state_update(s_ref, k, v, kw, gl):
        kd = (k.astype(F32) * kw).astype(BF16)
        contrib = lax.dot_general(kd, v, _TN, preferred_element_type=F32)
        s_ref[...] = s_ref[...] * gl + jnp.where(block_diag, contrib, 0.0)

    sf_ref[...] = jnp.zeros_like(sf_ref)
    sb_ref[...] = jnp.zeros_like(sb_ref)

    def ctx_fwd(c, carry):
        rows = pl.ds(pl.multiple_of(c * L, L), L)
        state_update(sf_ref, ck_ref[rows, :], cv_ref[rows, :], kw_f, gl_f)
        return carry

    def ctx_bwd(i, carry):
        rows = pl.ds(pl.multiple_of((ncc - 1 - i) * L, L), L)
        state_update(sb_ref, ck_ref[rows, :], cv_ref[rows, :], kw_b, gl_b)
        return carry

    lax.fori_loop(0, ncc, ctx_fwd, 0)
    lax.fori_loop(0, ncc, ctx_bwd, 0)

    def fwd_chunk(c, carry):
        rows = pl.ds(pl.multiple_of(c * L, L), L)
        q, k, v = q_ref[rows, :], k_ref[rows, :], v_ref[rows, :]
        q_heads = jnp.concatenate([q * head_mask[h] for h in range(HEAD_GROUP)], axis=0)
        s = lax.dot_general(q_heads, k, _NT, preferred_element_type=F32)
        s = (s * dec_ref[...]).astype(BF16)
        s_cat = jnp.concatenate([s[h * L:(h + 1) * L, :] for h in range(HEAD_GROUP)], axis=1)
        v_heads = jnp.concatenate([v * head_mask[h] for h in range(HEAD_GROUP)], axis=0)
        o = jnp.dot(s_cat, v_heads, preferred_element_type=F32)
        o = o + jnp.dot(q, sf_ref[...].astype(BF16), preferred_element_type=F32) * qw_f
        acc_ref[rows, :] = o
        state_update(sf_ref, k, v, kw_f, gl_f)
        return carry

    lax.fori_loop(0, nc, fwd_chunk, 0, unroll=16)

    def bwd_chunk(i, carry):
        rows = pl.ds(pl.multiple_of((nc - 1 - i) * L, L), L)
        q, k, v = q_ref[rows, :], k_ref[rows, :], v_ref[rows, :]
        o = acc_ref[rows, :] + jnp.dot(q, sb_ref[...].astype(BF16), preferred_element_type=F32) * qw_b
        sq = o * o
        sq_hi = sq.astype(BF16)
        sq_lo = (sq - sq_hi.astype(F32)).astype(BF16)
        ms = (jnp.dot(sq_hi, head_mean, preferred_element_type=F32)
              + jnp.dot(sq_lo, head_mean, preferred_element_type=F32))
        y = o * lax.rsqrt(ms + EPS) * _silu(rg_ref[rows, :].astype(F32))
        o_ref[rows, :] = y.astype(BF16)
        state_update(sb_ref, k, v, kw_b, gl_b)
        return carry

    lax.fori_loop(0, nc, bwd_chunk, 0, unroll=16)


def _retention(proj, cproj, lg_lane, lg_chunk):
    b, n, _ = proj.shape
    n_ctx = cproj.shape[1]
    L = RET_L
    col = lambda c0: pl.BlockSpec((None, n, GROUP_W), lambda bb, g: (bb, 0, c0 + g))
    ccol = lambda c0: pl.BlockSpec((None, n_ctx, GROUP_W), lambda bb, g: (bb, 0, c0 + g))
    return pl.pallas_call(
        _ret_kernel,
        grid=(b, N_GROUPS),
        in_specs=[col(COL_RQ), col(COL_RK), col(COL_RV), col(COL_RG), ccol(CCOL_RK), ccol(CCOL_RV),
                  pl.BlockSpec((2, GROUP_W), lambda bb, g: (0, g)),
                  pl.BlockSpec((2, HEAD_GROUP * L), lambda bb, g: (0, g))],
        out_specs=pl.BlockSpec((None, n, GROUP_W), lambda bb, g: (bb, 0, g)),
        out_shape=jax.ShapeDtypeStruct((b, n, MIX_W), BF16),
        scratch_shapes=[pltpu.VMEM((n, GROUP_W), F32),
                        pltpu.VMEM((HEAD_GROUP * L, L), F32),
                        pltpu.VMEM((GROUP_W, GROUP_W), F32),
                        pltpu.VMEM((GROUP_W, GROUP_W), F32)],
        compiler_params=_cparams("arbitrary", "arbitrary"),
        name="retention",
    )(proj, proj, proj, proj, cproj, cproj, lg_lane, lg_chunk)


def _na_patterns(rows):
    nblk = rows // NA_ROWS
    assert rows % NA_ROWS == 0 and nblk >= 3 and rows >= NA_KROWS
    pats = []
    for qb in (0, 1, nblk - 1):
        kb = _na_key_start(qb, rows)
        qr = qb * NA_ROWS + np.arange(NA_ROWS)
        r0 = np.clip(qr - NA_WIN_H // 2, 0, rows - NA_WIN_H)
        assert np.all(r0 >= kb) and np.all(r0 + NA_WIN_H <= kb + NA_KROWS)
        pats.append((tuple(int(v) for v in r0 - kb), int(kb - qb * NA_ROWS + NA_WIN_H - 1)))
    for qb in range(1, nblk - 1):
        kb = _na_key_start(qb, rows)
        qr = qb * NA_ROWS + np.arange(NA_ROWS)
        r0 = np.clip(qr - NA_WIN_H // 2, 0, rows - NA_WIN_H)
        assert (tuple(int(v) for v in r0 - kb), int(kb - qb * NA_ROWS + NA_WIN_H - 1)) == pats[1]
    return pats


def _na_key_start(qb, rows):
    assert (NA_WIN_H // 2) % NA_ROWS == 0 and NA_KROWS % NA_ROWS == 0
    return min(max(qb * NA_ROWS - NA_WIN_H // 2, 0), rows - NA_KROWS)


def _na_slot_range(pats):
    lo = min(off - (NA_ROWS - 1) for _, off in pats)
    hi = max(off + NA_KROWS - 1 for _, off in pats)
    return lo, hi


def _na_kernel(q_ref, k_ref, vt_ref, ck_ref, cv_ref, tz_ref, o_ref,
               bias_ref, cvt_ref, s0_ref, s1_ref, m0_ref, m1_ref, *, pats):
    n = q_ref.shape[0]
    n_ctx = ck_ref.shape[0]
    rows = n // GRID_W
    nblk = rows // NA_ROWS
    _, head_mask = _head_masks()
    _, dr_hi = _na_slot_range(pats)
    n_keys = NA_K + n_ctx

    @pl.when(pl.program_id(1) == 0)
    def _build_bias():
        key = lax.broadcasted_iota(jnp.int32, (NA_K, NA_Q), 0)
        qry = lax.broadcasted_iota(jnp.int32, (NA_K, NA_Q), 1)
        j, kc = key // GRID_W, key % GRID_W
        i, qc = qry // GRID_W, qry % GRID_W
        c0 = jnp.clip(qc - NA_WIN_W // 2, 0, GRID_W - NA_WIN_W)
        col_ok = (kc >= c0) & (kc < c0 + NA_WIN_W)
        for p, (r0, off) in enumerate(pats):
            lo = jnp.zeros_like(i)
            for ii in range(NA_ROWS):
                lo = jnp.where(i == ii, r0[ii], lo)
            ok = col_ok & (j >= lo) & (j < lo + NA_WIN_H)
            for h in range(HEAD_GROUP):
                t = tz_ref[h]
                for jj in range(NA_KROWS):
                    slot = dr_hi - (jj + off)
                    tile = t[:, slot * GRID_W:slot * GRID_W + NA_Q] * LOG2E
                    rs = slice(jj * GRID_W, (jj + 1) * GRID_W)
                    bias_ref[p, rs, h * NA_Q:(h + 1) * NA_Q] = jnp.where(ok[rs, :], tile, NEG_INF)

    cvt_ref[...] = cv_ref[...].astype(F32).T.astype(BF16)
    ones_rows = jnp.ones((2 * 8, n_keys), BF16)

    def block_rows(qb):
        kb = jnp.clip(qb * NA_ROWS - NA_WIN_H // 2, 0, rows - NA_KROWS)
        q_rows = pl.ds(pl.multiple_of(qb * NA_Q, NA_Q), NA_Q)
        k_tok = pl.ds(pl.multiple_of(kb * GRID_W, NA_Q), NA_K)
        return q_rows, k_tok

    def scores(qb, s_ref, m_ref):
        q_rows, k_tok = block_rows(qb)
        pat = jnp.where(qb == 0, 0, jnp.where(qb == nblk - 1, 2, 1))
        q = q_ref[q_rows, :]
        q_heads = jnp.concatenate([q * head_mask[h] for h in range(HEAD_GROUP)], axis=0)
        s_ref[0:NA_K, :] = (lax.dot_general(k_ref[k_tok, :], q_heads, _NT, preferred_element_type=F32)
                            + bias_ref[pat])
        s_ref[NA_K:n_keys, :] = lax.dot_general(ck_ref[...], q_heads, _NT, preferred_element_type=F32)
        m_ref[...] = jnp.max(s_ref[...], axis=0, keepdims=True)

    def attend(qb, s_ref, m_ref):
        q_rows, k_tok = block_rows(qb)
        p = jnp.exp2(s_ref[...] - m_ref[...]).astype(BF16)
        outs = []
        for h in range(HEAD_GROUP):
            hs = slice(h * HEAD_DIM, (h + 1) * HEAD_DIM)
            v_aug = jnp.concatenate(
                [jnp.concatenate([vt_ref[hs, k_tok], cvt_ref[hs, :]], axis=1), ones_rows], axis=0)
            oh = jnp.dot(v_aug, p[:, h * NA_Q:(h + 1) * NA_Q], preferred_element_type=F32)
            outs.append(oh[0:HEAD_DIM, :] / oh[HEAD_DIM:HEAD_DIM + 1, :])
        o_ref[q_rows, :] = jnp.concatenate(outs, axis=0).T.astype(BF16)

    assert nblk % 2 == 0
    scores(0, s0_ref, m0_ref)

    def pair(t, carry):
        qb = 2 * t
        scores(qb + 1, s1_ref, m1_ref)
        attend(qb, s0_ref, m0_ref)
        scores(jnp.minimum(qb + 2, nblk - 1), s0_ref, m0_ref)
        attend(qb + 1, s1_ref, m1_ref)
        return carry

    lax.fori_loop(0, nblk // 2, pair, 0, unroll=4)


def _na_toeplitz(rpb, pats):
    dr_lo, dr_hi = _na_slot_range(pats)
    slots = dr_hi - dr_lo + 1
    slots += (-slots) % (LANES // GRID_W)
    row_sel = np.zeros((RPB_ROWS, slots), np.float32)
    for s in range(slots):
        if 0 <= dr_hi - s < RPB_ROWS:
            row_sel[dr_hi - s, s] = 1.0
    col = np.arange(GRID_W)
    dc = col[:, None] - col[None, :] + NA_WIN_W - 1
    col_sel = (dc[None, :, :] == np.arange(RPB_COLS)[:, None, None]).astype(np.float32)
    t = jnp.einsum('hab,as,bkq->hksq', rpb, row_sel, col_sel, precision=lax.Precision.HIGHEST)
    return t.reshape(rpb.shape[0], GRID_W, slots * GRID_W)


def _neighborhood(proj, vt, cproj, tz, pats):
    b, n, _ = proj.shape
    n_ctx = cproj.shape[1]
    col = lambda c0: pl.BlockSpec((None, n, GROUP_W), lambda g, bb: (bb, 0, c0 + g))
    ccol = lambda c0: pl.BlockSpec((None, n_ctx, GROUP_W), lambda g, bb: (bb, 0, c0 + g))
    return pl.pallas_call(
        functools.partial(_na_kernel, pats=pats),
        grid=(N_GROUPS, b),
        in_specs=[col(COL_NQ), col(COL_NK),
                  pl.BlockSpec((None, GROUP_W, n), lambda g, bb: (bb, g, 0)),
                  ccol(CCOL_NK), ccol(CCOL_NV),
                  pl.BlockSpec((HEAD_GROUP,) + tz.shape[1:], lambda g, bb: (g, 0, 0))],
        out_specs=pl.BlockSpec((None, n, GROUP_W), lambda g, bb: (bb, 0, g)),
        out_shape=jax.ShapeDtypeStruct((b, n, MIX_W), BF16),
        scratch_shapes=[pltpu.VMEM((len(pats), NA_K, HEAD_GROUP * NA_Q), F32),
                        pltpu.VMEM((GROUP_W, n_ctx), BF16),
                        pltpu.VMEM((NA_K + n_ctx, HEAD_GROUP * NA_Q), F32),
                        pltpu.VMEM((NA_K + n_ctx, HEAD_GROUP * NA_Q), F32),
                        pltpu.VMEM((1, HEAD_GROUP * NA_Q), F32),
                        pltpu.VMEM((1, HEAD_GROUP * NA_Q), F32)],
        compiler_params=_cparams("arbitrary", "arbitrary"),
        name="neighborhood",
    )(proj, proj, vt, cproj, cproj, tz)


def _stream_cast(src_hbm, dst_ref, stage_ref, sems):
    chunk = stage_ref.shape[1]
    n_chunks = src_hbm.shape[0] // chunk

    def copy(i):
        return pltpu.make_async_copy(src_hbm.at[pl.ds(i * chunk, chunk), :], stage_ref.at[i % 2],
                                     sems.at[i % 2])

    copy(0).start()
    for i in range(n_chunks):
        if i + 1 < n_chunks:
            copy(i + 1).start()
        copy(i).wait()
        dst_ref[pl.ds(i * chunk, chunk), :] = stage_ref[i % 2].astype(BF16)


def _mix_ffn_kernel(x_ref, yr_ref, yn_ref, gr_ref, gn_ref, gt1_ref, sh2_ref, sc2_ref, gt2_ref,
                    g_post_mix_ref, g_pre_ffn_ref, g_post_ffn_ref,
                    w_ret_hbm, w_na_hbm, w_o_hbm, w1_hbm, w2_hbm, o_ref,
                    w_ret_ref, w_na_ref, w_o_ref, w1_ref, w2_ref,
                    stage_narrow_ref, stage_wide_ref, sems_narrow, sems_wide):
    @pl.when((pl.program_id(0) == 0) & (pl.program_id(1) == 0))
    def _load_weights():
        _stream_cast(w_ret_hbm, w_ret_ref, stage_narrow_ref, sems_narrow)
        _stream_cast(w_na_hbm, w_na_ref, stage_narrow_ref, sems_narrow)
        _stream_cast(w_o_hbm, w_o_ref, stage_narrow_ref, sems_narrow)
        _stream_cast(w1_hbm, w1_ref, stage_wide_ref, sems_wide)
        _stream_cast(w2_hbm, w2_ref, stage_narrow_ref, sems_narrow)

    tm = x_ref.shape[0]
    d_ff = w1_ref.shape[1]
    a = jnp.dot(yr_ref[...], w_ret_ref[...], preferred_element_type=F32)
    bb = jnp.dot(yn_ref[...], w_na_ref[...], preferred_element_type=F32)
    y = (jax.nn.sigmoid(gr_ref[...].astype(F32)) * a + jax.nn.sigmoid(gn_ref[...].astype(F32)) * bb)
    y = jnp.dot(y.astype(BF16), w_o_ref[...], preferred_element_type=F32)
    x1 = x_ref[...] + gt1_ref[...] * (_rms(y) * g_post_mix_ref[...])
    h2 = ((_rms(x1) * g_pre_ffn_ref[...]) * (1.0 + sc2_ref[...]) + sh2_ref[...]).astype(BF16)
    f = jnp.zeros((tm, x_ref.shape[1]), F32)
    for c0 in range(0, d_ff, FF_CHUNK):
        u = jnp.dot(h2, w1_ref[:, c0:c0 + FF_CHUNK], preferred_element_type=F32)
        u = jnp.square(jnp.maximum(u, 0.0)).astype(BF16)
        f = f + jnp.dot(u, w2_ref[c0:c0 + FF_CHUNK, :], preferred_element_type=F32)
    o_ref[...] = x1 + gt2_ref[...] * (_rms(f) * g_post_ffn_ref[...])


def _mix_ffn(x, y_ret, y_na, proj, mod4, g_post_mix, g_pre_ffn, g_post_ffn,
             w_ret, w_na, w_o, w1, w2):
    b, n, d = x.shape
    tm = MIX_TM
    tok = lambda w: pl.BlockSpec((None, tm, w), lambda i, bb: (bb, i, 0))
    mod_spec = lambda k: pl.BlockSpec((None, None, 1, d), lambda i, bb: (bb, k, 0, 0))
    return pl.pallas_call(
        _mix_ffn_kernel,
        grid=(n // tm, b),
        in_specs=[tok(d), tok(MIX_W), tok(MIX_W),
                  pl.BlockSpec((None, tm, d), lambda i, bb: (bb, i, 0)),
                  pl.BlockSpec((None, tm, d), lambda i, bb: (bb, i, 1)),
                  mod_spec(2), mod_spec(3), mod_spec(4), mod_spec(5),
                  _resident((1, d)), _resident((1, d)), _resident((1, d))]
                 + [pl.BlockSpec(memory_space=pl.ANY)] * 5,
        out_specs=tok(d),
        out_shape=jax.ShapeDtypeStruct((b, n, d), F32),
        scratch_shapes=[pltpu.VMEM(w.shape, BF16) for w in (w_ret, w_na, w_o, w1, w2)]
                       + [pltpu.VMEM((2, MIX_STAGE_ELEMS // d, d), F32),
                          pltpu.VMEM((2, MIX_STAGE_ELEMS // w1.shape[1], w1.shape[1]), F32),
                          pltpu.SemaphoreType.DMA((2,)), pltpu.SemaphoreType.DMA((2,))],
        compiler_params=_cparams("arbitrary", "arbitrary"),
        name="mix_ffn",
    )(x, y_ret, y_na, proj, proj, mod4, mod4, mod4, mod4, g_post_mix, g_pre_ffn, g_post_ffn,
      w_ret, w_na, w_o, w1, w2)


def _rope_tables(n):
    pos = np.arange(n)
    row = (pos // GRID_W).astype(np.float64)
    colp = (pos % GRID_W).astype(np.float64)
    d_axis = HEAD_DIM // 2
    inv = ROPE_BASE ** (-np.arange(0, d_axis, 2, dtype=np.float64) / d_axis)
    ang = np.concatenate([row[:, None] * inv, colp[:, None] * inv], axis=-1)
    cos, sin = np.cos(ang), np.sin(ang)
    zero = np.zeros_like(sin)
    reps = LANES // HEAD_DIM
    cos_t = np.tile(np.concatenate([cos, cos], axis=-1), (1, reps))
    sin_lo = np.tile(np.concatenate([-sin, zero], axis=-1), (1, reps))
    sin_hi = np.tile(np.concatenate([zero, sin], axis=-1), (1, reps))
    return tuple(jnp.asarray(t, F32) for t in (cos_t, sin_lo, sin_hi))


def kernel(x, c, ctx, c_ctx, w_ada, b_ada, norm_pre_mix, norm_post_mix, norm_pre_ffn, norm_post_ffn,
           w_in, ret_decay_logit, w_ret_out, na_rpb, w_na_out, w_o, w_ff1, w_ff2):
    b, n, d = x.shape
    n_ctx = ctx.shape[1]
    assert w_ada.shape[0] == 1, "single-layer block"
    assert n % NA_Q == 0 and n % PROJ_TM == 0 and n % MIX_TM == 0 and n % RET_L == 0
    assert n_ctx % RET_L == 0 and d == D_MODEL and 7 * MIX_W + GATE_COLS == w_in.shape[2]

    pad = (-(b + 1)) % 8
    c_rows = jnp.concatenate([c, c_ctx[None, :], jnp.zeros((pad, d), F32)], axis=0)
    mod = _ada(c_rows, w_ada[0], b_ada[0])
    mod4 = mod.reshape(c_rows.shape[0], 6, 1, d)

    w_all = w_in[0].astype(BF16)
    w_ctx_bf = jnp.concatenate([w_all[:, MIX_W:3 * MIX_W], w_all[:, 5 * MIX_W:7 * MIX_W]], axis=1)
    w_lat_bf = jnp.concatenate([w_all[:, 7 * MIX_W:], w_all[:, :6 * MIX_W]], axis=1)
    w_nvt_bf = w_all[:, 6 * MIX_W:7 * MIX_W].T
    g_pre_mix = norm_pre_mix[0].reshape(1, d)

    proj, nvt = _proj_latent(x, mod4, g_pre_mix, w_lat_bf, w_nvt_bf, _rope_tables(n))
    cproj = _proj_context(ctx.reshape(b * n_ctx, d), mod4, b, g_pre_mix, w_ctx_bf)
    cproj = cproj.reshape(b, n_ctx, 4 * MIX_W)

    logit = ret_decay_logit[0].astype(F32)
    y_ret = _retention(proj, cproj, jnp.repeat(logit, HEAD_DIM, axis=1), jnp.repeat(logit, RET_L, axis=1))

    pats = _na_patterns(n // GRID_W)
    y_na = _neighborhood(proj, nvt, cproj, _na_toeplitz(na_rpb[0].astype(F32), pats), pats)

    return _mix_ffn(x, y_ret, y_na, proj, mod4,
                    norm_post_mix[0].reshape(1, d), norm_pre_ffn[0].reshape(1, d),
                    norm_post_ffn[0].reshape(1, d),
                    w_ret_out[0], w_na_out[0], w_o[0], w_ff1[0], w_ff2[0])
```

```python
import functools
import math

import jax
import jax.numpy as jnp
import numpy as np
from jax import lax
from jax.experimental import pallas as pl
from jax.experimental.pallas import tpu as pltpu

F32 = jnp.float32
BF16 = jnp.bfloat16

D_MODEL = 1024
GRID_W = 64
HEADS = 8
HEAD_DIM = 64
MIX_W = HEADS * HEAD_DIM
NA_WIN_H = 8
NA_WIN_W = 16
ROPE_BASE = 10000.0
EPS = 1e-6
NEG_INF = -1e30
K_SCALE = HEAD_DIM ** -0.5
LOG2E = math.log2(math.e)

LANES = 128
V7X_VMEM_BYTES = 64 * 2 ** 20
VMEM_LIMIT = V7X_VMEM_BYTES - 8 * 2 ** 20

HEAD_GROUP = 4
GROUP_W = HEAD_GROUP * HEAD_DIM
N_GROUPS = HEADS // HEAD_GROUP
PROJ_TM = 1024
PROJ_SPLIT = 2
PROJ_STAGE_ROWS = 256
MIX_TM = 512
MIX_STAGE_ELEMS = 512 * 1024
FF_CHUNK = 1024
RET_L = 256
NA_ROWS = 4
NA_KROWS = 12
NA_Q = NA_ROWS * GRID_W
NA_K = NA_KROWS * GRID_W
RPB_ROWS = 2 * NA_WIN_H - 1
RPB_COLS = 2 * NA_WIN_W - 1

GATE_COLS = 2 * D_MODEL
COL_RQ, COL_RK, COL_RV, COL_RG, COL_NQ, COL_NK = (GATE_COLS // GROUP_W + i * N_GROUPS for i in range(6))
PROJ_COLS = GATE_COLS + 6 * MIX_W
CCOL_RK, CCOL_RV, CCOL_NK, CCOL_NV = (i * N_GROUPS for i in range(4))

_NT = (((1,), (1,)), ((), ()))
_TN = (((0,), (0,)), ((), ()))


def _cparams(*sem):
    return pltpu.CompilerParams(dimension_semantics=sem, vmem_limit_bytes=VMEM_LIMIT)


def _resident(shape):
    nd = len(shape)
    return pl.BlockSpec(shape, lambda *_: (0,) * nd, pipeline_mode=pl.Buffered(1))


def _stream_cast(src_hbm, dst_ref, stage_ref, sems, src_col=0, dst_col=0, transpose=False):
    chunk, width = stage_ref.shape[1:]
    n_chunks = src_hbm.shape[0] // chunk

    def copy(i):
        window = src_hbm.at[pl.ds(i * chunk, chunk), pl.ds(src_col, width)]
        return pltpu.make_async_copy(window, stage_ref.at[i % 2], sems.at[i % 2])

    copy(0).start()
    for i in range(n_chunks):
        if i + 1 < n_chunks:
            copy(i + 1).start()
        copy(i).wait()
        rows = pl.ds(i * chunk, chunk)
        if transpose:
            dst_ref[pl.ds(dst_col, width), rows] = stage_ref[i % 2].T.astype(BF16)
        else:
            dst_ref[rows, pl.ds(dst_col, width)] = stage_ref[i % 2].astype(BF16)


def _rms(x):
    return x * lax.rsqrt(jnp.mean(x * x, axis=-1, keepdims=True) + EPS)


def _silu(x):
    return x * jax.nn.sigmoid(x)


def _head_masks():
    lane_head = lax.broadcasted_iota(jnp.int32, (1, GROUP_W), 1) // HEAD_DIM
    sel = [lane_head == h for h in range(HEAD_GROUP)]
    return sel, [m.astype(BF16) for m in sel]


def _ada_kernel(c_ref, w_ref, b_ref, o_ref):
    a = _silu(c_ref[...])
    o_ref[...] = jnp.dot(a, w_ref[...], preferred_element_type=F32,
                         precision=lax.Precision.HIGHEST) + b_ref[...]


def _ada(c_rows, w_ada, b_ada):
    rows, d = c_rows.shape
    cols = w_ada.shape[1]
    tn = d
    return pl.pallas_call(
        _ada_kernel,
        grid=(cols // tn,),
        in_specs=[pl.BlockSpec((rows, d), lambda j: (0, 0)),
                  pl.BlockSpec((d, tn), lambda j: (0, j)),
                  pl.BlockSpec((1, tn), lambda j: (0, j))],
        out_specs=pl.BlockSpec((rows, tn), lambda j: (0, j)),
        out_shape=jax.ShapeDtypeStruct((rows, cols), F32),
        compiler_params=_cparams("arbitrary"),
        name="ada_mod",
    )(c_rows, w_ada, b_ada.reshape(1, cols))


def _rope_slab(t, cos, sin_lo, sin_hi):
    return (t * cos + pltpu.roll(t, 3 * LANES // 4, axis=1) * sin_lo
            + pltpu.roll(t, LANES // 4, axis=1) * sin_hi)


def _modulated_norm(x_ref, sh_ref, sc_ref, g_ref):
    return ((_rms(x_ref[...]) * g_ref[...]) * (1.0 + sc_ref[...]) + sh_ref[...]).astype(BF16)


def _proj_latent_kernel(x_ref, sh_ref, sc_ref, g_ref, w_hbm, cos_ref, slo_ref, shi_ref,
                        o_ref, vt_ref, w_ref, wvt_ref, stage_ref, stage_v_ref, sems, sems_v):
    @pl.when((pl.program_id(0) == 0) & (pl.program_id(1) == 0))
    def _load_weights():
        width = stage_ref.shape[2]
        for off in range(0, GATE_COLS, width):
            _stream_cast(w_hbm, w_ref, stage_ref, sems, src_col=7 * MIX_W + off, dst_col=off)
        for off in range(0, 6 * MIX_W, width):
            _stream_cast(w_hbm, w_ref, stage_ref, sems, src_col=off, dst_col=GATE_COLS + off)
        _stream_cast(w_hbm, wvt_ref, stage_v_ref, sems_v, src_col=6 * MIX_W, transpose=True)

    rope_lo, rope_hi = COL_RQ * GROUP_W, COL_RV * GROUP_W
    col_scale = {COL_RK * GROUP_W: K_SCALE, COL_NQ * GROUP_W: K_SCALE * LOG2E}
    tm = x_ref.shape[0]
    for r0 in range(0, tm, tm // PROJ_SPLIT):
        rows = slice(r0, r0 + tm // PROJ_SPLIT)
        x = x_ref[rows, :]
        h = ((_rms(x) * g_ref[...]) * (1.0 + sc_ref[...]) + sh_ref[...]).astype(BF16)
        for c0 in range(0, PROJ_COLS, MIX_W):
            r = jnp.dot(h, w_ref[:, c0:c0 + MIX_W], preferred_element_type=F32)
            if rope_lo <= c0 < rope_hi:
                cos, slo, shi = cos_ref[rows, :], slo_ref[rows, :], shi_ref[rows, :]
                r = jnp.concatenate(
                    [_rope_slab(r[:, j:j + LANES], cos, slo, shi) for j in range(0, MIX_W, LANES)], axis=1)
            if c0 in col_scale:
                r = r * col_scale[c0]
            o_ref[rows, c0:c0 + MIX_W] = r.astype(BF16)
        vt_ref[:, rows] = lax.dot_general(wvt_ref[...], h, _NT, preferred_element_type=F32).astype(BF16)


def _proj_latent(x, mod4, g, w_in, rope_tabs):
    b, n, d = x.shape
    tm = PROJ_TM
    mod_spec = lambda k: pl.BlockSpec((None, None, 1, d), lambda i, bb: (bb, k, 0, 0))
    tab_spec = pl.BlockSpec((tm, LANES), lambda i, bb: (i, 0))
    return pl.pallas_call(
        _proj_latent_kernel,
        grid=(n // tm, b),
        in_specs=[pl.BlockSpec((None, tm, d), lambda i, bb: (bb, i, 0)),
                  mod_spec(0), mod_spec(1), _resident((1, d)), pl.BlockSpec(memory_space=pl.ANY),
                  tab_spec, tab_spec, tab_spec],
        out_specs=[pl.BlockSpec((None, tm, PROJ_COLS), lambda i, bb: (bb, i, 0)),
                   pl.BlockSpec((None, MIX_W, tm), lambda i, bb: (bb, 0, i))],
        out_shape=[jax.ShapeDtypeStruct((b, n, PROJ_COLS), BF16),
                   jax.ShapeDtypeStruct((b, MIX_W, n), BF16)],
        scratch_shapes=[pltpu.VMEM((d, PROJ_COLS), BF16), pltpu.VMEM((MIX_W, d), BF16),
                        pltpu.VMEM((2, PROJ_STAGE_ROWS, d), F32), pltpu.VMEM((2, PROJ_STAGE_ROWS, MIX_W), F32),
                        pltpu.SemaphoreType.DMA((2,)), pltpu.SemaphoreType.DMA((2,))],
        compiler_params=_cparams("arbitrary", "arbitrary"),
        name="proj_latent",
    )(x, mod4, mod4, g, w_in, *rope_tabs)


def _proj_context_kernel(x_ref, sh_ref, sc_ref, g_ref, w_hbm, o_ref, w_ref, stage_ref, sems):
    @pl.when(pl.program_id(0) == 0)
    def _load_weights():
        _stream_cast(w_hbm, w_ref, stage_ref, sems, src_col=MIX_W, dst_col=0)
        _stream_cast(w_hbm, w_ref, stage_ref, sems, src_col=5 * MIX_W, dst_col=2 * MIX_W)

    h = _modulated_norm(x_ref, sh_ref, sc_ref, g_ref)
    for c0 in range(0, o_ref.shape[-1], MIX_W):
        r = jnp.dot(h, w_ref[:, c0:c0 + MIX_W], preferred_element_type=F32)
        if c0 == CCOL_RK * GROUP_W:
            r = r * K_SCALE
        o_ref[:, c0:c0 + MIX_W] = r.astype(BF16)


def _proj_context(ctx_rows, mod4, ctx_row, g, w_in):
    rows, d = ctx_rows.shape
    cols = 4 * MIX_W
    tm = min(PROJ_TM, rows)
    mod_spec = lambda k: pl.BlockSpec((None, None, 1, d), lambda i: (ctx_row, k, 0, 0))
    return pl.pallas_call(
        _proj_context_kernel,
        grid=(rows // tm,),
        in_specs=[pl.BlockSpec((tm, d), lambda i: (i, 0)),
                  mod_spec(0), mod_spec(1), _resident((1, d)), pl.BlockSpec(memory_space=pl.ANY)],
        out_specs=pl.BlockSpec((tm, cols), lambda i: (i, 0)),
        out_shape=jax.ShapeDtypeStruct((rows, cols), BF16),
        scratch_shapes=[pltpu.VMEM((d, cols), BF16), pltpu.VMEM((2, PROJ_STAGE_ROWS, 2 * MIX_W), F32),
                        pltpu.SemaphoreType.DMA((2,))],
        compiler_params=_cparams("arbitrary"),
        name="proj_context",
    )(ctx_rows, mod4, mod4, g, w_in)


def _log_sigmoid(x):
    return jnp.minimum(x, 0.0) - jnp.log1p(jnp.exp(-jnp.abs(x)))


def _ret_kernel(q_ref, k_ref, v_ref, rg_ref, ck_ref, cv_ref, lg_lane_ref, lg_chunk_ref, o_ref,
                acc_ref, dec_ref, sf_ref, sb_ref):
    L = RET_L
    n = q_ref.shape[0]
    n_ctx = ck_ref.shape[0]
    nc, ncc = n // L, n_ctx // L

    lg = _log_sigmoid(lg_lane_ref[...])
    lgf, lgb = lg[0:1, :], lg[1:2, :]
    pos = lax.broadcasted_iota(jnp.int32, (L, 1), 0).astype(F32)
    qw_f = jnp.exp(lgf * (pos + 1.0))
    kw_f = jnp.exp(lgf * (L - 1.0 - pos))
    qw_b = jnp.exp(lgb * (L - pos))
    kw_b = jnp.exp(lgb * pos)
    gl_f = jnp.exp(lgf * float(L))
    gl_b = jnp.exp(lgb * float(L))

    lane_head = lax.broadcasted_iota(jnp.int32, (1, GROUP_W), 1) // HEAD_DIM
    row_head = lax.broadcasted_iota(jnp.int32, (GROUP_W, 1), 0) // HEAD_DIM
    block_diag = row_head == lane_head
    head_mean = jnp.where(block_diag, 1.0 / HEAD_DIM, 0.0).astype(BF16)
    _, head_mask = _head_masks()

    lgc = _log_sigmoid(lg_chunk_ref[...])
    diff = pos - lax.broadcasted_iota(jnp.int32, (1, L), 1).astype(F32)
    for h in range(HEAD_GROUP):
        f = jnp.exp(lgc[0:1, h * L:(h + 1) * L] * jnp.maximum(diff, 0.0))
        bk = jnp.exp(lgc[1:2, h * L:(h + 1) * L] * jnp.maximum(-diff, 0.0))
        dec_ref[h * L:(h + 1) * L, :] = jnp.where(diff >= 0.0, f, bk)

    def state_update(s_ref, k, v, kw, gl):
        kd = (k.astype(F32) * kw).astype(BF16)
        contrib = lax.dot_general(kd, v, _TN, preferred_element_type=F32)
        s_ref[...] = s_ref[...] * gl + jnp.where(block_diag, contrib, 0.0)

    sf_ref[...] = jnp.zeros_like(sf_ref)
    sb_ref[...] = jnp.zeros_like(sb_ref)

    def ctx_fwd(c, carry):
        rows = pl.ds(pl.multiple_of(c * L, L), L)
        state_update(sf_ref, ck_ref[rows, :], cv_ref[rows, :], kw_f, gl_f)
        return carry

    def ctx_bwd(i, carry):
        rows = pl.ds(pl.multiple_of((ncc - 1 - i) * L, L), L)
        state_update(sb_ref, ck_ref[rows, :], cv_ref[rows, :], kw_b, gl_b)
        return carry

    lax.fori_loop(0, ncc, ctx_fwd, 0)
    lax.fori_loop(0, ncc, ctx_bwd, 0)

    def fwd_chunk(c, carry):
        rows = pl.ds(pl.multiple_of(c * L, L), L)
        q, k, v = q_ref[rows, :], k_ref[rows, :], v_ref[rows, :]
        q_heads = jnp.concatenate([q * head_mask[h] for h in range(HEAD_GROUP)], axis=0)
        s = lax.dot_general(q_heads, k, _NT, preferred_element_type=F32)
        s = (s * dec_ref[...]).astype(BF16)
        s_cat = jnp.concatenate([s[h * L:(h + 1) * L, :] for h in range(HEAD_GROUP)], axis=1)
        v_heads = jnp.concatenate([v * head_mask[h] for h in range(HEAD_GROUP)], axis=0)
        o = jnp.dot(s_cat, v_heads, preferred_element_type=F32)
        o = o + jnp.dot(q, sf_ref[...].astype(BF16), preferred_element_type=F32) * qw_f
        acc_ref[rows, :] = o
        state_update(sf_ref, k, v, kw_f, gl_f)
        return carry

    lax.fori_loop(0, nc, fwd_chunk, 0, unroll=16)

    def bwd_chunk(i, carry):
        rows = pl.ds(pl.multiple_of((nc - 1 - i) * L, L), L)
        q, k, v = q_ref[rows, :], k_ref[rows, :], v_ref[rows, :]
        o = acc_ref[rows, :] + jnp.dot(q, sb_ref[...].astype(BF16), preferred_element_type=F32) * qw_b
        sq = o * o
        sq_hi = sq.astype(BF16)
        sq_lo = (sq - sq_hi.astype(F32)).astype(BF16)
        ms = (jnp.dot(sq_hi, head_mean, preferred_element_type=F32)
              + jnp.dot(sq_lo, head_mean, preferred_element_type=F32))
        y = o * lax.rsqrt(ms + EPS) * _silu(rg_ref[rows, :].astype(F32))
        o_ref[rows, :] = y.astype(BF16)
        state_update(sb_ref, k, v, kw_b, gl_b)
        return carry

    lax.fori_loop(0, nc, bwd_chunk, 0, unroll=16)


def _retention(proj, cproj, lg_lane, lg_chunk):
    b, n, _ = proj.shape
    n_ctx = cproj.shape[1]
    L = RET_L
    col = lambda c0: pl.BlockSpec((None, n, GROUP_W), lambda bb, g: (bb, 0, c0 + g))
    ccol = lambda c0: pl.BlockSpec((None, n_ctx, GROUP_W), lambda bb, g: (bb, 0, c0 + g))
    return pl.pallas_call(
        _ret_kernel,
        grid=(b, N_GROUPS),
        in_specs=[col(COL_RQ), col(COL_RK), col(COL_RV), col(COL_RG), ccol(CCOL_RK), ccol(CCOL_RV),
                  pl.BlockSpec((2, GROUP_W), lambda bb, g: (0, g)),
                  pl.BlockSpec((2, HEAD_GROUP * L), lambda bb, g: (0, g))],
        out_specs=pl.BlockSpec((None, n, GROUP_W), lambda bb, g: (bb, 0, g)),
        out_shape=jax.ShapeDtypeStruct((b, n, MIX_W), BF16),
        scratch_shapes=[pltpu.VMEM((n, GROUP_W), F32),
                        pltpu.VMEM((HEAD_GROUP * L, L), F32),
                        pltpu.VMEM((GROUP_W, GROUP_W), F32),
                        pltpu.VMEM((GROUP_W, GROUP_W), F32)],
        compiler_params=_cparams("arbitrary", "arbitrary"),
        name="retention",
    )(proj, proj, proj, proj, cproj, cproj, lg_lane, lg_chunk)


def _na_patterns(rows):
    nblk = rows // NA_ROWS
    assert rows % NA_ROWS == 0 and nblk >= 3 and rows >= NA_KROWS
    pats = []
    for qb in (0, 1, nblk - 1):
        kb = _na_key_start(qb, rows)
        qr = qb * NA_ROWS + np.arange(NA_ROWS)
        r0 = np.clip(qr - NA_WIN_H // 2, 0, rows - NA_WIN_H)
        assert np.all(r0 >= kb) and np.all(r0 + NA_WIN_H <= kb + NA_KROWS)
        pats.append((tuple(int(v) for v in r0 - kb), int(kb - qb * NA_ROWS + NA_WIN_H - 1)))
    for qb in range(1, nblk - 1):
        kb = _na_key_start(qb, rows)
        qr = qb * NA_ROWS + np.arange(NA_ROWS)
        r0 = np.clip(qr - NA_WIN_H // 2, 0, rows - NA_WIN_H)
        assert (tuple(int(v) for v in r0 - kb), int(kb - qb * NA_ROWS + NA_WIN_H - 1)) == pats[1]
    return pats


def _na_key_start(qb, rows):
    assert (NA_WIN_H // 2) % NA_ROWS == 0 and NA_KROWS % NA_ROWS == 0
    return min(max(qb * NA_ROWS - NA_WIN_H // 2, 0), rows - NA_KROWS)


def _na_slot_range(pats):
    lo = min(off - (NA_ROWS - 1) for _, off in pats)
    hi = max(off + NA_KROWS - 1 for _, off in pats)
    return lo, hi


def _na_kernel(q_ref, k_ref, vt_ref, ck_ref, cv_ref, tz_ref, o_ref,
               bias_ref, cvt_ref, s0_ref, s1_ref, m0_ref, m1_ref, *, pats):
    n = q_ref.shape[0]
    n_ctx = ck_ref.shape[0]
    rows = n // GRID_W
    nblk = rows // NA_ROWS
    _, head_mask = _head_masks()
    _, dr_hi = _na_slot_range(pats)
    n_keys = NA_K + n_ctx

    @pl.when(pl.program_id(1) == 0)
    def _build_bias():
        key = lax.broadcasted_iota(jnp.int32, (NA_K, NA_Q), 0)
        qry = lax.broadcasted_iota(jnp.int32, (NA_K, NA_Q), 1)
        j, kc = key // GRID_W, key % GRID_W
        i, qc = qry // GRID_W, qry % GRID_W
        c0 = jnp.clip(qc - NA_WIN_W // 2, 0, GRID_W - NA_WIN_W)
        col_ok = (kc >= c0) & (kc < c0 + NA_WIN_W)
        for p, (r0, off) in enumerate(pats):
            lo = jnp.zeros_like(i)
            for ii in range(NA_ROWS):
                lo = jnp.where(i == ii, r0[ii], lo)
            ok = col_ok & (j >= lo) & (j < lo + NA_WIN_H)
            for h in range(HEAD_GROUP):
                t = tz_ref[h]
                for jj in range(NA_KROWS):
                    slot = dr_hi - (jj + off)
                    tile = t[:, slot * GRID_W:slot * GRID_W + NA_Q] * LOG2E
                    rs = slice(jj * GRID_W, (jj + 1) * GRID_W)
                    bias_ref[p, rs, h * NA_Q:(h + 1) * NA_Q] = jnp.where(ok[rs, :], tile, NEG_INF)

    cvt_ref[...] = cv_ref[...].astype(F32).T.astype(BF16)
    ones_rows = jnp.ones((2 * 8, n_keys), BF16)

    def block_rows(qb):
        kb = jnp.clip(qb * NA_ROWS - NA_WIN_H // 2, 0, rows - NA_KROWS)
        q_rows = pl.ds(pl.multiple_of(qb * NA_Q, NA_Q), NA_Q)
        k_tok = pl.ds(pl.multiple_of(kb * GRID_W, NA_Q), NA_K)
        return q_rows, k_tok

    def scores(qb, s_ref, m_ref):
        q_rows, k_tok = block_rows(qb)
        pat = jnp.where(qb == 0, 0, jnp.where(qb == nblk - 1, 2, 1))
        q = q_ref[q_rows, :]
        q_heads = jnp.concatenate([q * head_mask[h] for h in range(HEAD_GROUP)], axis=0)
        s_ref[0:NA_K, :] = (lax.dot_general(k_ref[k_tok, :], q_heads, _NT, preferred_element_type=F32)
                            + bias_ref[pat])
        s_ref[NA_K:n_keys, :] = lax.dot_general(ck_ref[...], q_heads, _NT, preferred_element_type=F32)
        m_ref[...] = jnp.max(s_ref[...], axis=0, keepdims=True)

    def attend(qb, s_ref, m_ref):
        q_rows, k_tok = block_rows(qb)
        p = jnp.exp2(s_ref[...] - m_ref[...]).astype(BF16)
        outs = []
        for h in range(HEAD_GROUP):
            hs = slice(h * HEAD_DIM, (h + 1) * HEAD_DIM)
            v_aug = jnp.concatenate(
                [jnp.concatenate([vt_ref[hs, k_tok], cvt_ref[hs, :]], axis=1), ones_rows], axis=0)
            oh = jnp.dot(v_aug, p[:, h * NA_Q:(h + 1) * NA_Q], preferred_element_type=F32)
            outs.append(oh[0:HEAD_DIM, :] / oh[HEAD_DIM:HEAD_DIM + 1, :])
        o_ref[q_rows, :] = jnp.concatenate(outs, axis=0).T.astype(BF16)

    assert nblk % 2 == 0
    scores(0, s0_ref, m0_ref)

    def pair(t, carry):
        qb = 2 * t
        scores(qb + 1, s1_ref, m1_ref)
        attend(qb, s0_ref, m0_ref)
        scores(jnp.minimum(qb + 2, nblk - 1), s0_ref, m0_ref)
        attend(qb + 1, s1_ref, m1_ref)
        return carry

    lax.fori_loop(0, nblk // 2, pair, 0, unroll=4)


def _na_toeplitz(rpb, pats):
    dr_lo, dr_hi = _na_slot_range(pats)
    slots = dr_hi - dr_lo + 1
    slots += (-slots) % (LANES // GRID_W)
    row_sel = np.zeros((RPB_ROWS, slots), np.float32)
    for s in range(slots):
        if 0 <= dr_hi - s < RPB_ROWS:
            row_sel[dr_hi - s, s] = 1.0
    col = np.arange(GRID_W)
    dc = col[:, None] - col[None, :] + NA_WIN_W - 1
    col_sel = (dc[None, :, :] == np.arange(RPB_COLS)[:, None, None]).astype(np.float32)
    t = jnp.einsum('hab,as,bkq->hksq', rpb, row_sel, col_sel, precision=lax.Precision.HIGHEST)
    return t.reshape(rpb.shape[0], GRID_W, slots * GRID_W)


def _neighborhood(proj, vt, cproj, tz, pats):
    b, n, _ = proj.shape
    n_ctx = cproj.shape[1]
    col = lambda c0: pl.BlockSpec((None, n, GROUP_W), lambda g, bb: (bb, 0, c0 + g))
    ccol = lambda c0: pl.BlockSpec((None, n_ctx, GROUP_W), lambda g, bb: (bb, 0, c0 + g))
    return pl.pallas_call(
        functools.partial(_na_kernel, pats=pats),
        grid=(N_GROUPS, b),
        in_specs=[col(COL_NQ), col(COL_NK),
                  pl.BlockSpec((None, GROUP_W, n), lambda g, bb: (bb, g, 0)),
                  ccol(CCOL_NK), ccol(CCOL_NV),
                  pl.BlockSpec((HEAD_GROUP,) + tz.shape[1:], lambda g, bb: (g, 0, 0))],
        out_specs=pl.BlockSpec((None, n, GROUP_W), lambda g, bb: (bb, 0, g)),
        out_shape=jax.ShapeDtypeStruct((b, n, MIX_W), BF16),
        scratch_shapes=[pltpu.VMEM((len(pats), NA_K, HEAD_GROUP * NA_Q), F32),
                        pltpu.VMEM((GROUP_W, n_ctx), BF16),
                        pltpu.VMEM((NA_K + n_ctx, HEAD_GROUP * NA_Q), F32),
                        pltpu.VMEM((NA_K + n_ctx, HEAD_GROUP * NA_Q), F32),
                        pltpu.VMEM((1, HEAD_GROUP * NA_Q), F32),
                        pltpu.VMEM((1, HEAD_GROUP * NA_Q), F32)],
        compiler_params=_cparams("arbitrary", "arbitrary"),
        name="neighborhood",
    )(proj, proj, vt, cproj, cproj, tz)


def _mix_ffn_kernel(x_ref, yr_ref, yn_ref, gr_ref, gn_ref, gt1_ref, sh2_ref, sc2_ref, gt2_ref,
                    g_post_mix_ref, g_pre_ffn_ref, g_post_ffn_ref,
                    w_ret_hbm, w_na_hbm, w_o_hbm, w1_hbm, w2_hbm, o_ref,
                    w_ret_ref, w_na_ref, w_o_ref, w1_ref, w2_ref,
                    stage_narrow_ref, stage_wide_ref, sems_narrow, sems_wide):
    @pl.when((pl.program_id(0) == 0) & (pl.program_id(1) == 0))
    def _load_weights():
        _stream_cast(w_ret_hbm, w_ret_ref, stage_narrow_ref, sems_narrow)
        _stream_cast(w_na_hbm, w_na_ref, stage_narrow_ref, sems_narrow)
        _stream_cast(w_o_hbm, w_o_ref, stage_narrow_ref, sems_narrow)
        _stream_cast(w1_hbm, w1_ref, stage_wide_ref, sems_wide)
        _stream_cast(w2_hbm, w2_ref, stage_narrow_ref, sems_narrow)

    tm = x_ref.shape[0]
    d_ff = w1_ref.shape[1]
    a = jnp.dot(yr_ref[...], w_ret_ref[...], preferred_element_type=F32)
    bb = jnp.dot(yn_ref[...], w_na_ref[...], preferred_element_type=F32)
    y = (jax.nn.sigmoid(gr_ref[...].astype(F32)) * a + jax.nn.sigmoid(gn_ref[...].astype(F32)) * bb)
    y = jnp.dot(y.astype(BF16), w_o_ref[...], preferred_element_type=F32)
    x1 = x_ref[...] + gt1_ref[...] * (_rms(y) * g_post_mix_ref[...])
    h2 = ((_rms(x1) * g_pre_ffn_ref[...]) * (1.0 + sc2_ref[...]) + sh2_ref[...]).astype(BF16)
    f = jnp.zeros((tm, x_ref.shape[1]), F32)
    for c0 in range(0, d_ff, FF_CHUNK):
        u = jnp.dot(h2, w1_ref[:, c0:c0 + FF_CHUNK], preferred_element_type=F32)
        u = jnp.square(jnp.maximum(u, 0.0)).astype(BF16)
        f = f + jnp.dot(u, w2_ref[c0:c0 + FF_CHUNK, :], preferred_element_type=F32)
    o_ref[...] = x1 + gt2_ref[...] * (_rms(f) * g_post_ffn_ref[...])


def _mix_ffn(x, y_ret, y_na, proj, mod4, g_post_mix, g_pre_ffn, g_post_ffn,
             w_ret, w_na, w_o, w1, w2):
    b, n, d = x.shape
    tm = MIX_TM
    tok = lambda w: pl.BlockSpec((None, tm, w), lambda i, bb: (bb, i, 0))
    mod_spec = lambda k: pl.BlockSpec((None, None, 1, d), lambda i, bb: (bb, k, 0, 0))
    return pl.pallas_call(
        _mix_ffn_kernel,
        grid=(n // tm, b),
        in_specs=[tok(d), tok(MIX_W), tok(MIX_W),
                  pl.BlockSpec((None, tm, d), lambda i, bb: (bb, i, 0)),
                  pl.BlockSpec((None, tm, d), lambda i, bb: (bb, i, 1)),
                  mod_spec(2), mod_spec(3), mod_spec(4), mod_spec(5),
                  _resident((1, d)), _resident((1, d)), _resident((1, d))]
                 + [pl.BlockSpec(memory_space=pl.ANY)] * 5,
        out_specs=tok(d),
        out_shape=jax.ShapeDtypeStruct((b, n, d), F32),
        scratch_shapes=[pltpu.VMEM(w.shape, BF16) for w in (w_ret, w_na, w_o, w1, w2)]
                       + [pltpu.VMEM((2, MIX_STAGE_ELEMS // d, d), F32),
                          pltpu.VMEM((2, MIX_STAGE_ELEMS // w1.shape[1], w1.shape[1]), F32),
                          pltpu.SemaphoreType.DMA((2,)), pltpu.SemaphoreType.DMA((2,))],
        compiler_params=_cparams("arbitrary", "arbitrary"),
        name="mix_ffn",
    )(x, y_ret, y_na, proj, proj, mod4, mod4, mod4, mod4, g_post_mix, g_pre_ffn, g_post_ffn,
      w_ret, w_na, w_o, w1, w2)


def _rope_tables(n):
    pos = np.arange(n)
    row = (pos // GRID_W).astype(np.float64)
    colp = (pos % GRID_W).astype(np.float64)
    d_axis = HEAD_DIM // 2
    inv = ROPE_BASE ** (-np.arange(0, d_axis, 2, dtype=np.float64) / d_axis)
    ang = np.concatenate([row[:, None] * inv, colp[:, None] * inv], axis=-1)
    cos, sin = np.cos(ang), np.sin(ang)
    zero = np.zeros_like(sin)
    reps = LANES // HEAD_DIM
    cos_t = np.tile(np.concatenate([cos, cos], axis=-1), (1, reps))
    sin_lo = np.tile(np.concatenate([-sin, zero], axis=-1), (1, reps))
    sin_hi = np.tile(np.concatenate([zero, sin], axis=-1), (1, reps))
    return tuple(jnp.asarray(t, F32) for t in (cos_t, sin_lo, sin_hi))


def kernel(x, c, ctx, c_ctx, w_ada, b_ada, norm_pre_mix, norm_post_mix, norm_pre_ffn, norm_post_ffn,
           w_in, ret_decay_logit, w_ret_out, na_rpb, w_na_out, w_o, w_ff1, w_ff2):
    b, n, d = x.shape
    n_ctx = ctx.shape[1]
    assert w_ada.shape[0] == 1, "single-layer block"
    assert n % NA_Q == 0 and n % PROJ_TM == 0 and n % MIX_TM == 0 and n % RET_L == 0
    assert n_ctx % RET_L == 0 and d == D_MODEL and 7 * MIX_W + GATE_COLS == w_in.shape[2]

    pad = (-(b + 1)) % 8
    c_rows = jnp.concatenate([c, c_ctx[None, :], jnp.zeros((pad, d), F32)], axis=0)
    mod = _ada(c_rows, w_ada[0], b_ada[0])
    mod4 = mod.reshape(c_rows.shape[0], 6, 1, d)

    g_pre_mix = norm_pre_mix[0].reshape(1, d)
    proj, nvt = _proj_latent(x, mod4, g_pre_mix, w_in[0], _rope_tables(n))
    cproj = _proj_context(ctx.reshape(b * n_ctx, d), mod4, b, g_pre_mix, w_in[0])
    cproj = cproj.reshape(b, n_ctx, 4 * MIX_W)

    logit = ret_decay_logit[0].astype(F32)
    y_ret = _retention(proj, cproj, jnp.repeat(logit, HEAD_DIM, axis=1), jnp.repeat(logit, RET_L, axis=1))

    pats = _na_patterns(n // GRID_W)
    y_na = _neighborhood(proj, nvt, cproj, _na_toeplitz(na_rpb[0].astype(F32), pats), pats)

    return _mix_ffn(x, y_ret, y_na, proj, mod4,
                    norm_post_mix[0].reshape(1, d), norm_pre_ffn[0].reshape(1, d),
                    norm_post_ffn[0].reshape(1, d),
                    w_ret_out[0], w_na_out[0], w_o[0], w_ff1[0], w_ff2[0])
```

```python
import functools
import math

import jax
import jax.numpy as jnp
import numpy as np
from jax import lax
from jax.experimental import pallas as pl
from jax.experimental.pallas import tpu as pltpu

F32 = jnp.float32
BF16 = jnp.bfloat16

D_MODEL = 1024
GRID_W = 64
HEADS = 8
HEAD_DIM = 64
MIX_W = HEADS * HEAD_DIM
NA_WIN_H = 8
NA_WIN_W = 16
ROPE_BASE = 10000.0
EPS = 1e-6
NEG_INF = -1e30
K_SCALE = HEAD_DIM ** -0.5
LOG2E = math.log2(math.e)

LANES = 128
V7X_VMEM_BYTES = 64 * 2 ** 20
VMEM_LIMIT = V7X_VMEM_BYTES - 8 * 2 ** 20

HEAD_GROUP = 4
GROUP_W = HEAD_GROUP * HEAD_DIM
N_GROUPS = HEADS // HEAD_GROUP
PROJ_TM = 1024
PROJ_SPLIT = 2
PROJ_STAGE_ROWS = 256
STAGE_SLOTS = 8
MIX_TM = 512
MIX_STAGE_ROWS = 256
FF_CHUNK = 1024
RET_L = 256
NA_ROWS = 4
NA_KROWS = 12
NA_Q = NA_ROWS * GRID_W
NA_K = NA_KROWS * GRID_W
RPB_ROWS = 2 * NA_WIN_H - 1
RPB_COLS = 2 * NA_WIN_W - 1

GATE_COLS = 2 * D_MODEL
COL_RQ, COL_RK, COL_RV, COL_RG, COL_NQ, COL_NK = (GATE_COLS // GROUP_W + i * N_GROUPS for i in range(6))
PROJ_COLS = GATE_COLS + 6 * MIX_W
CCOL_RK, CCOL_RV, CCOL_NK, CCOL_NV = (i * N_GROUPS for i in range(4))

_NT = (((1,), (1,)), ((), ()))
_TN = (((0,), (0,)), ((), ()))


def _cparams(*sem):
    return pltpu.CompilerParams(dimension_semantics=sem, vmem_limit_bytes=VMEM_LIMIT)


def _resident(shape):
    nd = len(shape)
    return pl.BlockSpec(shape, lambda *_: (0,) * nd, pipeline_mode=pl.Buffered(1))


def _stream_cast(tasks, stage_ref, sems):
    slots, chunk, width = stage_ref.shape
    jobs = [task + (r0,) for task in tasks for r0 in range(0, task[0].shape[0], chunk)]

    def copy(j):
        src, src_col, _, _, _, r0 = jobs[j]
        window = src.at[pl.ds(r0, chunk), pl.ds(src_col, width)]
        return pltpu.make_async_copy(window, stage_ref.at[j % slots], sems.at[j % slots])

    for j in range(min(slots - 1, len(jobs))):
        copy(j).start()
    for j in range(len(jobs)):
        if j + slots - 1 < len(jobs):
            copy(j + slots - 1).start()
        copy(j).wait()
        _, _, dst, dst_col, transpose, r0 = jobs[j]
        if transpose:
            dst[pl.ds(dst_col, width), pl.ds(r0, chunk)] = stage_ref[j % slots].T.astype(BF16)
        else:
            dst[pl.ds(r0, chunk), pl.ds(dst_col, width)] = stage_ref[j % slots].astype(BF16)


def _rms(x):
    return x * lax.rsqrt(jnp.mean(x * x, axis=-1, keepdims=True) + EPS)


def _silu(x):
    return x * jax.nn.sigmoid(x)


def _head_masks():
    lane_head = lax.broadcasted_iota(jnp.int32, (1, GROUP_W), 1) // HEAD_DIM
    sel = [lane_head == h for h in range(HEAD_GROUP)]
    return sel, [m.astype(BF16) for m in sel]


def _ada_kernel(c_ref, w_ref, b_ref, o_ref):
    a = _silu(c_ref[...])
    o_ref[...] = jnp.dot(a, w_ref[...], preferred_element_type=F32,
                         precision=lax.Precision.HIGHEST) + b_ref[...]


def _ada(c_rows, w_ada, b_ada):
    rows, d = c_rows.shape
    cols = w_ada.shape[1]
    tn = d
    return pl.pallas_call(
        _ada_kernel,
        grid=(cols // tn,),
        in_specs=[pl.BlockSpec((rows, d), lambda j: (0, 0)),
                  pl.BlockSpec((d, tn), lambda j: (0, j)),
                  pl.BlockSpec((1, tn), lambda j: (0, j))],
        out_specs=pl.BlockSpec((rows, tn), lambda j: (0, j)),
        out_shape=jax.ShapeDtypeStruct((rows, cols), F32),
        compiler_params=_cparams("arbitrary"),
        name="ada_mod",
    )(c_rows, w_ada, b_ada.reshape(1, cols))


def _rope_slab(t, cos, sin_lo, sin_hi):
    return (t * cos + pltpu.roll(t, 3 * LANES // 4, axis=1) * sin_lo
            + pltpu.roll(t, LANES // 4, axis=1) * sin_hi)


def _modulated_norm(x_ref, sh_ref, sc_ref, g_ref):
    return ((_rms(x_ref[...]) * g_ref[...]) * (1.0 + sc_ref[...]) + sh_ref[...]).astype(BF16)


def _proj_latent_kernel(x_ref, sh_ref, sc_ref, g_ref, w_hbm, cos_ref, slo_ref, shi_ref,
                        o_ref, vt_ref, w_ref, wvt_ref, stage_ref, sems):
    @pl.when((pl.program_id(0) == 0) & (pl.program_id(1) == 0))
    def _load_weights():
        width = stage_ref.shape[2]
        tasks = [(w_hbm, 7 * MIX_W + off, w_ref, off, False) for off in range(0, GATE_COLS, width)]
        tasks += [(w_hbm, off, w_ref, GATE_COLS + off, False) for off in range(0, 6 * MIX_W, width)]
        tasks += [(w_hbm, 6 * MIX_W + off, wvt_ref, off, True) for off in range(0, MIX_W, width)]
        _stream_cast(tasks, stage_ref, sems)

    rope_lo, rope_hi = COL_RQ * GROUP_W, COL_RV * GROUP_W
    col_scale = {COL_RK * GROUP_W: K_SCALE, COL_NQ * GROUP_W: K_SCALE * LOG2E}
    tm = x_ref.shape[0]
    for r0 in range(0, tm, tm // PROJ_SPLIT):
        rows = slice(r0, r0 + tm // PROJ_SPLIT)
        x = x_ref[rows, :]
        h = ((_rms(x) * g_ref[...]) * (1.0 + sc_ref[...]) + sh_ref[...]).astype(BF16)
        for c0 in range(0, PROJ_COLS, MIX_W):
            r = jnp.dot(h, w_ref[:, c0:c0 + MIX_W], preferred_element_type=F32)
            if rope_lo <= c0 < rope_hi:
                cos, slo, shi = cos_ref[rows, :], slo_ref[rows, :], shi_ref[rows, :]
                r = jnp.concatenate(
                    [_rope_slab(r[:, j:j + LANES], cos, slo, shi) for j in range(0, MIX_W, LANES)], axis=1)
            if c0 in col_scale:
                r = r * col_scale[c0]
            o_ref[rows, c0:c0 + MIX_W] = r.astype(BF16)
        vt_ref[:, rows] = lax.dot_general(wvt_ref[...], h, _NT, preferred_element_type=F32).astype(BF16)


def _proj_latent(x, mod4, g, w_in, rope_tabs):
    b, n, d = x.shape
    tm = PROJ_TM
    mod_spec = lambda k: pl.BlockSpec((None, None, 1, d), lambda i, bb: (bb, k, 0, 0))
    tab_spec = pl.BlockSpec((tm, LANES), lambda i, bb: (i, 0))
    return pl.pallas_call(
        _proj_latent_kernel,
        grid=(n // tm, b),
        in_specs=[pl.BlockSpec((None, tm, d), lambda i, bb: (bb, i, 0)),
                  mod_spec(0), mod_spec(1), _resident((1, d)), pl.BlockSpec(memory_space=pl.ANY),
                  tab_spec, tab_spec, tab_spec],
        out_specs=[pl.BlockSpec((None, tm, PROJ_COLS), lambda i, bb: (bb, i, 0)),
                   pl.BlockSpec((None, MIX_W, tm), lambda i, bb: (bb, 0, i))],
        out_shape=[jax.ShapeDtypeStruct((b, n, PROJ_COLS), BF16),
                   jax.ShapeDtypeStruct((b, MIX_W, n), BF16)],
        scratch_shapes=[pltpu.VMEM((d, PROJ_COLS), BF16), pltpu.VMEM((MIX_W, d), BF16),
                        pltpu.VMEM((STAGE_SLOTS, PROJ_STAGE_ROWS, MIX_W), F32),
                        pltpu.SemaphoreType.DMA((STAGE_SLOTS,))],
        compiler_params=_cparams("arbitrary", "arbitrary"),
        name="proj_latent",
    )(x, mod4, mod4, g, w_in, *rope_tabs)


def _proj_context_kernel(x_ref, sh_ref, sc_ref, g_ref, w_hbm, o_ref, w_ref, stage_ref, sems):
    @pl.when(pl.program_id(0) == 0)
    def _load_weights():
        width = stage_ref.shape[2]
        tasks = [(w_hbm, MIX_W + off, w_ref, off, False) for off in range(0, 2 * MIX_W, width)]
        tasks += [(w_hbm, 5 * MIX_W + off, w_ref, 2 * MIX_W + off, False) for off in range(0, 2 * MIX_W, width)]
        _stream_cast(tasks, stage_ref, sems)

    h = _modulated_norm(x_ref, sh_ref, sc_ref, g_ref)
    for c0 in range(0, o_ref.shape[-1], MIX_W):
        r = jnp.dot(h, w_ref[:, c0:c0 + MIX_W], preferred_element_type=F32)
        if c0 == CCOL_RK * GROUP_W:
            r = r * K_SCALE
        o_ref[:, c0:c0 + MIX_W] = r.astype(BF16)


def _proj_context(ctx_rows, mod4, ctx_row, g, w_in):
    rows, d = ctx_rows.shape
    cols = 4 * MIX_W
    tm = min(PROJ_TM, rows)
    mod_spec = lambda k: pl.BlockSpec((None, None, 1, d), lambda i: (ctx_row, k, 0, 0))
    return pl.pallas_call(
        _proj_context_kernel,
        grid=(rows // tm,),
        in_specs=[pl.BlockSpec((tm, d), lambda i: (i, 0)),
                  mod_spec(0), mod_spec(1), _resident((1, d)), pl.BlockSpec(memory_space=pl.ANY)],
        out_specs=pl.BlockSpec((tm, cols), lambda i: (i, 0)),
        out_shape=jax.ShapeDtypeStruct((rows, cols), BF16),
        scratch_shapes=[pltpu.VMEM((d, cols), BF16), pltpu.VMEM((STAGE_SLOTS, PROJ_STAGE_ROWS, MIX_W), F32),
                        pltpu.SemaphoreType.DMA((STAGE_SLOTS,))],
        compiler_params=_cparams("arbitrary"),
        name="proj_context",
    )(ctx_rows, mod4, mod4, g, w_in)


def _log_sigmoid(x):
    return jnp.minimum(x, 0.0) - jnp.log1p(jnp.exp(-jnp.abs(x)))


def _ret_kernel(q_ref, k_ref, v_ref, rg_ref, ck_ref, cv_ref, lg_lane_ref, lg_chunk_ref, o_ref,
                acc_ref, dec_ref, sf_ref, sb_ref):
    L = RET_L
    n = q_ref.shape[0]
    n_ctx = ck_ref.shape[0]
    nc, ncc = n // L, n_ctx // L

    lg = _log_sigmoid(lg_lane_ref[...])
    lgf, lgb = lg[0:1, :], lg[1:2, :]
    pos = lax.broadcasted_iota(jnp.int32, (L, 1), 0).astype(F32)
    qw_f = jnp.exp(lgf * (pos + 1.0))
    kw_f = jnp.exp(lgf * (L - 1.0 - pos))
    qw_b = jnp.exp(lgb * (L - pos))
    kw_b = jnp.exp(lgb * pos)
    gl_f = jnp.exp(lgf * float(L))
    gl_b = jnp.exp(lgb * float(L))

    lane_head = lax.broadcasted_iota(jnp.int32, (1, GROUP_W), 1) // HEAD_DIM
    row_head = lax.broadcasted_iota(jnp.int32, (GROUP_W, 1), 0) // HEAD_DIM
    block_diag = row_head == lane_head
    head_mean = jnp.where(block_diag, 1.0 / HEAD_DIM, 0.0).astype(BF16)
    _, head_mask = _head_masks()

    lgc = _log_sigmoid(lg_chunk_ref[...])
    diff = pos - lax.broadcasted_iota(jnp.int32, (1, L), 1).astype(F32)
    for h in range(HEAD_GROUP):
        f = jnp.exp(lgc[0:1, h * L:(h + 1) * L] * jnp.maximum(diff, 0.0))
        bk = jnp.exp(lgc[1:2, h * L:(h + 1) * L] * jnp.maximum(-diff, 0.0))
        dec_ref[h * L:(h + 1) * L, :] = jnp.where(diff >= 0.0, f, bk)

    def state_update(s_ref, k, v, kw, gl):
        kd = (k.astype(F32) * kw).astype(BF16)
        contrib = lax.dot_general(kd, v, _TN, preferred_element_type=F32)
        s_ref[...] = s_ref[...] * gl + jnp.where(block_diag, contrib, 0.0)

    sf_ref[...] = jnp.zeros_like(sf_ref)
    sb_ref[...] = jnp.zeros_like(sb_ref)

    def ctx_fwd(c, carry):
        rows = pl.ds(pl.multiple_of(c * L, L), L)
        state_update(sf_ref, ck_ref[rows, :], cv_ref[rows, :], kw_f, gl_f)
        return carry

    def ctx_bwd(i, carry):
        rows = pl.ds(pl.multiple_of((ncc - 1 - i) * L, L), L)
        state_update(sb_ref, ck_ref[rows, :], cv_ref[rows, :], kw_b, gl_b)
        return carry

    lax.fori_loop(0, ncc, ctx_fwd, 0)
    lax.fori_loop(0, ncc, ctx_bwd, 0)

    def fwd_chunk(c, carry):
        rows = pl.ds(pl.multiple_of(c * L, L), L)
        q, k, v = q_ref[rows, :], k_ref[rows, :], v_ref[rows, :]
        q_heads = jnp.concatenate([q * head_mask[h] for h in range(HEAD_GROUP)], axis=0)
        s = lax.dot_general(q_heads, k, _NT, preferred_element_type=F32)
        s = (s * dec_ref[...]).astype(BF16)
        s_cat = jnp.concatenate([s[h * L:(h + 1) * L, :] for h in range(HEAD_GROUP)], axis=1)
        v_heads = jnp.concatenate([v * head_mask[h] for h in range(HEAD_GROUP)], axis=0)
        o = jnp.dot(s_cat, v_heads, preferred_element_type=F32)
        o = o + jnp.dot(q, sf_ref[...].astype(BF16), preferred_element_type=F32) * qw_f
        acc_ref[rows, :] = o
        state_update(sf_ref, k, v, kw_f, gl_f)
        return carry

    lax.fori_loop(0, nc, fwd_chunk, 0, unroll=16)

    def bwd_chunk(i, carry):
        rows = pl.ds(pl.multiple_of((nc - 1 - i) * L, L), L)
        q, k, v = q_ref[rows, :], k_ref[rows, :], v_ref[rows, :]
        o = acc_ref[rows, :] + jnp.dot(q, sb_ref[...].astype(BF16), preferred_element_type=F32) * qw_b
        sq = o * o
        sq_hi = sq.astype(BF16)
        sq_lo = (sq - sq_hi.astype(F32)).astype(BF16)
        ms = (jnp.dot(sq_hi, head_mean, preferred_element_type=F32)
              + jnp.dot(sq_lo, head_mean, preferred_element_type=F32))
        y = o * lax.rsqrt(ms + EPS) * _silu(rg_ref[rows, :].astype(F32))
        o_ref[rows, :] = y.astype(BF16)
        state_update(sb_ref, k, v, kw_b, gl_b)
        return carry

    lax.fori_loop(0, nc, bwd_chunk, 0, unroll=16)


def _retention(proj, cproj, lg_lane, lg_chunk):
    b, n, _ = proj.shape
    n_ctx = cproj.shape[1]
    L = RET_L
    col = lambda c0: pl.BlockSpec((None, n, GROUP_W), lambda bb, g: (bb, 0, c0 + g))
    ccol = lambda c0: pl.BlockSpec((None, n_ctx, GROUP_W), lambda bb, g: (bb, 0, c0 + g))
    return pl.pallas_call(
        _ret_kernel,
        grid=(b, N_GROUPS),
        in_specs=[col(COL_RQ), col(COL_RK), col(COL_RV), col(COL_RG), ccol(CCOL_RK), ccol(CCOL_RV),
                  pl.BlockSpec((2, GROUP_W), lambda bb, g: (0, g)),
                  pl.BlockSpec((2, HEAD_GROUP * L), lambda bb, g: (0, g))],
        out_specs=pl.BlockSpec((None, n, GROUP_W), lambda bb, g: (bb, 0, g)),
        out_shape=jax.ShapeDtypeStruct((b, n, MIX_W), BF16),
        scratch_shapes=[pltpu.VMEM((n, GROUP_W), F32),
                        pltpu.VMEM((HEAD_GROUP * L, L), F32),
                        pltpu.VMEM((GROUP_W, GROUP_W), F32),
                        pltpu.VMEM((GROUP_W, GROUP_W), F32)],
        compiler_params=_cparams("arbitrary", "arbitrary"),
        name="retention",
    )(proj, proj, proj, proj, cproj, cproj, lg_lane, lg_chunk)


def _na_patterns(rows):
    nblk = rows // NA_ROWS
    assert rows % NA_ROWS == 0 and nblk >= 3 and rows >= NA_KROWS
    pats = []
    for qb in (0, 1, nblk - 1):
        kb = _na_key_start(qb, rows)
        qr = qb * NA_ROWS + np.arange(NA_ROWS)
        r0 = np.clip(qr - NA_WIN_H // 2, 0, rows - NA_WIN_H)
        assert np.all(r0 >= kb) and np.all(r0 + NA_WIN_H <= kb + NA_KROWS)
        pats.append((tuple(int(v) for v in r0 - kb), int(kb - qb * NA_ROWS + NA_WIN_H - 1)))
    for qb in range(1, nblk - 1):
        kb = _na_key_start(qb, rows)
        qr = qb * NA_ROWS + np.arange(NA_ROWS)
        r0 = np.clip(qr - NA_WIN_H // 2, 0, rows - NA_WIN_H)
        assert (tuple(int(v) for v in r0 - kb), int(kb - qb * NA_ROWS + NA_WIN_H - 1)) == pats[1]
    return pats


def _na_key_start(qb, rows):
    assert (NA_WIN_H // 2) % NA_ROWS == 0 and NA_KROWS % NA_ROWS == 0
    return min(max(qb * NA_ROWS - NA_WIN_H // 2, 0), rows - NA_KROWS)


def _na_slot_range(pats):
    lo = min(off - (NA_ROWS - 1) for _, off in pats)
    hi = max(off + NA_KROWS - 1 for _, off in pats)
    return lo, hi


def _na_kernel(q_ref, k_ref, vt_ref, ck_ref, cv_ref, tz_ref, o_ref,
               bias_ref, cvt_ref, s0_ref, s1_ref, m0_ref, m1_ref, *, pats):
    n = q_ref.shape[0]
    n_ctx = ck_ref.shape[0]
    rows = n // GRID_W
    nblk = rows // NA_ROWS
    _, head_mask = _head_masks()
    _, dr_hi = _na_slot_range(pats)
    n_keys = NA_K + n_ctx

    @pl.when(pl.program_id(1) == 0)
    def _build_bias():
        key = lax.broadcasted_iota(jnp.int32, (NA_K, NA_Q), 0)
        qry = lax.broadcasted_iota(jnp.int32, (NA_K, NA_Q), 1)
        j, kc = key // GRID_W, key % GRID_W
        i, qc = qry // GRID_W, qry % GRID_W
        c0 = jnp.clip(qc - NA_WIN_W // 2, 0, GRID_W - NA_WIN_W)
        col_ok = (kc >= c0) & (kc < c0 + NA_WIN_W)
        for p, (r0, off) in enumerate(pats):
            lo = jnp.zeros_like(i)
            for ii in range(NA_ROWS):
                lo = jnp.where(i == ii, r0[ii], lo)
            ok = col_ok & (j >= lo) & (j < lo + NA_WIN_H)
            for h in range(HEAD_GROUP):
                t = tz_ref[h]
                for jj in range(NA_KROWS):
                    slot = dr_hi - (jj + off)
                    tile = t[:, slot * GRID_W:slot * GRID_W + NA_Q] * LOG2E
                    rs = slice(jj * GRID_W, (jj + 1) * GRID_W)
                    bias_ref[p, rs, h * NA_Q:(h + 1) * NA_Q] = jnp.where(ok[rs, :], tile, NEG_INF)

    cvt_ref[...] = cv_ref[...].astype(F32).T.astype(BF16)
    ones_rows = jnp.ones((2 * 8, n_keys), BF16)

    def block_rows(qb):
        kb = jnp.clip(qb * NA_ROWS - NA_WIN_H // 2, 0, rows - NA_KROWS)
        q_rows = pl.ds(pl.multiple_of(qb * NA_Q, NA_Q), NA_Q)
        k_tok = pl.ds(pl.multiple_of(kb * GRID_W, NA_Q), NA_K)
        return q_rows, k_tok

    def scores(qb, s_ref, m_ref):
        q_rows, k_tok = block_rows(qb)
        pat = jnp.where(qb == 0, 0, jnp.where(qb == nblk - 1, 2, 1))
        q = q_ref[q_rows, :]
        q_heads = jnp.concatenate([q * head_mask[h] for h in range(HEAD_GROUP)], axis=0)
        s_ref[0:NA_K, :] = (lax.dot_general(k_ref[k_tok, :], q_heads, _NT, preferred_element_type=F32)
                            + bias_ref[pat])
        s_ref[NA_K:n_keys, :] = lax.dot_general(ck_ref[...], q_heads, _NT, preferred_element_type=F32)
        m_ref[...] = jnp.max(s_ref[...], axis=0, keepdims=True)

    def attend(qb, s_ref, m_ref):
        q_rows, k_tok = block_rows(qb)
        p = jnp.exp2(s_ref[...] - m_ref[...]).astype(BF16)
        outs = []
        for h in range(HEAD_GROUP):
            hs = slice(h * HEAD_DIM, (h + 1) * HEAD_DIM)
            v_aug = jnp.concatenate(
                [jnp.concatenate([vt_ref[hs, k_tok], cvt_ref[hs, :]], axis=1), ones_rows], axis=0)
            oh = jnp.dot(v_aug, p[:, h * NA_Q:(h + 1) * NA_Q], preferred_element_type=F32)
            outs.append(oh[0:HEAD_DIM, :] / oh[HEAD_DIM:HEAD_DIM + 1, :])
        o_ref[q_rows, :] = jnp.concatenate(outs, axis=0).T.astype(BF16)

    assert nblk % 2 == 0
    scores(0, s0_ref, m0_ref)

    def pair(t, carry):
        qb = 2 * t
        scores(qb + 1, s1_ref, m1_ref)
        attend(qb, s0_ref, m0_ref)
        scores(jnp.minimum(qb + 2, nblk - 1), s0_ref, m0_ref)
        attend(qb + 1, s1_ref, m1_ref)
        return carry

    lax.fori_loop(0, nblk // 2, pair, 0, unroll=4)


def _na_toeplitz(rpb, pats):
    dr_lo, dr_hi = _na_slot_range(pats)
    slots = dr_hi - dr_lo + 1
    slots += (-slots) % (LANES // GRID_W)
    row_sel = np.zeros((RPB_ROWS, slots), np.float32)
    for s in range(slots):
        if 0 <= dr_hi - s < RPB_ROWS:
            row_sel[dr_hi - s, s] = 1.0
    col = np.arange(GRID_W)
    dc = col[:, None] - col[None, :] + NA_WIN_W - 1
    col_sel = (dc[None, :, :] == np.arange(RPB_COLS)[:, None, None]).astype(np.float32)
    t = jnp.einsum('hab,as,bkq->hksq', rpb, row_sel, col_sel, precision=lax.Precision.HIGHEST)
    return t.reshape(rpb.shape[0], GRID_W, slots * GRID_W)


def _neighborhood(proj, vt, cproj, tz, pats):
    b, n, _ = proj.shape
    n_ctx = cproj.shape[1]
    col = lambda c0: pl.BlockSpec((None, n, GROUP_W), lambda g, bb: (bb, 0, c0 + g))
    ccol = lambda c0: pl.BlockSpec((None, n_ctx, GROUP_W), lambda g, bb: (bb, 0, c0 + g))
    return pl.pallas_call(
        functools.partial(_na_kernel, pats=pats),
        grid=(N_GROUPS, b),
        in_specs=[col(COL_NQ), col(COL_NK),
                  pl.BlockSpec((None, GROUP_W, n), lambda g, bb: (bb, g, 0)),
                  ccol(CCOL_NK), ccol(CCOL_NV),
                  pl.BlockSpec((HEAD_GROUP,) + tz.shape[1:], lambda g, bb: (g, 0, 0))],
        out_specs=pl.BlockSpec((None, n, GROUP_W), lambda g, bb: (bb, 0, g)),
        out_shape=jax.ShapeDtypeStruct((b, n, MIX_W), BF16),
        scratch_shapes=[pltpu.VMEM((len(pats), NA_K, HEAD_GROUP * NA_Q), F32),
                        pltpu.VMEM((GROUP_W, n_ctx), BF16),
                        pltpu.VMEM((NA_K + n_ctx, HEAD_GROUP * NA_Q), F32),
                        pltpu.VMEM((NA_K + n_ctx, HEAD_GROUP * NA_Q), F32),
                        pltpu.VMEM((1, HEAD_GROUP * NA_Q), F32),
                        pltpu.VMEM((1, HEAD_GROUP * NA_Q), F32)],
        compiler_params=_cparams("arbitrary", "arbitrary"),
        name="neighborhood",
    )(proj, proj, vt, cproj, cproj, tz)


def _mix_ffn_kernel(x_ref, yr_ref, yn_ref, gr_ref, gn_ref, gt1_ref, sh2_ref, sc2_ref, gt2_ref,
                    g_post_mix_ref, g_pre_ffn_ref, g_post_ffn_ref,
                    w_ret_hbm, w_na_hbm, w_o_hbm, w1_hbm, w2_hbm, o_ref,
                    w_ret_ref, w_na_ref, w_o_ref, w1_ref, w2_ref, stage_ref, sems):
    @pl.when((pl.program_id(0) == 0) & (pl.program_id(1) == 0))
    def _load_weights():
        width = stage_ref.shape[2]
        pairs = ((w_ret_hbm, w_ret_ref), (w_na_hbm, w_na_ref), (w_o_hbm, w_o_ref),
                 (w1_hbm, w1_ref), (w2_hbm, w2_ref))
        tasks = [(src, off, dst, off, False) for src, dst in pairs for off in range(0, src.shape[1], width)]
        _stream_cast(tasks, stage_ref, sems)

    tm = x_ref.shape[0]
    d_ff = w1_ref.shape[1]
    a = jnp.dot(yr_ref[...], w_ret_ref[...], preferred_element_type=F32)
    bb = jnp.dot(yn_ref[...], w_na_ref[...], preferred_element_type=F32)
    y = (jax.nn.sigmoid(gr_ref[...].astype(F32)) * a + jax.nn.sigmoid(gn_ref[...].astype(F32)) * bb)
    y = jnp.dot(y.astype(BF16), w_o_ref[...], preferred_element_type=F32)
    x1 = x_ref[...] + gt1_ref[...] * (_rms(y) * g_post_mix_ref[...])
    h2 = ((_rms(x1) * g_pre_ffn_ref[...]) * (1.0 + sc2_ref[...]) + sh2_ref[...]).astype(BF16)
    f = jnp.zeros((tm, x_ref.shape[1]), F32)
    for c0 in range(0, d_ff, FF_CHUNK):
        u = jnp.dot(h2, w1_ref[:, c0:c0 + FF_CHUNK], preferred_element_type=F32)
        u = jnp.square(jnp.maximum(u, 0.0)).astype(BF16)
        f = f + jnp.dot(u, w2_ref[c0:c0 + FF_CHUNK, :], preferred_element_type=F32)
    o_ref[...] = x1 + gt2_ref[...] * (_rms(f) * g_post_ffn_ref[...])


def _mix_ffn(x, y_ret, y_na, proj, mod4, g_post_mix, g_pre_ffn, g_post_ffn,
             w_ret, w_na, w_o, w1, w2):
    b, n, d = x.shape
    tm = MIX_TM
    tok = lambda w: pl.BlockSpec((None, tm, w), lambda i, bb: (bb, i, 0))
    mod_spec = lambda k: pl.BlockSpec((None, None, 1, d), lambda i, bb: (bb, k, 0, 0))
    return pl.pallas_call(
        _mix_ffn_kernel,
        grid=(n // tm, b),
        in_specs=[tok(d), tok(MIX_W), tok(MIX_W),
                  pl.BlockSpec((None, tm, d), lambda i, bb: (bb, i, 0)),
                  pl.BlockSpec((None, tm, d), lambda i, bb: (bb, i, 1)),
                  mod_spec(2), mod_spec(3), mod_spec(4), mod_spec(5),
                  _resident((1, d)), _resident((1, d)), _resident((1, d))]
                 + [pl.BlockSpec(memory_space=pl.ANY)] * 5,
        out_specs=tok(d),
        out_shape=jax.ShapeDtypeStruct((b, n, d), F32),
        scratch_shapes=[pltpu.VMEM(w.shape, BF16) for w in (w_ret, w_na, w_o, w1, w2)]
                       + [pltpu.VMEM((STAGE_SLOTS, MIX_STAGE_ROWS, d), F32),
                          pltpu.SemaphoreType.DMA((STAGE_SLOTS,))],
        compiler_params=_cparams("arbitrary", "arbitrary"),
        name="mix_ffn",
    )(x, y_ret, y_na, proj, proj, mod4, mod4, mod4, mod4, g_post_mix, g_pre_ffn, g_post_ffn,
      w_ret, w_na, w_o, w1, w2)


def _rope_tables(n):
    pos = np.arange(n)
    row = (pos // GRID_W).astype(np.float64)
    colp = (pos % GRID_W).astype(np.float64)
    d_axis = HEAD_DIM // 2
    inv = ROPE_BASE ** (-np.arange(0, d_axis, 2, dtype=np.float64) / d_axis)
    ang = np.concatenate([row[:, None] * inv, colp[:, None] * inv], axis=-1)
    cos, sin = np.cos(ang), np.sin(ang)
    zero = np.zeros_like(sin)
    reps = LANES // HEAD_DIM
    cos_t = np.tile(np.concatenate([cos, cos], axis=-1), (1, reps))
    sin_lo = np.tile(np.concatenate([-sin, zero], axis=-1), (1, reps))
    sin_hi = np.tile(np.concatenate([zero, sin], axis=-1), (1, reps))
    return tuple(jnp.asarray(t, F32) for t in (cos_t, sin_lo, sin_hi))


def kernel(x, c, ctx, c_ctx, w_ada, b_ada, norm_pre_mix, norm_post_mix, norm_pre_ffn, norm_post_ffn,
           w_in, ret_decay_logit, w_ret_out, na_rpb, w_na_out, w_o, w_ff1, w_ff2):
    b, n, d = x.shape
    n_ctx = ctx.shape[1]
    assert w_ada.shape[0] == 1, "single-layer block"
    assert n % NA_Q == 0 and n % PROJ_TM == 0 and n % MIX_TM == 0 and n % RET_L == 0
    assert n_ctx % RET_L == 0 and d == D_MODEL and 7 * MIX_W + GATE_COLS == w_in.shape[2]

    pad = (-(b + 1)) % 8
    c_rows = jnp.concatenate([c, c_ctx[None, :], jnp.zeros((pad, d), F32)], axis=0)
    mod = _ada(c_rows, w_ada[0], b_ada[0])
    mod4 = mod.reshape(c_rows.shape[0], 6, 1, d)

    g_pre_mix = norm_pre_mix[0].reshape(1, d)
    proj, nvt = _proj_latent(x, mod4, g_pre_mix, w_in[0], _rope_tables(n))
    cproj = _proj_context(ctx.reshape(b * n_ctx, d), mod4, b, g_pre_mix, w_in[0])
    cproj = cproj.reshape(b, n_ctx, 4 * MIX_W)

    logit = ret_decay_logit[0].astype(F32)
    y_ret = _retention(proj, cproj, jnp.repeat(logit, HEAD_DIM, axis=1), jnp.repeat(logit, RET_L, axis=1))

    pats = _na_patterns(n // GRID_W)
    y_na = _neighborhood(proj, nvt, cproj, _na_toeplitz(na_rpb[0].astype(F32), pats), pats)

    return _mix_ffn(x, y_ret, y_na, proj, mod4,
                    norm_post_mix[0].reshape(1, d), norm_pre_ffn[0].reshape(1, d),
                    norm_post_ffn[0].reshape(1, d),
                    w_ret_out[0], w_na_out[0], w_o[0], w_ff1[0], w_ff2[0])
```

```python
import functools
import math

import jax
import jax.numpy as jnp
import numpy as np
from jax import lax
from jax.experimental import pallas as pl
from jax.experimental.pallas import tpu as pltpu

F32 = jnp.float32
BF16 = jnp.bfloat16

D_MODEL = 1024
GRID_W = 64
HEADS = 8
HEAD_DIM = 64
MIX_W = HEADS * HEAD_DIM
NA_WIN_H = 8
NA_WIN_W = 16
ROPE_BASE = 10000.0
EPS = 1e-6
NEG_INF = -1e30
K_SCALE = HEAD_DIM ** -0.5
LOG2E = math.log2(math.e)

LANES = 128
V7X_VMEM_BYTES = 64 * 2 ** 20
VMEM_LIMIT = V7X_VMEM_BYTES - 8 * 2 ** 20

HEAD_GROUP = 4
GROUP_W = HEAD_GROUP * HEAD_DIM
N_GROUPS = HEADS // HEAD_GROUP
PROJ_TM = 1024
PROJ_SPLIT = 2
PROJ_STAGE_ROWS = 256
STAGE_SLOTS = 8
MIX_TM = 512
MIX_STAGE_ROWS = 256
FF_CHUNK = 1024
RET_L = 256
NA_ROWS = 4
NA_KROWS = 12
NA_Q = NA_ROWS * GRID_W
NA_K = NA_KROWS * GRID_W
RPB_ROWS = 2 * NA_WIN_H - 1
RPB_COLS = 2 * NA_WIN_W - 1

GATE_COLS = 2 * D_MODEL
COL_RQ, COL_RK, COL_RV, COL_RG, COL_NQ, COL_NK = (GATE_COLS // GROUP_W + i * N_GROUPS for i in range(6))
PROJ_COLS = GATE_COLS + 6 * MIX_W
CCOL_RK, CCOL_RV, CCOL_NK, CCOL_NV = (i * N_GROUPS for i in range(4))

_NT = (((1,), (1,)), ((), ()))
_TN = (((0,), (0,)), ((), ()))


def _cparams(*sem):
    return pltpu.CompilerParams(dimension_semantics=sem, vmem_limit_bytes=VMEM_LIMIT)


def _resident(shape):
    nd = len(shape)
    return pl.BlockSpec(shape, lambda *_: (0,) * nd, pipeline_mode=pl.Buffered(1))


def _stream_cast(tasks, stage_ref, sems):
    slots, chunk, width = stage_ref.shape
    jobs = [task + (r0,) for task in tasks for r0 in range(0, task[0].shape[0], chunk)]

    def copy(j):
        src, src_col, _, _, _, r0 = jobs[j]
        window = src.at[pl.ds(r0, chunk), pl.ds(src_col, width)]
        return pltpu.make_async_copy(window, stage_ref.at[j % slots], sems.at[j % slots])

    for j in range(min(slots - 1, len(jobs))):
        copy(j).start()
    for j in range(len(jobs)):
        if j + slots - 1 < len(jobs):
            copy(j + slots - 1).start()
        copy(j).wait()
        _, _, dst, dst_col, transpose, r0 = jobs[j]
        if transpose:
            dst[pl.ds(dst_col, width), pl.ds(r0, chunk)] = stage_ref[j % slots].T.astype(BF16)
        else:
            dst[pl.ds(r0, chunk), pl.ds(dst_col, width)] = stage_ref[j % slots].astype(BF16)


def _rms(x):
    return x * lax.rsqrt(jnp.mean(x * x, axis=-1, keepdims=True) + EPS)


def _silu(x):
    return x * jax.nn.sigmoid(x)


def _head_masks():
    lane_head = lax.broadcasted_iota(jnp.int32, (1, GROUP_W), 1) // HEAD_DIM
    sel = [lane_head == h for h in range(HEAD_GROUP)]
    return sel, [m.astype(BF16) for m in sel]


def _ada_kernel(c_ref, w_ref, b_ref, o_ref):
    a = _silu(c_ref[...])
    w = w_ref[...]
    a_hi = a.astype(BF16)
    a_lo = (a - a_hi.astype(F32)).astype(BF16)
    w_hi = w.astype(BF16)
    w_lo = (w - w_hi.astype(F32)).astype(BF16)
    rows = a.shape[0]
    top = jnp.dot(jnp.concatenate([a_hi, a_lo], axis=0), w_hi, preferred_element_type=F32)
    o_ref[...] = (top[:rows] + top[rows:] + jnp.dot(a_hi, w_lo, preferred_element_type=F32)) + b_ref[...]


def _ada(c_rows, w_ada, b_ada):
    rows, d = c_rows.shape
    cols = w_ada.shape[1]
    tn = d
    return pl.pallas_call(
        _ada_kernel,
        grid=(cols // tn,),
        in_specs=[pl.BlockSpec((rows, d), lambda j: (0, 0)),
                  pl.BlockSpec((d, tn), lambda j: (0, j)),
                  pl.BlockSpec((1, tn), lambda j: (0, j))],
        out_specs=pl.BlockSpec((rows, tn), lambda j: (0, j)),
        out_shape=jax.ShapeDtypeStruct((rows, cols), F32),
        compiler_params=_cparams("arbitrary"),
        name="ada_mod",
    )(c_rows, w_ada, b_ada.reshape(1, cols))


def _rope_slab(t, cos, sin_lo, sin_hi):
    return (t * cos + pltpu.roll(t, 3 * LANES // 4, axis=1) * sin_lo
            + pltpu.roll(t, LANES // 4, axis=1) * sin_hi)


def _modulated_norm(x_ref, sh_ref, sc_ref, g_ref):
    return ((_rms(x_ref[...]) * g_ref[...]) * (1.0 + sc_ref[...]) + sh_ref[...]).astype(BF16)


def _proj_latent_kernel(x_ref, sh_ref, sc_ref, g_ref, w_hbm, cos_ref, slo_ref, shi_ref,
                        o_ref, vt_ref, w_ref, wvt_ref, stage_ref, sems):
    @pl.when((pl.program_id(0) == 0) & (pl.program_id(1) == 0))
    def _load_weights():
        width = stage_ref.shape[2]
        tasks = [(w_hbm, 7 * MIX_W + off, w_ref, off, False) for off in range(0, GATE_COLS, width)]
        tasks += [(w_hbm, off, w_ref, GATE_COLS + off, False) for off in range(0, 6 * MIX_W, width)]
        tasks += [(w_hbm, 6 * MIX_W + off, wvt_ref, off, True) for off in range(0, MIX_W, width)]
        _stream_cast(tasks, stage_ref, sems)

    rope_lo, rope_hi = COL_RQ * GROUP_W, COL_RV * GROUP_W
    col_scale = {COL_RK * GROUP_W: K_SCALE, COL_NQ * GROUP_W: K_SCALE * LOG2E}
    tm = x_ref.shape[0]
    for r0 in range(0, tm, tm // PROJ_SPLIT):
        rows = slice(r0, r0 + tm // PROJ_SPLIT)
        x = x_ref[rows, :]
        h = ((_rms(x) * g_ref[...]) * (1.0 + sc_ref[...]) + sh_ref[...]).astype(BF16)
        for c0 in range(0, PROJ_COLS, MIX_W):
            r = jnp.dot(h, w_ref[:, c0:c0 + MIX_W], preferred_element_type=F32)
            if rope_lo <= c0 < rope_hi:
                cos, slo, shi = cos_ref[rows, :], slo_ref[rows, :], shi_ref[rows, :]
                r = jnp.concatenate(
                    [_rope_slab(r[:, j:j + LANES], cos, slo, shi) for j in range(0, MIX_W, LANES)], axis=1)
            if c0 in col_scale:
                r = r * col_scale[c0]
            o_ref[rows, c0:c0 + MIX_W] = r.astype(BF16)
        vt_ref[:, rows] = lax.dot_general(wvt_ref[...], h, _NT, preferred_element_type=F32).astype(BF16)


def _proj_latent(x, mod4, g, w_in, rope_tabs):
    b, n, d = x.shape
    tm = PROJ_TM
    mod_spec = lambda k: pl.BlockSpec((None, None, 1, d), lambda i, bb: (bb, k, 0, 0))
    tab_spec = pl.BlockSpec((tm, LANES), lambda i, bb: (i, 0))
    return pl.pallas_call(
        _proj_latent_kernel,
        grid=(n // tm, b),
        in_specs=[pl.BlockSpec((None, tm, d), lambda i, bb: (bb, i, 0)),
                  mod_spec(0), mod_spec(1), _resident((1, d)), pl.BlockSpec(memory_space=pl.ANY),
                  tab_spec, tab_spec, tab_spec],
        out_specs=[pl.BlockSpec((None, tm, PROJ_COLS), lambda i, bb: (bb, i, 0)),
                   pl.BlockSpec((None, MIX_W, tm), lambda i, bb: (bb, 0, i))],
        out_shape=[jax.ShapeDtypeStruct((b, n, PROJ_COLS), BF16),
                   jax.ShapeDtypeStruct((b, MIX_W, n), BF16)],
        scratch_shapes=[pltpu.VMEM((d, PROJ_COLS), BF16), pltpu.VMEM((MIX_W, d), BF16),
                        pltpu.VMEM((STAGE_SLOTS, PROJ_STAGE_ROWS, MIX_W), F32),
                        pltpu.SemaphoreType.DMA((STAGE_SLOTS,))],
        compiler_params=_cparams("arbitrary", "arbitrary"),
        name="proj_latent",
    )(x, mod4, mod4, g, w_in, *rope_tabs)


def _proj_context_kernel(x_ref, sh_ref, sc_ref, g_ref, w_hbm, o_ref, w_ref, stage_ref, sems):
    @pl.when(pl.program_id(0) == 0)
    def _load_weights():
        width = stage_ref.shape[2]
        tasks = [(w_hbm, MIX_W + off, w_ref, off, False) for off in range(0, 2 * MIX_W, width)]
        tasks += [(w_hbm, 5 * MIX_W + off, w_ref, 2 * MIX_W + off, False) for off in range(0, 2 * MIX_W, width)]
        _stream_cast(tasks, stage_ref, sems)

    h = _modulated_norm(x_ref, sh_ref, sc_ref, g_ref)
    for c0 in range(0, o_ref.shape[-1], MIX_W):
        r = jnp.dot(h, w_ref[:, c0:c0 + MIX_W], preferred_element_type=F32)
        if c0 == CCOL_RK * GROUP_W:
            r = r * K_SCALE
        o_ref[:, c0:c0 + MIX_W] = r.astype(BF16)


def _proj_context(ctx_rows, mod4, ctx_row, g, w_in):
    rows, d = ctx_rows.shape
    cols = 4 * MIX_W
    tm = min(PROJ_TM, rows)
    mod_spec = lambda k: pl.BlockSpec((None, None, 1, d), lambda i: (ctx_row, k, 0, 0))
    return pl.pallas_call(
        _proj_context_kernel,
        grid=(rows // tm,),
        in_specs=[pl.BlockSpec((tm, d), lambda i: (i, 0)),
                  mod_spec(0), mod_spec(1), _resident((1, d)), pl.BlockSpec(memory_space=pl.ANY)],
        out_specs=pl.BlockSpec((tm, cols), lambda i: (i, 0)),
        out_shape=jax.ShapeDtypeStruct((rows, cols), BF16),
        scratch_shapes=[pltpu.VMEM((d, cols), BF16), pltpu.VMEM((STAGE_SLOTS, PROJ_STAGE_ROWS, MIX_W), F32),
                        pltpu.SemaphoreType.DMA((STAGE_SLOTS,))],
        compiler_params=_cparams("arbitrary"),
        name="proj_context",
    )(ctx_rows, mod4, mod4, g, w_in)


def _log_sigmoid(x):
    return jnp.minimum(x, 0.0) - jnp.log1p(jnp.exp(-jnp.abs(x)))


def _ret_kernel(q_ref, k_ref, v_ref, rg_ref, ck_ref, cv_ref, lg_lane_ref, lg_chunk_ref, o_ref,
                acc_ref, dec_ref, sf_ref, sb_ref):
    L = RET_L
    n = q_ref.shape[0]
    n_ctx = ck_ref.shape[0]
    nc, ncc = n // L, n_ctx // L

    lg = _log_sigmoid(lg_lane_ref[...])
    lgf, lgb = lg[0:1, :], lg[1:2, :]
    pos = lax.broadcasted_iota(jnp.int32, (L, 1), 0).astype(F32)
    qw_f = jnp.exp(lgf * (pos + 1.0))
    kw_f = jnp.exp(lgf * (L - 1.0 - pos))
    qw_b = jnp.exp(lgb * (L - pos))
    kw_b = jnp.exp(lgb * pos)
    gl_f = jnp.exp(lgf * float(L))
    gl_b = jnp.exp(lgb * float(L))

    lane_head = lax.broadcasted_iota(jnp.int32, (1, GROUP_W), 1) // HEAD_DIM
    row_head = lax.broadcasted_iota(jnp.int32, (GROUP_W, 1), 0) // HEAD_DIM
    block_diag = row_head == lane_head
    head_mean = jnp.where(block_diag, 1.0 / HEAD_DIM, 0.0).astype(BF16)
    _, head_mask = _head_masks()

    lgc = _log_sigmoid(lg_chunk_ref[...])
    diff = pos - lax.broadcasted_iota(jnp.int32, (1, L), 1).astype(F32)
    for h in range(HEAD_GROUP):
        f = jnp.exp(lgc[0:1, h * L:(h + 1) * L] * jnp.maximum(diff, 0.0))
        bk = jnp.exp(lgc[1:2, h * L:(h + 1) * L] * jnp.maximum(-diff, 0.0))
        dec_ref[h * L:(h + 1) * L, :] = jnp.where(diff >= 0.0, f, bk)

    def state_update(s_ref, k, v, kw, gl):
        kd = (k.astype(F32) * kw).astype(BF16)
        contrib = lax.dot_general(kd, v, _TN, preferred_element_type=F32)
        s_ref[...] = s_ref[...] * gl + jnp.where(block_diag, contrib, 0.0)

    sf_ref[...] = jnp.zeros_like(sf_ref)
    sb_ref[...] = jnp.zeros_like(sb_ref)

    def ctx_fwd(c, carry):
        rows = pl.ds(pl.multiple_of(c * L, L), L)
        state_update(sf_ref, ck_ref[rows, :], cv_ref[rows, :], kw_f, gl_f)
        return carry

    def ctx_bwd(i, carry):
        rows = pl.ds(pl.multiple_of((ncc - 1 - i) * L, L), L)
        state_update(sb_ref, ck_ref[rows, :], cv_ref[rows, :], kw_b, gl_b)
        return carry

    lax.fori_loop(0, ncc, ctx_fwd, 0)
    lax.fori_loop(0, ncc, ctx_bwd, 0)

    def fwd_chunk(c, carry):
        rows = pl.ds(pl.multiple_of(c * L, L), L)
        q, k, v = q_ref[rows, :], k_ref[rows, :], v_ref[rows, :]
        q_heads = jnp.concatenate([q * head_mask[h] for h in range(HEAD_GROUP)], axis=0)
        s = lax.dot_general(q_heads, k, _NT, preferred_element_type=F32)
        s = (s * dec_ref[...]).astype(BF16)
        s_cat = jnp.concatenate([s[h * L:(h + 1) * L, :] for h in range(HEAD_GROUP)], axis=1)
        v_heads = jnp.concatenate([v * head_mask[h] for h in range(HEAD_GROUP)], axis=0)
        o = jnp.dot(s_cat, v_heads, preferred_element_type=F32)
        o = o + jnp.dot(q, sf_ref[...].astype(BF16), preferred_element_type=F32) * qw_f
        acc_ref[rows, :] = o
        state_update(sf_ref, k, v, kw_f, gl_f)
        return carry

    lax.fori_loop(0, nc, fwd_chunk, 0, unroll=16)

    def bwd_chunk(i, carry):
        rows = pl.ds(pl.multiple_of((nc - 1 - i) * L, L), L)
        q, k, v = q_ref[rows, :], k_ref[rows, :], v_ref[rows, :]
        o = acc_ref[rows, :] + jnp.dot(q, sb_ref[...].astype(BF16), preferred_element_type=F32) * qw_b
        sq = o * o
        sq_hi = sq.astype(BF16)
        sq_lo = (sq - sq_hi.astype(F32)).astype(BF16)
        ms = (jnp.dot(sq_hi, head_mean, preferred_element_type=F32)
              + jnp.dot(sq_lo, head_mean, preferred_element_type=F32))
        y = o * lax.rsqrt(ms + EPS) * _silu(rg_ref[rows, :].astype(F32))
        o_ref[rows, :] = y.astype(BF16)
        state_update(sb_ref, k, v, kw_b, gl_b)
        return carry

    lax.fori_loop(0, nc, bwd_chunk, 0, unroll=16)


def _retention(proj, cproj, lg_lane, lg_chunk):
    b, n, _ = proj.shape
    n_ctx = cproj.shape[1]
    L = RET_L
    col = lambda c0: pl.BlockSpec((None, n, GROUP_W), lambda bb, g: (bb, 0, c0 + g))
    ccol = lambda c0: pl.BlockSpec((None, n_ctx, GROUP_W), lambda bb, g: (bb, 0, c0 + g))
    return pl.pallas_call(
        _ret_kernel,
        grid=(b, N_GROUPS),
        in_specs=[col(COL_RQ), col(COL_RK), col(COL_RV), col(COL_RG), ccol(CCOL_RK), ccol(CCOL_RV),
                  pl.BlockSpec((2, GROUP_W), lambda bb, g: (0, g)),
                  pl.BlockSpec((2, HEAD_GROUP * L), lambda bb, g: (0, g))],
        out_specs=pl.BlockSpec((None, n, GROUP_W), lambda bb, g: (bb, 0, g)),
        out_shape=jax.ShapeDtypeStruct((b, n, MIX_W), BF16),
        scratch_shapes=[pltpu.VMEM((n, GROUP_W), F32),
                        pltpu.VMEM((HEAD_GROUP * L, L), F32),
                        pltpu.VMEM((GROUP_W, GROUP_W), F32),
                        pltpu.VMEM((GROUP_W, GROUP_W), F32)],
        compiler_params=_cparams("arbitrary", "arbitrary"),
        name="retention",
    )(proj, proj, proj, proj, cproj, cproj, lg_lane, lg_chunk)


def _na_patterns(rows):
    nblk = rows // NA_ROWS
    assert rows % NA_ROWS == 0 and nblk >= 3 and rows >= NA_KROWS
    pats = []
    for qb in (0, 1, nblk - 1):
        kb = _na_key_start(qb, rows)
        qr = qb * NA_ROWS + np.arange(NA_ROWS)
        r0 = np.clip(qr - NA_WIN_H // 2, 0, rows - NA_WIN_H)
        assert np.all(r0 >= kb) and np.all(r0 + NA_WIN_H <= kb + NA_KROWS)
        pats.append((tuple(int(v) for v in r0 - kb), int(kb - qb * NA_ROWS + NA_WIN_H - 1)))
    for qb in range(1, nblk - 1):
        kb = _na_key_start(qb, rows)
        qr = qb * NA_ROWS + np.arange(NA_ROWS)
        r0 = np.clip(qr - NA_WIN_H // 2, 0, rows - NA_WIN_H)
        assert (tuple(int(v) for v in r0 - kb), int(kb - qb * NA_ROWS + NA_WIN_H - 1)) == pats[1]
    return pats


def _na_key_start(qb, rows):
    assert (NA_WIN_H // 2) % NA_ROWS == 0 and NA_KROWS % NA_ROWS == 0
    return min(max(qb * NA_ROWS - NA_WIN_H // 2, 0), rows - NA_KROWS)


def _na_slot_range(pats):
    lo = min(off - (NA_ROWS - 1) for _, off in pats)
    hi = max(off + NA_KROWS - 1 for _, off in pats)
    return lo, hi


def _na_kernel(q_ref, k_ref, vt_ref, ck_ref, cv_ref, tz_ref, o_ref,
               bias_ref, cvt_ref, s0_ref, s1_ref, m0_ref, m1_ref, *, pats):
    n = q_ref.shape[0]
    n_ctx = ck_ref.shape[0]
    rows = n // GRID_W
    nblk = rows // NA_ROWS
    _, head_mask = _head_masks()
    _, dr_hi = _na_slot_range(pats)
    n_keys = NA_K + n_ctx

    @pl.when(pl.program_id(1) == 0)
    def _build_bias():
        key = lax.broadcasted_iota(jnp.int32, (NA_K, NA_Q), 0)
        qry = lax.broadcasted_iota(jnp.int32, (NA_K, NA_Q), 1)
        j, kc = key // GRID_W, key % GRID_W
        i, qc = qry // GRID_W, qry % GRID_W
        c0 = jnp.clip(qc - NA_WIN_W // 2, 0, GRID_W - NA_WIN_W)
        col_ok = (kc >= c0) & (kc < c0 + NA_WIN_W)
        for p, (r0, off) in enumerate(pats):
            lo = jnp.zeros_like(i)
            for ii in range(NA_ROWS):
                lo = jnp.where(i == ii, r0[ii], lo)
            ok = col_ok & (j >= lo) & (j < lo + NA_WIN_H)
            for h in range(HEAD_GROUP):
                t = tz_ref[h]
                for jj in range(NA_KROWS):
                    slot = dr_hi - (jj + off)
                    tile = t[:, slot * GRID_W:slot * GRID_W + NA_Q] * LOG2E
                    rs = slice(jj * GRID_W, (jj + 1) * GRID_W)
                    bias_ref[p, rs, h * NA_Q:(h + 1) * NA_Q] = jnp.where(ok[rs, :], tile, NEG_INF)

    cvt_ref[...] = cv_ref[...].astype(F32).T.astype(BF16)
    ones_rows = jnp.ones((2 * 8, n_keys), BF16)

    def block_rows(qb):
        kb = jnp.clip(qb * NA_ROWS - NA_WIN_H // 2, 0, rows - NA_KROWS)
        q_rows = pl.ds(pl.multiple_of(qb * NA_Q, NA_Q), NA_Q)
        k_tok = pl.ds(pl.multiple_of(kb * GRID_W, NA_Q), NA_K)
        return q_rows, k_tok

    def scores(qb, s_ref, m_ref):
        q_rows, k_tok = block_rows(qb)
        pat = jnp.where(qb == 0, 0, jnp.where(qb == nblk - 1, 2, 1))
        q = q_ref[q_rows, :]
        q_heads = jnp.concatenate([q * head_mask[h] for h in range(HEAD_GROUP)], axis=0)
        s_ref[0:NA_K, :] = (lax.dot_general(k_ref[k_tok, :], q_heads, _NT, preferred_element_type=F32)
                            + bias_ref[pat])
        s_ref[NA_K:n_keys, :] = lax.dot_general(ck_ref[...], q_heads, _NT, preferred_element_type=F32)
        m_ref[...] = jnp.max(s_ref[...], axis=0, keepdims=True)

    def attend(qb, s_ref, m_ref):
        q_rows, k_tok = block_rows(qb)
        p = jnp.exp2(s_ref[...] - m_ref[...]).astype(BF16)
        outs = []
        for h in range(HEAD_GROUP):
            hs = slice(h * HEAD_DIM, (h + 1) * HEAD_DIM)
            v_aug = jnp.concatenate(
                [jnp.concatenate([vt_ref[hs, k_tok], cvt_ref[hs, :]], axis=1), ones_rows], axis=0)
            oh = jnp.dot(v_aug, p[:, h * NA_Q:(h + 1) * NA_Q], preferred_element_type=F32)
            outs.append(oh[0:HEAD_DIM, :] / oh[HEAD_DIM:HEAD_DIM + 1, :])
        o_ref[q_rows, :] = jnp.concatenate(outs, axis=0).T.astype(BF16)

    assert nblk % 2 == 0
    scores(0, s0_ref, m0_ref)

    def pair(t, carry):
        qb = 2 * t
        scores(qb + 1, s1_ref, m1_ref)
        attend(qb, s0_ref, m0_ref)
        scores(jnp.minimum(qb + 2, nblk - 1), s0_ref, m0_ref)
        attend(qb + 1, s1_ref, m1_ref)
        return carry

    lax.fori_loop(0, nblk // 2, pair, 0, unroll=4)


def _na_toeplitz(rpb, pats):
    dr_lo, dr_hi = _na_slot_range(pats)
    slots = dr_hi - dr_lo + 1
    slots += (-slots) % (LANES // GRID_W)
    row_sel = np.zeros((RPB_ROWS, slots), np.float32)
    for s in range(slots):
        if 0 <= dr_hi - s < RPB_ROWS:
            row_sel[dr_hi - s, s] = 1.0
    col = np.arange(GRID_W)
    dc = col[:, None] - col[None, :] + NA_WIN_W - 1
    col_sel = (dc[None, :, :] == np.arange(RPB_COLS)[:, None, None]).astype(np.float32)
    t = jnp.einsum('hab,as,bkq->hksq', rpb, row_sel, col_sel, precision=lax.Precision.HIGHEST)
    return t.reshape(rpb.shape[0], GRID_W, slots * GRID_W)


def _neighborhood(proj, vt, cproj, tz, pats):
    b, n, _ = proj.shape
    n_ctx = cproj.shape[1]
    col = lambda c0: pl.BlockSpec((None, n, GROUP_W), lambda g, bb: (bb, 0, c0 + g))
    ccol = lambda c0: pl.BlockSpec((None, n_ctx, GROUP_W), lambda g, bb: (bb, 0, c0 + g))
    return pl.pallas_call(
        functools.partial(_na_kernel, pats=pats),
        grid=(N_GROUPS, b),
        in_specs=[col(COL_NQ), col(COL_NK),
                  pl.BlockSpec((None, GROUP_W, n), lambda g, bb: (bb, g, 0)),
                  ccol(CCOL_NK), ccol(CCOL_NV),
                  pl.BlockSpec((HEAD_GROUP,) + tz.shape[1:], lambda g, bb: (g, 0, 0))],
        out_specs=pl.BlockSpec((None, n, GROUP_W), lambda g, bb: (bb, 0, g)),
        out_shape=jax.ShapeDtypeStruct((b, n, MIX_W), BF16),
        scratch_shapes=[pltpu.VMEM((len(pats), NA_K, HEAD_GROUP * NA_Q), F32),
                        pltpu.VMEM((GROUP_W, n_ctx), BF16),
                        pltpu.VMEM((NA_K + n_ctx, HEAD_GROUP * NA_Q), F32),
                        pltpu.VMEM((NA_K + n_ctx, HEAD_GROUP * NA_Q), F32),
                        pltpu.VMEM((1, HEAD_GROUP * NA_Q), F32),
                        pltpu.VMEM((1, HEAD_GROUP * NA_Q), F32)],
        compiler_params=_cparams("arbitrary", "arbitrary"),
        name="neighborhood",
    )(proj, proj, vt, cproj, cproj, tz)


def _mix_ffn_kernel(x_ref, yr_ref, yn_ref, gr_ref, gn_ref, gt1_ref, sh2_ref, sc2_ref, gt2_ref,
                    g_post_mix_ref, g_pre_ffn_ref, g_post_ffn_ref,
                    w_ret_hbm, w_na_hbm, w_o_hbm, w1_hbm, w2_hbm, o_ref,
                    w_ret_ref, w_na_ref, w_o_ref, w1_ref, w2_ref, stage_ref, sems):
    @pl.when((pl.program_id(0) == 0) & (pl.program_id(1) == 0))
    def _load_weights():
        width = stage_ref.shape[2]
        pairs = ((w_ret_hbm, w_ret_ref), (w_na_hbm, w_na_ref), (w_o_hbm, w_o_ref),
                 (w1_hbm, w1_ref), (w2_hbm, w2_ref))
        tasks = [(src, off, dst, off, False) for src, dst in pairs for off in range(0, src.shape[1], width)]
        _stream_cast(tasks, stage_ref, sems)

    tm = x_ref.shape[0]
    d_ff = w1_ref.shape[1]
    a = jnp.dot(yr_ref[...], w_ret_ref[...], preferred_element_type=F32)
    bb = jnp.dot(yn_ref[...], w_na_ref[...], preferred_element_type=F32)
    y = (jax.nn.sigmoid(gr_ref[...].astype(F32)) * a + jax.nn.sigmoid(gn_ref[...].astype(F32)) * bb)
    y = jnp.dot(y.astype(BF16), w_o_ref[...], preferred_element_type=F32)
    x1 = x_ref[...] + gt1_ref[...] * (_rms(y) * g_post_mix_ref[...])
    h2 = ((_rms(x1) * g_pre_ffn_ref[...]) * (1.0 + sc2_ref[...]) + sh2_ref[...]).astype(BF16)
    f = jnp.zeros((tm, x_ref.shape[1]), F32)
    for c0 in range(0, d_ff, FF_CHUNK):
        u = jnp.dot(h2, w1_ref[:, c0:c0 + FF_CHUNK], preferred_element_type=F32)
        u = jnp.square(jnp.maximum(u, 0.0)).astype(BF16)
        f = f + jnp.dot(u, w2_ref[c0:c0 + FF_CHUNK, :], preferred_element_type=F32)
    o_ref[...] = x1 + gt2_ref[...] * (_rms(f) * g_post_ffn_ref[...])


def _mix_ffn(x, y_ret, y_na, proj, mod4, g_post_mix, g_pre_ffn, g_post_ffn,
             w_ret, w_na, w_o, w1, w2):
    b, n, d = x.shape
    tm = MIX_TM
    tok = lambda w: pl.BlockSpec((None, tm, w), lambda i, bb: (bb, i, 0))
    mod_spec = lambda k: pl.BlockSpec((None, None, 1, d), lambda i, bb: (bb, k, 0, 0))
    return pl.pallas_call(
        _mix_ffn_kernel,
        grid=(n // tm, b),
        in_specs=[tok(d), tok(MIX_W), tok(MIX_W),
                  pl.BlockSpec((None, tm, d), lambda i, bb: (bb, i, 0)),
                  pl.BlockSpec((None, tm, d), lambda i, bb: (bb, i, 1)),
                  mod_spec(2), mod_spec(3), mod_spec(4), mod_spec(5),
                  _resident((1, d)), _resident((1, d)), _resident((1, d))]
                 + [pl.BlockSpec(memory_space=pl.ANY)] * 5,
        out_specs=tok(d),
        out_shape=jax.ShapeDtypeStruct((b, n, d), F32),
        scratch_shapes=[pltpu.VMEM(w.shape, BF16) for w in (w_ret, w_na, w_o, w1, w2)]
                       + [pltpu.VMEM((STAGE_SLOTS, MIX_STAGE_ROWS, d), F32),
                          pltpu.SemaphoreType.DMA((STAGE_SLOTS,))],
        compiler_params=_cparams("arbitrary", "arbitrary"),
        name="mix_ffn",
    )(x, y_ret, y_na, proj, proj, mod4, mod4, mod4, mod4, g_post_mix, g_pre_ffn, g_post_ffn,
      w_ret, w_na, w_o, w1, w2)


def _rope_tables(n):
    pos = np.arange(n)
    row = (pos // GRID_W).astype(np.float64)
    colp = (pos % GRID_W).astype(np.float64)
    d_axis = HEAD_DIM // 2
    inv = ROPE_BASE ** (-np.arange(0, d_axis, 2, dtype=np.float64) / d_axis)
    ang = np.concatenate([row[:, None] * inv, colp[:, None] * inv], axis=-1)
    cos, sin = np.cos(ang), np.sin(ang)
    zero = np.zeros_like(sin)
    reps = LANES // HEAD_DIM
    cos_t = np.tile(np.concatenate([cos, cos], axis=-1), (1, reps))
    sin_lo = np.tile(np.concatenate([-sin, zero], axis=-1), (1, reps))
    sin_hi = np.tile(np.concatenate([zero, sin], axis=-1), (1, reps))
    return tuple(jnp.asarray(t, F32) for t in (cos_t, sin_lo, sin_hi))


def kernel(x, c, ctx, c_ctx, w_ada, b_ada, norm_pre_mix, norm_post_mix, norm_pre_ffn, norm_post_ffn,
           w_in, ret_decay_logit, w_ret_out, na_rpb, w_na_out, w_o, w_ff1, w_ff2):
    b, n, d = x.shape
    n_ctx = ctx.shape[1]
    assert w_ada.shape[0] == 1, "single-layer block"
    assert n % NA_Q == 0 and n % PROJ_TM == 0 and n % MIX_TM == 0 and n % RET_L == 0
    assert n_ctx % RET_L == 0 and d == D_MODEL and 7 * MIX_W + GATE_COLS == w_in.shape[2]

    pad = (-(b + 1)) % 8
    c_rows = jnp.concatenate([c, c_ctx[None, :], jnp.zeros((pad, d), F32)], axis=0)
    mod = _ada(c_rows, w_ada[0], b_ada[0])
    mod4 = mod.reshape(c_rows.shape[0], 6, 1, d)

    g_pre_mix = norm_pre_mix[0].reshape(1, d)
    proj, nvt = _proj_latent(x, mod4, g_pre_mix, w_in[0], _rope_tables(n))
    cproj = _proj_context(ctx.reshape(b * n_ctx, d), mod4, b, g_pre_mix, w_in[0])
    cproj = cproj.reshape(b, n_ctx, 4 * MIX_W)

    logit = ret_decay_logit[0].astype(F32)
    y_ret = _retention(proj, cproj, jnp.repeat(logit, HEAD_DIM, axis=1), jnp.repeat(logit, RET_L, axis=1))

    pats = _na_patterns(n // GRID_W)
    y_na = _neighborhood(proj, nvt, cproj, _na_toeplitz(na_rpb[0].astype(F32), pats), pats)

    return _mix_ffn(x, y_ret, y_na, proj, mod4,
                    norm_post_mix[0].reshape(1, d), norm_pre_ffn[0].reshape(1, d),
                    norm_post_ffn[0].reshape(1, d),
                    w_ret_out[0], w_na_out[0], w_o[0], w_ff1[0], w_ff2[0])
```

```python
import functools
import math

import jax
import jax.numpy as jnp
import numpy as np
from jax import lax
from jax.experimental import pallas as pl
from jax.experimental.pallas import tpu as pltpu

F32 = jnp.float32
BF16 = jnp.bfloat16

D_MODEL = 1024
GRID_W = 64
HEADS = 8
HEAD_DIM = 64
MIX_W = HEADS * HEAD_DIM
NA_WIN_H = 8
NA_WIN_W = 16
ROPE_BASE = 10000.0
EPS = 1e-6
NEG_INF = -1e30
K_SCALE = HEAD_DIM ** -0.5
LOG2E = math.log2(math.e)

LANES = 128
V7X_VMEM_BYTES = 64 * 2 ** 20
VMEM_LIMIT = V7X_VMEM_BYTES - 8 * 2 ** 20

HEAD_GROUP = 4
GROUP_W = HEAD_GROUP * HEAD_DIM
N_GROUPS = HEADS // HEAD_GROUP
PROJ_TM = 1024
PROJ_SPLIT = 2
PROJ_STAGE_ROWS = 256
STAGE_SLOTS = 8
MIX_TM = 512
MIX_STAGE_ROWS = 256
FF_CHUNK = 1024
RET_L = 256
NA_ROWS = 4
NA_KROWS = 12
NA_Q = NA_ROWS * GRID_W
NA_K = NA_KROWS * GRID_W
RPB_ROWS = 2 * NA_WIN_H - 1
RPB_COLS = 2 * NA_WIN_W - 1

GATE_COLS = 2 * D_MODEL
COL_RQ, COL_RK, COL_RV, COL_RG, COL_NQ, COL_NK = (GATE_COLS // GROUP_W + i * N_GROUPS for i in range(6))
PROJ_COLS = GATE_COLS + 6 * MIX_W
CCOL_RK, CCOL_RV, CCOL_NK, CCOL_NV = (i * N_GROUPS for i in range(4))

_NT = (((1,), (1,)), ((), ()))
_TN = (((0,), (0,)), ((), ()))


def _cparams(*sem):
    return pltpu.CompilerParams(dimension_semantics=sem, vmem_limit_bytes=VMEM_LIMIT)


def _resident(shape):
    nd = len(shape)
    return pl.BlockSpec(shape, lambda *_: (0,) * nd, pipeline_mode=pl.Buffered(1))


def _stream_cast(tasks, stage_ref, sems):
    slots, chunk, width = stage_ref.shape
    jobs = [task + (r0,) for task in tasks for r0 in range(0, task[0].shape[0], chunk)]

    def copy(j):
        src, src_col, _, _, _, r0 = jobs[j]
        window = src.at[pl.ds(r0, chunk), pl.ds(src_col, width)]
        return pltpu.make_async_copy(window, stage_ref.at[j % slots], sems.at[j % slots])

    for j in range(min(slots - 1, len(jobs))):
        copy(j).start()
    for j in range(len(jobs)):
        if j + slots - 1 < len(jobs):
            copy(j + slots - 1).start()
        copy(j).wait()
        _, _, dst, dst_col, transpose, r0 = jobs[j]
        if transpose:
            dst[pl.ds(dst_col, width), pl.ds(r0, chunk)] = stage_ref[j % slots].T.astype(BF16)
        else:
            dst[pl.ds(r0, chunk), pl.ds(dst_col, width)] = stage_ref[j % slots].astype(BF16)


def _rms(x):
    return x * lax.rsqrt(jnp.mean(x * x, axis=-1, keepdims=True) + EPS)


def _silu(x):
    return x * jax.nn.sigmoid(x)


def _head_masks():
    lane_head = lax.broadcasted_iota(jnp.int32, (1, GROUP_W), 1) // HEAD_DIM
    sel = [lane_head == h for h in range(HEAD_GROUP)]
    return sel, [m.astype(BF16) for m in sel]


def _ada_kernel(c_ref, w_ref, b_ref, o_ref):
    a = _silu(c_ref[...])
    w = w_ref[...]
    a_hi = a.astype(BF16)
    a_lo = (a - a_hi.astype(F32)).astype(BF16)
    w_hi = w.astype(BF16)
    w_lo = (w - w_hi.astype(F32)).astype(BF16)
    rows = a.shape[0]
    top = jnp.dot(jnp.concatenate([a_hi, a_lo], axis=0), w_hi, preferred_element_type=F32)
    o_ref[...] = (top[:rows] + top[rows:] + jnp.dot(a_hi, w_lo, preferred_element_type=F32)) + b_ref[...]


def _ada(c_rows, w_ada, b_ada):
    rows, d = c_rows.shape
    cols = w_ada.shape[1]
    tn = d
    return pl.pallas_call(
        _ada_kernel,
        grid=(cols // tn,),
        in_specs=[pl.BlockSpec((rows, d), lambda j: (0, 0)),
                  pl.BlockSpec((d, tn), lambda j: (0, j)),
                  pl.BlockSpec((1, tn), lambda j: (0, j))],
        out_specs=pl.BlockSpec((rows, tn), lambda j: (0, j)),
        out_shape=jax.ShapeDtypeStruct((rows, cols), F32),
        compiler_params=_cparams("arbitrary"),
        name="ada_mod",
    )(c_rows, w_ada, b_ada.reshape(1, cols))


def _rope_slab(t, cos, sin_lo, sin_hi):
    return (t * cos + pltpu.roll(t, 3 * LANES // 4, axis=1) * sin_lo
            + pltpu.roll(t, LANES // 4, axis=1) * sin_hi)


def _modulated_norm(x_ref, sh_ref, sc_ref, g_ref):
    return ((_rms(x_ref[...]) * g_ref[...]) * (1.0 + sc_ref[...]) + sh_ref[...]).astype(BF16)


def _proj_latent_kernel(x_ref, sh_ref, sc_ref, g_ref, w_hbm, cos_ref, slo_ref, shi_ref,
                        o_ref, vt_ref, w_ref, wvt_ref, stage_ref, sems):
    @pl.when((pl.program_id(0) == 0) & (pl.program_id(1) == 0))
    def _load_weights():
        width = stage_ref.shape[2]
        tasks = [(w_hbm, 7 * MIX_W + off, w_ref, off, False) for off in range(0, GATE_COLS, width)]
        tasks += [(w_hbm, off, w_ref, GATE_COLS + off, False) for off in range(0, 6 * MIX_W, width)]
        tasks += [(w_hbm, 6 * MIX_W + off, wvt_ref, off, True) for off in range(0, MIX_W, width)]
        _stream_cast(tasks, stage_ref, sems)

    rope_lo, rope_hi = COL_RQ * GROUP_W, COL_RV * GROUP_W
    col_scale = {COL_RK * GROUP_W: K_SCALE, COL_NQ * GROUP_W: K_SCALE * LOG2E}
    tm = x_ref.shape[0]
    for r0 in range(0, tm, tm // PROJ_SPLIT):
        rows = slice(r0, r0 + tm // PROJ_SPLIT)
        x = x_ref[rows, :]
        h = ((_rms(x) * g_ref[...]) * (1.0 + sc_ref[...]) + sh_ref[...]).astype(BF16)
        for c0 in range(0, PROJ_COLS, MIX_W):
            r = jnp.dot(h, w_ref[:, c0:c0 + MIX_W], preferred_element_type=F32)
            if rope_lo <= c0 < rope_hi:
                cos, slo, shi = cos_ref[rows, :], slo_ref[rows, :], shi_ref[rows, :]
                r = jnp.concatenate(
                    [_rope_slab(r[:, j:j + LANES], cos, slo, shi) for j in range(0, MIX_W, LANES)], axis=1)
            if c0 in col_scale:
                r = r * col_scale[c0]
            o_ref[rows, c0:c0 + MIX_W] = r.astype(BF16)
        vt_ref[:, rows] = lax.dot_general(wvt_ref[...], h, _NT, preferred_element_type=F32).astype(BF16)


def _proj_latent(x, mod4, g, w_in, rope_tabs):
    b, n, d = x.shape
    tm = PROJ_TM
    mod_spec = lambda k: pl.BlockSpec((None, None, 1, d), lambda i, bb: (bb, k, 0, 0))
    tab_spec = pl.BlockSpec((tm, LANES), lambda i, bb: (i, 0))
    return pl.pallas_call(
        _proj_latent_kernel,
        grid=(n // tm, b),
        in_specs=[pl.BlockSpec((None, tm, d), lambda i, bb: (bb, i, 0)),
                  mod_spec(0), mod_spec(1), _resident((1, d)), pl.BlockSpec(memory_space=pl.ANY),
                  tab_spec, tab_spec, tab_spec],
        out_specs=[pl.BlockSpec((None, tm, PROJ_COLS), lambda i, bb: (bb, i, 0)),
                   pl.BlockSpec((None, MIX_W, tm), lambda i, bb: (bb, 0, i))],
        out_shape=[jax.ShapeDtypeStruct((b, n, PROJ_COLS), BF16),
                   jax.ShapeDtypeStruct((b, MIX_W, n), BF16)],
        scratch_shapes=[pltpu.VMEM((d, PROJ_COLS), BF16), pltpu.VMEM((MIX_W, d), BF16),
                        pltpu.VMEM((STAGE_SLOTS, PROJ_STAGE_ROWS, MIX_W), F32),
                        pltpu.SemaphoreType.DMA((STAGE_SLOTS,))],
        compiler_params=_cparams("arbitrary", "arbitrary"),
        name="proj_latent",
    )(x, mod4, mod4, g, w_in, *rope_tabs)


def _proj_context_kernel(x_ref, sh_ref, sc_ref, g_ref, w_hbm, o_ref, w_ref, stage_ref, sems):
    @pl.when(pl.program_id(0) == 0)
    def _load_weights():
        width = stage_ref.shape[2]
        tasks = [(w_hbm, MIX_W + off, w_ref, off, False) for off in range(0, 2 * MIX_W, width)]
        tasks += [(w_hbm, 5 * MIX_W + off, w_ref, 2 * MIX_W + off, False) for off in range(0, 2 * MIX_W, width)]
        _stream_cast(tasks, stage_ref, sems)

    h = _modulated_norm(x_ref, sh_ref, sc_ref, g_ref)
    for c0 in range(0, o_ref.shape[-1], MIX_W):
        r = jnp.dot(h, w_ref[:, c0:c0 + MIX_W], preferred_element_type=F32)
        if c0 == CCOL_RK * GROUP_W:
            r = r * K_SCALE
        o_ref[:, c0:c0 + MIX_W] = r.astype(BF16)


def _proj_context(ctx_rows, mod4, ctx_row, g, w_in):
    rows, d = ctx_rows.shape
    cols = 4 * MIX_W
    tm = min(PROJ_TM, rows)
    mod_spec = lambda k: pl.BlockSpec((None, None, 1, d), lambda i: (ctx_row, k, 0, 0))
    return pl.pallas_call(
        _proj_context_kernel,
        grid=(rows // tm,),
        in_specs=[pl.BlockSpec((tm, d), lambda i: (i, 0)),
                  mod_spec(0), mod_spec(1), _resident((1, d)), pl.BlockSpec(memory_space=pl.ANY)],
        out_specs=pl.BlockSpec((tm, cols), lambda i: (i, 0)),
        out_shape=jax.ShapeDtypeStruct((rows, cols), BF16),
        scratch_shapes=[pltpu.VMEM((d, cols), BF16), pltpu.VMEM((STAGE_SLOTS, PROJ_STAGE_ROWS, MIX_W), F32),
                        pltpu.SemaphoreType.DMA((STAGE_SLOTS,))],
        compiler_params=_cparams("arbitrary"),
        name="proj_context",
    )(ctx_rows, mod4, mod4, g, w_in)


def _log_sigmoid(x):
    return jnp.minimum(x, 0.0) - jnp.log1p(jnp.exp(-jnp.abs(x)))


def _ret_kernel(q_ref, k_ref, v_ref, rg_ref, ck_ref, cv_ref, lg_lane_ref, lg_chunk_ref, o_ref,
                acc_ref, dec_ref, sf_ref, sb_ref):
    L = RET_L
    n = q_ref.shape[0]
    n_ctx = ck_ref.shape[0]
    nc, ncc = n // L, n_ctx // L

    lg = _log_sigmoid(lg_lane_ref[...])
    lgf, lgb = lg[0:1, :], lg[1:2, :]
    pos = lax.broadcasted_iota(jnp.int32, (L, 1), 0).astype(F32)
    qw_f = jnp.exp(lgf * (pos + 1.0))
    kw_f = jnp.exp(lgf * (L - 1.0 - pos))
    qw_b = jnp.exp(lgb * (L - pos))
    kw_b = jnp.exp(lgb * pos)
    gl_f = jnp.exp(lgf * float(L))
    gl_b = jnp.exp(lgb * float(L))

    lane_head = lax.broadcasted_iota(jnp.int32, (1, GROUP_W), 1) // HEAD_DIM
    row_head = lax.broadcasted_iota(jnp.int32, (GROUP_W, 1), 0) // HEAD_DIM
    block_diag = row_head == lane_head
    head_mean = jnp.where(block_diag, 1.0 / HEAD_DIM, 0.0).astype(BF16)
    _, head_mask = _head_masks()

    lgc = _log_sigmoid(lg_chunk_ref[...])
    diff = pos - lax.broadcasted_iota(jnp.int32, (1, L), 1).astype(F32)
    for h in range(HEAD_GROUP):
        f = jnp.exp(lgc[0:1, h * L:(h + 1) * L] * jnp.maximum(diff, 0.0))
        bk = jnp.exp(lgc[1:2, h * L:(h + 1) * L] * jnp.maximum(-diff, 0.0))
        dec_ref[h * L:(h + 1) * L, :] = jnp.where(diff >= 0.0, f, bk)

    def state_update(s_ref, k, v, kw, gl):
        kd = (k.astype(F32) * kw).astype(BF16)
        contrib = lax.dot_general(kd, v, _TN, preferred_element_type=F32)
        s_ref[...] = s_ref[...] * gl + jnp.where(block_diag, contrib, 0.0)

    sf_ref[...] = jnp.zeros_like(sf_ref)
    sb_ref[...] = jnp.zeros_like(sb_ref)

    def ctx_fwd(c, carry):
        rows = pl.ds(pl.multiple_of(c * L, L), L)
        state_update(sf_ref, ck_ref[rows, :], cv_ref[rows, :], kw_f, gl_f)
        return carry

    def ctx_bwd(i, carry):
        rows = pl.ds(pl.multiple_of((ncc - 1 - i) * L, L), L)
        state_update(sb_ref, ck_ref[rows, :], cv_ref[rows, :], kw_b, gl_b)
        return carry

    lax.fori_loop(0, ncc, ctx_fwd, 0)
    lax.fori_loop(0, ncc, ctx_bwd, 0)

    def fwd_chunk(c, carry):
        rows = pl.ds(pl.multiple_of(c * L, L), L)
        q, k, v = q_ref[rows, :], k_ref[rows, :], v_ref[rows, :]
        q_heads = jnp.concatenate([q * head_mask[h] for h in range(HEAD_GROUP)], axis=0)
        s = lax.dot_general(q_heads, k, _NT, preferred_element_type=F32)
        s = (s * dec_ref[...]).astype(BF16)
        s_cat = jnp.concatenate([s[h * L:(h + 1) * L, :] for h in range(HEAD_GROUP)], axis=1)
        v_heads = jnp.concatenate([v * head_mask[h] for h in range(HEAD_GROUP)], axis=0)
        o = jnp.dot(s_cat, v_heads, preferred_element_type=F32)
        o = o + jnp.dot(q, sf_ref[...].astype(BF16), preferred_element_type=F32) * qw_f
        acc_ref[rows, :] = o
        state_update(sf_ref, k, v, kw_f, gl_f)
        return carry

    lax.fori_loop(0, nc, fwd_chunk, 0, unroll=16)

    def bwd_chunk(i, carry):
        rows = pl.ds(pl.multiple_of((nc - 1 - i) * L, L), L)
        q, k, v = q_ref[rows, :], k_ref[rows, :], v_ref[rows, :]
        o = acc_ref[rows, :] + jnp.dot(q, sb_ref[...].astype(BF16), preferred_element_type=F32) * qw_b
        sq = o * o
        sq_hi = sq.astype(BF16)
        sq_lo = (sq - sq_hi.astype(F32)).astype(BF16)
        ms = (jnp.dot(sq_hi, head_mean, preferred_element_type=F32)
              + jnp.dot(sq_lo, head_mean, preferred_element_type=F32))
        y = o * lax.rsqrt(ms + EPS) * _silu(rg_ref[rows, :].astype(F32))
        o_ref[rows, :] = y.astype(BF16)
        state_update(sb_ref, k, v, kw_b, gl_b)
        return carry

    lax.fori_loop(0, nc, bwd_chunk, 0, unroll=16)


def _retention(proj, cproj, lg_lane, lg_chunk):
    b, n, _ = proj.shape
    n_ctx = cproj.shape[1]
    L = RET_L
    col = lambda c0: pl.BlockSpec((None, n, GROUP_W), lambda bb, g: (bb, 0, c0 + g))
    ccol = lambda c0: pl.BlockSpec((None, n_ctx, GROUP_W), lambda bb, g: (bb, 0, c0 + g))
    return pl.pallas_call(
        _ret_kernel,
        grid=(b, N_GROUPS),
        in_specs=[col(COL_RQ), col(COL_RK), col(COL_RV), col(COL_RG), ccol(CCOL_RK), ccol(CCOL_RV),
                  pl.BlockSpec((2, GROUP_W), lambda bb, g: (0, g)),
                  pl.BlockSpec((2, HEAD_GROUP * L), lambda bb, g: (0, g))],
        out_specs=pl.BlockSpec((None, n, GROUP_W), lambda bb, g: (bb, 0, g)),
        out_shape=jax.ShapeDtypeStruct((b, n, MIX_W), BF16),
        scratch_shapes=[pltpu.VMEM((n, GROUP_W), F32),
                        pltpu.VMEM((HEAD_GROUP * L, L), F32),
                        pltpu.VMEM((GROUP_W, GROUP_W), F32),
                        pltpu.VMEM((GROUP_W, GROUP_W), F32)],
        compiler_params=_cparams("arbitrary", "arbitrary"),
        name="retention",
    )(proj, proj, proj, proj, cproj, cproj, lg_lane, lg_chunk)


def _na_patterns(rows):
    nblk = rows // NA_ROWS
    assert rows % NA_ROWS == 0 and nblk >= 3 and rows >= NA_KROWS
    pats = []
    for qb in (0, 1, nblk - 1):
        kb = _na_key_start(qb, rows)
        qr = qb * NA_ROWS + np.arange(NA_ROWS)
        r0 = np.clip(qr - NA_WIN_H // 2, 0, rows - NA_WIN_H)
        assert np.all(r0 >= kb) and np.all(r0 + NA_WIN_H <= kb + NA_KROWS)
        pats.append((tuple(int(v) for v in r0 - kb), int(kb - qb * NA_ROWS + NA_WIN_H - 1)))
    for qb in range(1, nblk - 1):
        kb = _na_key_start(qb, rows)
        qr = qb * NA_ROWS + np.arange(NA_ROWS)
        r0 = np.clip(qr - NA_WIN_H // 2, 0, rows - NA_WIN_H)
        assert (tuple(int(v) for v in r0 - kb), int(kb - qb * NA_ROWS + NA_WIN_H - 1)) == pats[1]
    return pats


def _na_key_start(qb, rows):
    assert (NA_WIN_H // 2) % NA_ROWS == 0 and NA_KROWS % NA_ROWS == 0
    return min(max(qb * NA_ROWS - NA_WIN_H // 2, 0), rows - NA_KROWS)


def _na_slot_range(pats):
    lo = min(off - (NA_ROWS - 1) for _, off in pats)
    hi = max(off + NA_KROWS - 1 for _, off in pats)
    return lo, hi


def _na_kernel(q_ref, k_ref, vt_ref, ck_ref, cv_ref, tz_ref, o_ref,
               bias_ref, cvt_ref, s0_ref, s1_ref, m0_ref, m1_ref, *, pats):
    n = q_ref.shape[0]
    n_ctx = ck_ref.shape[0]
    rows = n // GRID_W
    nblk = rows // NA_ROWS
    _, head_mask = _head_masks()
    _, dr_hi = _na_slot_range(pats)
    n_keys = NA_K + n_ctx

    @pl.when(pl.program_id(1) == 0)
    def _build_bias():
        key = lax.broadcasted_iota(jnp.int32, (NA_K, NA_Q), 0)
        qry = lax.broadcasted_iota(jnp.int32, (NA_K, NA_Q), 1)
        j, kc = key // GRID_W, key % GRID_W
        i, qc = qry // GRID_W, qry % GRID_W
        c0 = jnp.clip(qc - NA_WIN_W // 2, 0, GRID_W - NA_WIN_W)
        col_ok = (kc >= c0) & (kc < c0 + NA_WIN_W)
        for p, (r0, off) in enumerate(pats):
            lo = jnp.zeros_like(i)
            for ii in range(NA_ROWS):
                lo = jnp.where(i == ii, r0[ii], lo)
            ok = col_ok & (j >= lo) & (j < lo + NA_WIN_H)
            for h in range(HEAD_GROUP):
                t = tz_ref[h]
                for jj in range(NA_KROWS):
                    slot = dr_hi - (jj + off)
                    tile = t[:, slot * GRID_W:slot * GRID_W + NA_Q] * LOG2E
                    rs = slice(jj * GRID_W, (jj + 1) * GRID_W)
                    bias_ref[p, rs, h * NA_Q:(h + 1) * NA_Q] = jnp.where(ok[rs, :], tile, NEG_INF)

    cvt_ref[...] = cv_ref[...].astype(F32).T.astype(BF16)
    ones_rows = jnp.ones((2 * 8, n_keys), BF16)

    def block_rows(qb):
        kb = jnp.clip(qb * NA_ROWS - NA_WIN_H // 2, 0, rows - NA_KROWS)
        q_rows = pl.ds(pl.multiple_of(qb * NA_Q, NA_Q), NA_Q)
        k_tok = pl.ds(pl.multiple_of(kb * GRID_W, NA_Q), NA_K)
        return q_rows, k_tok

    def scores(qb, s_ref, m_ref):
        q_rows, k_tok = block_rows(qb)
        pat = jnp.where(qb == 0, 0, jnp.where(qb == nblk - 1, 2, 1))
        q = q_ref[q_rows, :]
        q_heads = jnp.concatenate([q * head_mask[h] for h in range(HEAD_GROUP)], axis=0)
        s_ref[0:NA_K, :] = (lax.dot_general(k_ref[k_tok, :], q_heads, _NT, preferred_element_type=F32)
                            + bias_ref[pat])
        s_ref[NA_K:n_keys, :] = lax.dot_general(ck_ref[...], q_heads, _NT, preferred_element_type=F32)
        m_ref[...] = jnp.max(s_ref[...], axis=0, keepdims=True)

    def attend(qb, s_ref, m_ref):
        q_rows, k_tok = block_rows(qb)
        p = jnp.exp2(s_ref[...] - m_ref[...]).astype(BF16)
        outs = []
        for h in range(HEAD_GROUP):
            hs = slice(h * HEAD_DIM, (h + 1) * HEAD_DIM)
            v_aug = jnp.concatenate(
                [jnp.concatenate([vt_ref[hs, k_tok], cvt_ref[hs, :]], axis=1), ones_rows], axis=0)
            oh = jnp.dot(v_aug, p[:, h * NA_Q:(h + 1) * NA_Q], preferred_element_type=F32)
            outs.append(oh[0:HEAD_DIM, :] / oh[HEAD_DIM:HEAD_DIM + 1, :])
        o_ref[q_rows, :] = jnp.concatenate(outs, axis=0).T.astype(BF16)

    assert nblk % 2 == 0
    scores(0, s0_ref, m0_ref)

    def pair(t, carry):
        qb = 2 * t
        scores(qb + 1, s1_ref, m1_ref)
        attend(qb, s0_ref, m0_ref)
        scores(jnp.minimum(qb + 2, nblk - 1), s0_ref, m0_ref)
        attend(qb + 1, s1_ref, m1_ref)
        return carry

    lax.fori_loop(0, nblk // 2, pair, 0, unroll=4)


def _na_toeplitz(rpb, pats):
    dr_lo, dr_hi = _na_slot_range(pats)
    slots = dr_hi - dr_lo + 1
    slots += (-slots) % (LANES // GRID_W)
    lane = np.arange(slots * GRID_W)
    row = dr_hi - lane // GRID_W
    row_sel = (row[None, :] == np.arange(RPB_ROWS)[:, None]).astype(np.float32)
    rows = jnp.einsum('hab,ax->hbx', rpb, row_sel, precision=lax.Precision.HIGHEST)
    dc = np.arange(GRID_W)[:, None] - (lane % GRID_W)[None, :] + NA_WIN_W - 1
    col_sel = dc[None, :, :] == np.arange(RPB_COLS)[:, None, None]
    return jnp.sum(jnp.where(col_sel[None], rows[:, :, None, :], 0.0), axis=1)


def _neighborhood(proj, vt, cproj, tz, pats):
    b, n, _ = proj.shape
    n_ctx = cproj.shape[1]
    col = lambda c0: pl.BlockSpec((None, n, GROUP_W), lambda g, bb: (bb, 0, c0 + g))
    ccol = lambda c0: pl.BlockSpec((None, n_ctx, GROUP_W), lambda g, bb: (bb, 0, c0 + g))
    return pl.pallas_call(
        functools.partial(_na_kernel, pats=pats),
        grid=(N_GROUPS, b),
        in_specs=[col(COL_NQ), col(COL_NK),
                  pl.BlockSpec((None, GROUP_W, n), lambda g, bb: (bb, g, 0)),
                  ccol(CCOL_NK), ccol(CCOL_NV),
                  pl.BlockSpec((HEAD_GROUP,) + tz.shape[1:], lambda g, bb: (g, 0, 0))],
        out_specs=pl.BlockSpec((None, n, GROUP_W), lambda g, bb: (bb, 0, g)),
        out_shape=jax.ShapeDtypeStruct((b, n, MIX_W), BF16),
        scratch_shapes=[pltpu.VMEM((len(pats), NA_K, HEAD_GROUP * NA_Q), F32),
                        pltpu.VMEM((GROUP_W, n_ctx), BF16),
                        pltpu.VMEM((NA_K + n_ctx, HEAD_GROUP * NA_Q), F32),
                        pltpu.VMEM((NA_K + n_ctx, HEAD_GROUP * NA_Q), F32),
                        pltpu.VMEM((1, HEAD_GROUP * NA_Q), F32),
                        pltpu.VMEM((1, HEAD_GROUP * NA_Q), F32)],
        compiler_params=_cparams("arbitrary", "arbitrary"),
        name="neighborhood",
    )(proj, proj, vt, cproj, cproj, tz)


def _mix_ffn_kernel(x_ref, yr_ref, yn_ref, gr_ref, gn_ref, gt1_ref, sh2_ref, sc2_ref, gt2_ref,
                    g_post_mix_ref, g_pre_ffn_ref, g_post_ffn_ref,
                    w_ret_hbm, w_na_hbm, w_o_hbm, w1_hbm, w2_hbm, o_ref,
                    w_ret_ref, w_na_ref, w_o_ref, w1_ref, w2_ref, stage_ref, sems):
    @pl.when((pl.program_id(0) == 0) & (pl.program_id(1) == 0))
    def _load_weights():
        width = stage_ref.shape[2]
        pairs = ((w_ret_hbm, w_ret_ref), (w_na_hbm, w_na_ref), (w_o_hbm, w_o_ref),
                 (w1_hbm, w1_ref), (w2_hbm, w2_ref))
        tasks = [(src, off, dst, off, False) for src, dst in pairs for off in range(0, src.shape[1], width)]
        _stream_cast(tasks, stage_ref, sems)

    tm = x_ref.shape[0]
    d_ff = w1_ref.shape[1]
    a = jnp.dot(yr_ref[...], w_ret_ref[...], preferred_element_type=F32)
    bb = jnp.dot(yn_ref[...], w_na_ref[...], preferred_element_type=F32)
    y = (jax.nn.sigmoid(gr_ref[...].astype(F32)) * a + jax.nn.sigmoid(gn_ref[...].astype(F32)) * bb)
    y = jnp.dot(y.astype(BF16), w_o_ref[...], preferred_element_type=F32)
    x1 = x_ref[...] + gt1_ref[...] * (_rms(y) * g_post_mix_ref[...])
    h2 = ((_rms(x1) * g_pre_ffn_ref[...]) * (1.0 + sc2_ref[...]) + sh2_ref[...]).astype(BF16)
    f = jnp.zeros((tm, x_ref.shape[1]), F32)
    for c0 in range(0, d_ff, FF_CHUNK):
        u = jnp.dot(h2, w1_ref[:, c0:c0 + FF_CHUNK], preferred_element_type=F32)
        u = jnp.square(jnp.maximum(u, 0.0)).astype(BF16)
        f = f + jnp.dot(u, w2_ref[c0:c0 + FF_CHUNK, :], preferred_element_type=F32)
    o_ref[...] = x1 + gt2_ref[...] * (_rms(f) * g_post_ffn_ref[...])


def _mix_ffn(x, y_ret, y_na, proj, mod4, g_post_mix, g_pre_ffn, g_post_ffn,
             w_ret, w_na, w_o, w1, w2):
    b, n, d = x.shape
    tm = MIX_TM
    tok = lambda w: pl.BlockSpec((None, tm, w), lambda i, bb: (bb, i, 0))
    mod_spec = lambda k: pl.BlockSpec((None, None, 1, d), lambda i, bb: (bb, k, 0, 0))
    return pl.pallas_call(
        _mix_ffn_kernel,
        grid=(n // tm, b),
        in_specs=[tok(d), tok(MIX_W), tok(MIX_W),
                  pl.BlockSpec((None, tm, d), lambda i, bb: (bb, i, 0)),
                  pl.BlockSpec((None, tm, d), lambda i, bb: (bb, i, 1)),
                  mod_spec(2), mod_spec(3), mod_spec(4), mod_spec(5),
                  _resident((1, d)), _resident((1, d)), _resident((1, d))]
                 + [pl.BlockSpec(memory_space=pl.ANY)] * 5,
        out_specs=tok(d),
        out_shape=jax.ShapeDtypeStruct((b, n, d), F32),
        scratch_shapes=[pltpu.VMEM(w.shape, BF16) for w in (w_ret, w_na, w_o, w1, w2)]
                       + [pltpu.VMEM((STAGE_SLOTS, MIX_STAGE_ROWS, d), F32),
                          pltpu.SemaphoreType.DMA((STAGE_SLOTS,))],
        compiler_params=_cparams("arbitrary", "arbitrary"),
        name="mix_ffn",
    )(x, y_ret, y_na, proj, proj, mod4, mod4, mod4, mod4, g_post_mix, g_pre_ffn, g_post_ffn,
      w_ret, w_na, w_o, w1, w2)


def _rope_tables(n):
    pos = np.arange(n)
    row = (pos // GRID_W).astype(np.float64)
    colp = (pos % GRID_W).astype(np.float64)
    d_axis = HEAD_DIM // 2
    inv = ROPE_BASE ** (-np.arange(0, d_axis, 2, dtype=np.float64) / d_axis)
    ang = np.concatenate([row[:, None] * inv, colp[:, None] * inv], axis=-1)
    cos, sin = np.cos(ang), np.sin(ang)
    zero = np.zeros_like(sin)
    reps = LANES // HEAD_DIM
    cos_t = np.tile(np.concatenate([cos, cos], axis=-1), (1, reps))
    sin_lo = np.tile(np.concatenate([-sin, zero], axis=-1), (1, reps))
    sin_hi = np.tile(np.concatenate([zero, sin], axis=-1), (1, reps))
    return tuple(jnp.asarray(t, F32) for t in (cos_t, sin_lo, sin_hi))


def kernel(x, c, ctx, c_ctx, w_ada, b_ada, norm_pre_mix, norm_post_mix, norm_pre_ffn, norm_post_ffn,
           w_in, ret_decay_logit, w_ret_out, na_rpb, w_na_out, w_o, w_ff1, w_ff2):
    b, n, d = x.shape
    n_ctx = ctx.shape[1]
    assert w_ada.shape[0] == 1, "single-layer block"
    assert n % NA_Q == 0 and n % PROJ_TM == 0 and n % MIX_TM == 0 and n % RET_L == 0
    assert n_ctx % RET_L == 0 and d == D_MODEL and 7 * MIX_W + GATE_COLS == w_in.shape[2]

    pad = (-(b + 1)) % 8
    c_rows = jnp.concatenate([c, c_ctx[None, :], jnp.zeros((pad, d), F32)], axis=0)
    mod = _ada(c_rows, w_ada[0], b_ada[0])
    mod4 = mod.reshape(c_rows.shape[0], 6, 1, d)

    g_pre_mix = norm_pre_mix[0].reshape(1, d)
    proj, nvt = _proj_latent(x, mod4, g_pre_mix, w_in[0], _rope_tables(n))
    cproj = _proj_context(ctx.reshape(b * n_ctx, d), mod4, b, g_pre_mix, w_in[0])
    cproj = cproj.reshape(b, n_ctx, 4 * MIX_W)

    logit = ret_decay_logit[0].astype(F32)
    y_ret = _retention(proj, cproj, jnp.repeat(logit, HEAD_DIM, axis=1), jnp.repeat(logit, RET_L, axis=1))

    pats = _na_patterns(n // GRID_W)
    y_na = _neighborhood(proj, nvt, cproj, _na_toeplitz(na_rpb[0].astype(F32), pats), pats)

    return _mix_ffn(x, y_ret, y_na, proj, mod4,
                    norm_post_mix[0].reshape(1, d), norm_pre_ffn[0].reshape(1, d),
                    norm_post_ffn[0].reshape(1, d),
                    w_ret_out[0], w_na_out[0], w_o[0], w_ff1[0], w_ff2[0])
```

```python
import functools
import math

import jax
import jax.numpy as jnp
import numpy as np
from jax import lax
from jax.experimental import pallas as pl
from jax.experimental.pallas import tpu as pltpu

F32 = jnp.float32
BF16 = jnp.bfloat16

D_MODEL = 1024
GRID_W = 64
HEADS = 8
HEAD_DIM = 64
MIX_W = HEADS * HEAD_DIM
NA_WIN_H = 8
NA_WIN_W = 16
ROPE_BASE = 10000.0
EPS = 1e-6
NEG_INF = -1e30
K_SCALE = HEAD_DIM ** -0.5
LOG2E = math.log2(math.e)

LANES = 128
V7X_VMEM_BYTES = 64 * 2 ** 20
VMEM_LIMIT = V7X_VMEM_BYTES - 8 * 2 ** 20

HEAD_GROUP = 4
GROUP_W = HEAD_GROUP * HEAD_DIM
N_GROUPS = HEADS // HEAD_GROUP
PROJ_TM = 1024
PROJ_SPLIT = 2
PROJ_STAGE_ROWS = 256
STAGE_SLOTS = 8
MIX_TM = 512
MIX_STAGE_ROWS = 256
FF_CHUNK = 1024
RET_L = 256
NA_ROWS = 4
NA_KROWS = 12
NA_Q = NA_ROWS * GRID_W
NA_K = NA_KROWS * GRID_W
RPB_ROWS = 2 * NA_WIN_H - 1
RPB_COLS = 2 * NA_WIN_W - 1

GATE_COLS = 2 * D_MODEL
COL_RQ, COL_RK, COL_RV, COL_RG, COL_NQ, COL_NK = (GATE_COLS // GROUP_W + i * N_GROUPS for i in range(6))
PROJ_COLS = GATE_COLS + 6 * MIX_W
CCOL_RK, CCOL_RV, CCOL_NK, CCOL_NV = (i * N_GROUPS for i in range(4))

_NT = (((1,), (1,)), ((), ()))
_TN = (((0,), (0,)), ((), ()))


def _cparams(*sem):
    return pltpu.CompilerParams(dimension_semantics=sem, vmem_limit_bytes=VMEM_LIMIT)


def _resident(shape):
    nd = len(shape)
    return pl.BlockSpec(shape, lambda *_: (0,) * nd, pipeline_mode=pl.Buffered(1))


def _stream_cast(tasks, stage_ref, sems):
    slots, chunk, width = stage_ref.shape
    jobs = [task + (r0,) for task in tasks for r0 in range(0, task[0].shape[0], chunk)]

    def copy(j):
        src, src_col, _, _, _, r0 = jobs[j]
        window = src.at[pl.ds(r0, chunk), pl.ds(src_col, width)]
        return pltpu.make_async_copy(window, stage_ref.at[j % slots], sems.at[j % slots])

    for j in range(min(slots - 1, len(jobs))):
        copy(j).start()
    for j in range(len(jobs)):
        if j + slots - 1 < len(jobs):
            copy(j + slots - 1).start()
        copy(j).wait()
        _, _, dst, dst_col, transpose, r0 = jobs[j]
        if transpose:
            dst[pl.ds(dst_col, width), pl.ds(r0, chunk)] = stage_ref[j % slots].T.astype(BF16)
        else:
            dst[pl.ds(r0, chunk), pl.ds(dst_col, width)] = stage_ref[j % slots].astype(BF16)


def _rms(x):
    return x * lax.rsqrt(jnp.mean(x * x, axis=-1, keepdims=True) + EPS)


def _silu(x):
    return x * jax.nn.sigmoid(x)


def _head_masks():
    lane_head = lax.broadcasted_iota(jnp.int32, (1, GROUP_W), 1) // HEAD_DIM
    sel = [lane_head == h for h in range(HEAD_GROUP)]
    return sel, [m.astype(BF16) for m in sel]


def _ada_kernel(c_ref, w_ref, b_ref, o_ref):
    a = _silu(c_ref[...])
    w = w_ref[...]
    a_hi = a.astype(BF16)
    a_lo = (a - a_hi.astype(F32)).astype(BF16)
    w_hi = w.astype(BF16)
    w_lo = (w - w_hi.astype(F32)).astype(BF16)
    rows = a.shape[0]
    top = jnp.dot(jnp.concatenate([a_hi, a_lo], axis=0), w_hi, preferred_element_type=F32)
    o_ref[...] = (top[:rows] + top[rows:] + jnp.dot(a_hi, w_lo, preferred_element_type=F32)) + b_ref[...]


def _ada(c_rows, w_ada, b_ada):
    rows, d = c_rows.shape
    cols = w_ada.shape[1]
    tn = d
    return pl.pallas_call(
        _ada_kernel,
        grid=(cols // tn,),
        in_specs=[pl.BlockSpec((rows, d), lambda j: (0, 0)),
                  pl.BlockSpec((d, tn), lambda j: (0, j)),
                  pl.BlockSpec((1, tn), lambda j: (0, j))],
        out_specs=pl.BlockSpec((rows, tn), lambda j: (0, j)),
        out_shape=jax.ShapeDtypeStruct((rows, cols), F32),
        compiler_params=_cparams("arbitrary"),
        name="ada_mod",
    )(c_rows, w_ada, b_ada.reshape(1, cols))


def _rope_slab(t, cos, sin_lo, sin_hi):
    return (t * cos + pltpu.roll(t, 3 * LANES // 4, axis=1) * sin_lo
            + pltpu.roll(t, LANES // 4, axis=1) * sin_hi)


def _modulated_norm(x_ref, sh_ref, sc_ref, g_ref):
    return ((_rms(x_ref[...]) * g_ref[...]) * (1.0 + sc_ref[...]) + sh_ref[...]).astype(BF16)


def _proj_latent_kernel(x_ref, sh_ref, sc_ref, g_ref, w_hbm, cos_ref, slo_ref, shi_ref,
                        o_ref, vt_ref, w_ref, wvt_ref, stage_ref, sems):
    @pl.when((pl.program_id(0) == 0) & (pl.program_id(1) == 0))
    def _load_weights():
        width = stage_ref.shape[2]
        tasks = [(w_hbm, 7 * MIX_W + off, w_ref, off, False) for off in range(0, GATE_COLS, width)]
        tasks += [(w_hbm, off, w_ref, GATE_COLS + off, False) for off in range(0, 6 * MIX_W, width)]
        tasks += [(w_hbm, 6 * MIX_W + off, wvt_ref, off, True) for off in range(0, MIX_W, width)]
        _stream_cast(tasks, stage_ref, sems)

    rope_lo, rope_hi = COL_RQ * GROUP_W, COL_RV * GROUP_W
    col_scale = {COL_RK * GROUP_W: K_SCALE, COL_NQ * GROUP_W: K_SCALE * LOG2E}
    tm = x_ref.shape[0]
    for r0 in range(0, tm, tm // PROJ_SPLIT):
        rows = slice(r0, r0 + tm // PROJ_SPLIT)
        x = x_ref[rows, :]
        h = ((_rms(x) * g_ref[...]) * (1.0 + sc_ref[...]) + sh_ref[...]).astype(BF16)
        for c0 in range(0, PROJ_COLS, MIX_W):
            r = jnp.dot(h, w_ref[:, c0:c0 + MIX_W], preferred_element_type=F32)
            if rope_lo <= c0 < rope_hi:
                cos, slo, shi = cos_ref[rows, :], slo_ref[rows, :], shi_ref[rows, :]
                r = jnp.concatenate(
                    [_rope_slab(r[:, j:j + LANES], cos, slo, shi) for j in range(0, MIX_W, LANES)], axis=1)
            if c0 in col_scale:
                r = r * col_scale[c0]
            o_ref[rows, c0:c0 + MIX_W] = r.astype(BF16)
        vt_ref[:, rows] = lax.dot_general(wvt_ref[...], h, _NT, preferred_element_type=F32).astype(BF16)


def _proj_latent(x, mod4, g, w_in, rope_tabs):
    b, n, d = x.shape
    tm = PROJ_TM
    mod_spec = lambda k: pl.BlockSpec((None, None, 1, d), lambda i, bb: (bb, k, 0, 0))
    tab_spec = pl.BlockSpec((tm, LANES), lambda i, bb: (i, 0))
    return pl.pallas_call(
        _proj_latent_kernel,
        grid=(n // tm, b),
        in_specs=[pl.BlockSpec((None, tm, d), lambda i, bb: (bb, i, 0)),
                  mod_spec(0), mod_spec(1), _resident((1, d)), pl.BlockSpec(memory_space=pl.ANY),
                  tab_spec, tab_spec, tab_spec],
        out_specs=[pl.BlockSpec((None, tm, PROJ_COLS), lambda i, bb: (bb, i, 0)),
                   pl.BlockSpec((None, MIX_W, tm), lambda i, bb: (bb, 0, i))],
        out_shape=[jax.ShapeDtypeStruct((b, n, PROJ_COLS), BF16),
                   jax.ShapeDtypeStruct((b, MIX_W, n), BF16)],
        scratch_shapes=[pltpu.VMEM((d, PROJ_COLS), BF16), pltpu.VMEM((MIX_W, d), BF16),
                        pltpu.VMEM((STAGE_SLOTS, PROJ_STAGE_ROWS, MIX_W), F32),
                        pltpu.SemaphoreType.DMA((STAGE_SLOTS,))],
        compiler_params=_cparams("arbitrary", "arbitrary"),
        name="proj_latent",
    )(x, mod4, mod4, g, w_in, *rope_tabs)


def _proj_context_kernel(x_ref, sh_ref, sc_ref, g_ref, w_hbm, o_ref, w_ref, stage_ref, sems):
    @pl.when(pl.program_id(0) == 0)
    def _load_weights():
        width = stage_ref.shape[2]
        tasks = [(w_hbm, MIX_W + off, w_ref, off, False) for off in range(0, 2 * MIX_W, width)]
        tasks += [(w_hbm, 5 * MIX_W + off, w_ref, 2 * MIX_W + off, False) for off in range(0, 2 * MIX_W, width)]
        _stream_cast(tasks, stage_ref, sems)

    h = _modulated_norm(x_ref, sh_ref, sc_ref, g_ref)
    for c0 in range(0, o_ref.shape[-1], MIX_W):
        r = jnp.dot(h, w_ref[:, c0:c0 + MIX_W], preferred_element_type=F32)
        if c0 == CCOL_RK * GROUP_W:
            r = r * K_SCALE
        o_ref[:, c0:c0 + MIX_W] = r.astype(BF16)


def _proj_context(ctx_rows, mod4, ctx_row, g, w_in):
    rows, d = ctx_rows.shape
    cols = 4 * MIX_W
    tm = min(PROJ_TM, rows)
    mod_spec = lambda k: pl.BlockSpec((None, None, 1, d), lambda i: (ctx_row, k, 0, 0))
    return pl.pallas_call(
        _proj_context_kernel,
        grid=(rows // tm,),
        in_specs=[pl.BlockSpec((tm, d), lambda i: (i, 0)),
                  mod_spec(0), mod_spec(1), _resident((1, d)), pl.BlockSpec(memory_space=pl.ANY)],
        out_specs=pl.BlockSpec((tm, cols), lambda i: (i, 0)),
        out_shape=jax.ShapeDtypeStruct((rows, cols), BF16),
        scratch_shapes=[pltpu.VMEM((d, cols), BF16), pltpu.VMEM((STAGE_SLOTS, PROJ_STAGE_ROWS, MIX_W), F32),
                        pltpu.SemaphoreType.DMA((STAGE_SLOTS,))],
        compiler_params=_cparams("arbitrary"),
        name="proj_context",
    )(ctx_rows, mod4, mod4, g, w_in)


def _log_sigmoid(x):
    return jnp.minimum(x, 0.0) - jnp.log1p(jnp.exp(-jnp.abs(x)))


def _ret_kernel(q_ref, k_ref, v_ref, rg_ref, ck_ref, cv_ref, lg_lane_ref, lg_chunk_ref, o_ref,
                acc_ref, dec_ref, sf_ref, sb_ref):
    L = RET_L
    n = q_ref.shape[0]
    n_ctx = ck_ref.shape[0]
    nc, ncc = n // L, n_ctx // L

    lg = _log_sigmoid(lg_lane_ref[...])
    lgf, lgb = lg[0:1, :], lg[1:2, :]
    pos = lax.broadcasted_iota(jnp.int32, (L, 1), 0).astype(F32)
    qw_f = jnp.exp(lgf * (pos + 1.0))
    kw_f = jnp.exp(lgf * (L - 1.0 - pos))
    qw_b = jnp.exp(lgb * (L - pos))
    kw_b = jnp.exp(lgb * pos)
    gl_f = jnp.exp(lgf * float(L))
    gl_b = jnp.exp(lgb * float(L))

    lane_head = lax.broadcasted_iota(jnp.int32, (1, GROUP_W), 1) // HEAD_DIM
    row_head = lax.broadcasted_iota(jnp.int32, (GROUP_W, 1), 0) // HEAD_DIM
    block_diag = row_head == lane_head
    head_mean = jnp.where(block_diag, 1.0 / HEAD_DIM, 0.0).astype(BF16)
    _, head_mask = _head_masks()

    lgc = _log_sigmoid(lg_chunk_ref[...])
    diff = pos - lax.broadcasted_iota(jnp.int32, (1, L), 1).astype(F32)
    for h in range(HEAD_GROUP):
        f = jnp.exp(lgc[0:1, h * L:(h + 1) * L] * jnp.maximum(diff, 0.0))
        bk = jnp.exp(lgc[1:2, h * L:(h + 1) * L] * jnp.maximum(-diff, 0.0))
        dec_ref[h * L:(h + 1) * L, :] = jnp.where(diff >= 0.0, f, bk)

    def state_update(s_ref, k, v, kw, gl):
        kd = (k.astype(F32) * kw).astype(BF16)
        contrib = lax.dot_general(kd, v, _TN, preferred_element_type=F32)
        s_ref[...] = s_ref[...] * gl + jnp.where(block_diag, contrib, 0.0)

    sf_ref[...] = jnp.zeros_like(sf_ref)
    sb_ref[...] = jnp.zeros_like(sb_ref)

    def ctx_fwd(c, carry):
        rows = pl.ds(pl.multiple_of(c * L, L), L)
        state_update(sf_ref, ck_ref[rows, :], cv_ref[rows, :], kw_f, gl_f)
        return carry

    def ctx_bwd(i, carry):
        rows = pl.ds(pl.multiple_of((ncc - 1 - i) * L, L), L)
        state_update(sb_ref, ck_ref[rows, :], cv_ref[rows, :], kw_b, gl_b)
        return carry

    lax.fori_loop(0, ncc, ctx_fwd, 0)
    lax.fori_loop(0, ncc, ctx_bwd, 0)

    def fwd_chunk(c, carry):
        rows = pl.ds(pl.multiple_of(c * L, L), L)
        q, k, v = q_ref[rows, :], k_ref[rows, :], v_ref[rows, :]
        acc_ref[rows, :] = jnp.dot(q, sf_ref[...].astype(BF16), preferred_element_type=F32) * qw_f
        state_update(sf_ref, k, v, kw_f, gl_f)
        return carry

    lax.fori_loop(0, nc, fwd_chunk, 0, unroll=16)

    def bwd_chunk(i, carry):
        rows = pl.ds(pl.multiple_of((nc - 1 - i) * L, L), L)
        q, k, v = q_ref[rows, :], k_ref[rows, :], v_ref[rows, :]
        q_heads = jnp.concatenate([q * head_mask[h] for h in range(HEAD_GROUP)], axis=0)
        s = lax.dot_general(q_heads, k, _NT, preferred_element_type=F32)
        s = (s * dec_ref[...]).astype(BF16)
        s_cat = jnp.concatenate([s[h * L:(h + 1) * L, :] for h in range(HEAD_GROUP)], axis=1)
        v_heads = jnp.concatenate([v * head_mask[h] for h in range(HEAD_GROUP)], axis=0)
        o = jnp.dot(s_cat, v_heads, preferred_element_type=F32) + acc_ref[rows, :]
        o = o + jnp.dot(q, sb_ref[...].astype(BF16), preferred_element_type=F32) * qw_b
        sq = o * o
        sq_hi = sq.astype(BF16)
        sq_lo = (sq - sq_hi.astype(F32)).astype(BF16)
        ms = (jnp.dot(sq_hi, head_mean, preferred_element_type=F32)
              + jnp.dot(sq_lo, head_mean, preferred_element_type=F32))
        y = o * lax.rsqrt(ms + EPS) * _silu(rg_ref[rows, :].astype(F32))
        o_ref[rows, :] = y.astype(BF16)
        state_update(sb_ref, k, v, kw_b, gl_b)
        return carry

    lax.fori_loop(0, nc, bwd_chunk, 0, unroll=16)


def _retention(proj, cproj, lg_lane, lg_chunk):
    b, n, _ = proj.shape
    n_ctx = cproj.shape[1]
    L = RET_L
    col = lambda c0: pl.BlockSpec((None, n, GROUP_W), lambda bb, g: (bb, 0, c0 + g))
    ccol = lambda c0: pl.BlockSpec((None, n_ctx, GROUP_W), lambda bb, g: (bb, 0, c0 + g))
    return pl.pallas_call(
        _ret_kernel,
        grid=(b, N_GROUPS),
        in_specs=[col(COL_RQ), col(COL_RK), col(COL_RV), col(COL_RG), ccol(CCOL_RK), ccol(CCOL_RV),
                  pl.BlockSpec((2, GROUP_W), lambda bb, g: (0, g)),
                  pl.BlockSpec((2, HEAD_GROUP * L), lambda bb, g: (0, g))],
        out_specs=pl.BlockSpec((None, n, GROUP_W), lambda bb, g: (bb, 0, g)),
        out_shape=jax.ShapeDtypeStruct((b, n, MIX_W), BF16),
        scratch_shapes=[pltpu.VMEM((n, GROUP_W), F32),
                        pltpu.VMEM((HEAD_GROUP * L, L), F32),
                        pltpu.VMEM((GROUP_W, GROUP_W), F32),
                        pltpu.VMEM((GROUP_W, GROUP_W), F32)],
        compiler_params=_cparams("arbitrary", "arbitrary"),
        name="retention",
    )(proj, proj, proj, proj, cproj, cproj, lg_lane, lg_chunk)


def _na_patterns(rows):
    nblk = rows // NA_ROWS
    assert rows % NA_ROWS == 0 and nblk >= 3 and rows >= NA_KROWS
    pats = []
    for qb in (0, 1, nblk - 1):
        kb = _na_key_start(qb, rows)
        qr = qb * NA_ROWS + np.arange(NA_ROWS)
        r0 = np.clip(qr - NA_WIN_H // 2, 0, rows - NA_WIN_H)
        assert np.all(r0 >= kb) and np.all(r0 + NA_WIN_H <= kb + NA_KROWS)
        pats.append((tuple(int(v) for v in r0 - kb), int(kb - qb * NA_ROWS + NA_WIN_H - 1)))
    for qb in range(1, nblk - 1):
        kb = _na_key_start(qb, rows)
        qr = qb * NA_ROWS + np.arange(NA_ROWS)
        r0 = np.clip(qr - NA_WIN_H // 2, 0, rows - NA_WIN_H)
        assert (tuple(int(v) for v in r0 - kb), int(kb - qb * NA_ROWS + NA_WIN_H - 1)) == pats[1]
    return pats


def _na_key_start(qb, rows):
    assert (NA_WIN_H // 2) % NA_ROWS == 0 and NA_KROWS % NA_ROWS == 0
    return min(max(qb * NA_ROWS - NA_WIN_H // 2, 0), rows - NA_KROWS)


def _na_slot_range(pats):
    lo = min(off - (NA_ROWS - 1) for _, off in pats)
    hi = max(off + NA_KROWS - 1 for _, off in pats)
    return lo, hi


def _na_kernel(q_ref, k_ref, vt_ref, ck_ref, cv_ref, tz_ref, o_ref,
               bias_ref, cvt_ref, s0_ref, s1_ref, m0_ref, m1_ref, *, pats):
    n = q_ref.shape[0]
    n_ctx = ck_ref.shape[0]
    rows = n // GRID_W
    nblk = rows // NA_ROWS
    _, head_mask = _head_masks()
    _, dr_hi = _na_slot_range(pats)
    n_keys = NA_K + n_ctx

    @pl.when(pl.program_id(1) == 0)
    def _build_bias():
        key = lax.broadcasted_iota(jnp.int32, (NA_K, NA_Q), 0)
        qry = lax.broadcasted_iota(jnp.int32, (NA_K, NA_Q), 1)
        j, kc = key // GRID_W, key % GRID_W
        i, qc = qry // GRID_W, qry % GRID_W
        c0 = jnp.clip(qc - NA_WIN_W // 2, 0, GRID_W - NA_WIN_W)
        col_ok = (kc >= c0) & (kc < c0 + NA_WIN_W)
        for p, (r0, off) in enumerate(pats):
            lo = jnp.zeros_like(i)
            for ii in range(NA_ROWS):
                lo = jnp.where(i == ii, r0[ii], lo)
            ok = col_ok & (j >= lo) & (j < lo + NA_WIN_H)
            for h in range(HEAD_GROUP):
                t = tz_ref[h]
                for jj in range(NA_KROWS):
                    slot = dr_hi - (jj + off)
                    tile = t[:, slot * GRID_W:slot * GRID_W + NA_Q] * LOG2E
                    rs = slice(jj * GRID_W, (jj + 1) * GRID_W)
                    bias_ref[p, rs, h * NA_Q:(h + 1) * NA_Q] = jnp.where(ok[rs, :], tile, NEG_INF)

    cvt_ref[...] = cv_ref[...].astype(F32).T.astype(BF16)
    ones_rows = jnp.ones((2 * 8, n_keys), BF16)

    def block_rows(qb):
        kb = jnp.clip(qb * NA_ROWS - NA_WIN_H // 2, 0, rows - NA_KROWS)
        q_rows = pl.ds(pl.multiple_of(qb * NA_Q, NA_Q), NA_Q)
        k_tok = pl.ds(pl.multiple_of(kb * GRID_W, NA_Q), NA_K)
        return q_rows, k_tok

    def scores(qb, s_ref, m_ref):
        q_rows, k_tok = block_rows(qb)
        pat = jnp.where(qb == 0, 0, jnp.where(qb == nblk - 1, 2, 1))
        q = q_ref[q_rows, :]
        q_heads = jnp.concatenate([q * head_mask[h] for h in range(HEAD_GROUP)], axis=0)
        s_ref[0:NA_K, :] = (lax.dot_general(k_ref[k_tok, :], q_heads, _NT, preferred_element_type=F32)
                            + bias_ref[pat])
        s_ref[NA_K:n_keys, :] = lax.dot_general(ck_ref[...], q_heads, _NT, preferred_element_type=F32)
        m_ref[...] = jnp.max(s_ref[...], axis=0, keepdims=True)

    def attend(qb, s_ref, m_ref):
        q_rows, k_tok = block_rows(qb)
        outs = []
        half = HEAD_GROUP // 2 * NA_Q
        for c0 in range(0, HEAD_GROUP * NA_Q, half):
            p = jnp.exp2(s_ref[:, c0:c0 + half] - m_ref[:, c0:c0 + half]).astype(BF16)
            for h in range(c0 // NA_Q, (c0 + half) // NA_Q):
                hs = slice(h * HEAD_DIM, (h + 1) * HEAD_DIM)
                v_aug = jnp.concatenate(
                    [jnp.concatenate([vt_ref[hs, k_tok], cvt_ref[hs, :]], axis=1), ones_rows], axis=0)
                oh = jnp.dot(v_aug, p[:, h * NA_Q - c0:(h + 1) * NA_Q - c0], preferred_element_type=F32)
                outs.append(oh[0:HEAD_DIM, :] / oh[HEAD_DIM:HEAD_DIM + 1, :])
        o_ref[q_rows, :] = jnp.concatenate(outs, axis=0).T.astype(BF16)

    assert nblk % 2 == 0
    scores(0, s0_ref, m0_ref)

    def pair(t, carry):
        qb = 2 * t
        scores(qb + 1, s1_ref, m1_ref)
        attend(qb, s0_ref, m0_ref)
        scores(jnp.minimum(qb + 2, nblk - 1), s0_ref, m0_ref)
        attend(qb + 1, s1_ref, m1_ref)
        return carry

    lax.fori_loop(0, nblk // 2, pair, 0, unroll=4)


def _na_toeplitz(rpb, pats):
    dr_lo, dr_hi = _na_slot_range(pats)
    slots = dr_hi - dr_lo + 1
    slots += (-slots) % (LANES // GRID_W)
    row_sel = np.zeros((RPB_ROWS, slots), np.float32)
    for s in range(slots):
        if 0 <= dr_hi - s < RPB_ROWS:
            row_sel[dr_hi - s, s] = 1.0
    col = np.arange(GRID_W)
    dc = col[:, None] - col[None, :] + NA_WIN_W - 1
    col_sel = (dc[None, :, :] == np.arange(RPB_COLS)[:, None, None]).astype(np.float32)
    t = jnp.einsum('hab,as,bkq->hksq', rpb, row_sel, col_sel, precision=lax.Precision.HIGHEST)
    return t.reshape(rpb.shape[0], GRID_W, slots * GRID_W)


def _neighborhood(proj, vt, cproj, tz, pats):
    b, n, _ = proj.shape
    n_ctx = cproj.shape[1]
    col = lambda c0: pl.BlockSpec((None, n, GROUP_W), lambda g, bb: (bb, 0, c0 + g))
    ccol = lambda c0: pl.BlockSpec((None, n_ctx, GROUP_W), lambda g, bb: (bb, 0, c0 + g))
    return pl.pallas_call(
        functools.partial(_na_kernel, pats=pats),
        grid=(N_GROUPS, b),
        in_specs=[col(COL_NQ), col(COL_NK),
                  pl.BlockSpec((None, GROUP_W, n), lambda g, bb: (bb, g, 0)),
                  ccol(CCOL_NK), ccol(CCOL_NV),
                  pl.BlockSpec((HEAD_GROUP,) + tz.shape[1:], lambda g, bb: (g, 0, 0))],
        out_specs=pl.BlockSpec((None, n, GROUP_W), lambda g, bb: (bb, 0, g)),
        out_shape=jax.ShapeDtypeStruct((b, n, MIX_W), BF16),
        scratch_shapes=[pltpu.VMEM((len(pats), NA_K, HEAD_GROUP * NA_Q), F32),
                        pltpu.VMEM((GROUP_W, n_ctx), BF16),
                        pltpu.VMEM((NA_K + n_ctx, HEAD_GROUP * NA_Q), F32),
                        pltpu.VMEM((NA_K + n_ctx, HEAD_GROUP * NA_Q), F32),
                        pltpu.VMEM((1, HEAD_GROUP * NA_Q), F32),
                        pltpu.VMEM((1, HEAD_GROUP * NA_Q), F32)],
        compiler_params=_cparams("arbitrary", "arbitrary"),
        name="neighborhood",
    )(proj, proj, vt, cproj, cproj, tz)


def _mix_ffn_kernel(x_ref, yr_ref, yn_ref, gr_ref, gn_ref, gt1_ref, sh2_ref, sc2_ref, gt2_ref,
                    g_post_mix_ref, g_pre_ffn_ref, g_post_ffn_ref,
                    w_ret_hbm, w_na_hbm, w_o_hbm, w1_hbm, w2_hbm, o_ref,
                    w_ret_ref, w_na_ref, w_o_ref, w1_ref, w2_ref, stage_ref, sems):
    @pl.when((pl.program_id(0) == 0) & (pl.program_id(1) == 0))
    def _load_weights():
        width = stage_ref.shape[2]
        pairs = ((w_ret_hbm, w_ret_ref), (w_na_hbm, w_na_ref), (w_o_hbm, w_o_ref),
                 (w1_hbm, w1_ref), (w2_hbm, w2_ref))
        tasks = [(src, off, dst, off, False) for src, dst in pairs for off in range(0, src.shape[1], width)]
        _stream_cast(tasks, stage_ref, sems)

    tm = x_ref.shape[0]
    d_ff = w1_ref.shape[1]
    a = jnp.dot(yr_ref[...], w_ret_ref[...], preferred_element_type=F32)
    bb = jnp.dot(yn_ref[...], w_na_ref[...], preferred_element_type=F32)
    y = (jax.nn.sigmoid(gr_ref[...].astype(F32)) * a + jax.nn.sigmoid(gn_ref[...].astype(F32)) * bb)
    y = jnp.dot(y.astype(BF16), w_o_ref[...], preferred_element_type=F32)
    x1 = x_ref[...] + gt1_ref[...] * (_rms(y) * g_post_mix_ref[...])
    h2 = ((_rms(x1) * g_pre_ffn_ref[...]) * (1.0 + sc2_ref[...]) + sh2_ref[...]).astype(BF16)
    f = jnp.zeros((tm, x_ref.shape[1]), F32)
    for c0 in range(0, d_ff, FF_CHUNK):
        u = jnp.dot(h2, w1_ref[:, c0:c0 + FF_CHUNK], preferred_element_type=F32)
        u = jnp.square(jnp.maximum(u, 0.0)).astype(BF16)
        f = f + jnp.dot(u, w2_ref[c0:c0 + FF_CHUNK, :], preferred_element_type=F32)
    o_ref[...] = x1 + gt2_ref[...] * (_rms(f) * g_post_ffn_ref[...])


def _mix_ffn(x, y_ret, y_na, proj, mod4, g_post_mix, g_pre_ffn, g_post_ffn,
             w_ret, w_na, w_o, w1, w2):
    b, n, d = x.shape
    tm = MIX_TM
    tok = lambda w: pl.BlockSpec((None, tm, w), lambda i, bb: (bb, i, 0))
    mod_spec = lambda k: pl.BlockSpec((None, None, 1, d), lambda i, bb: (bb, k, 0, 0))
    return pl.pallas_call(
        _mix_ffn_kernel,
        grid=(n // tm, b),
        in_specs=[tok(d), tok(MIX_W), tok(MIX_W),
                  pl.BlockSpec((None, tm, d), lambda i, bb: (bb, i, 0)),
                  pl.BlockSpec((None, tm, d), lambda i, bb: (bb, i, 1)),
                  mod_spec(2), mod_spec(3), mod_spec(4), mod_spec(5),
                  _resident((1, d)), _resident((1, d)), _resident((1, d))]
                 + [pl.BlockSpec(memory_space=pl.ANY)] * 5,
        out_specs=tok(d),
        out_shape=jax.ShapeDtypeStruct((b, n, d), F32),
        scratch_shapes=[pltpu.VMEM(w.shape, BF16) for w in (w_ret, w_na, w_o, w1, w2)]
                       + [pltpu.VMEM((STAGE_SLOTS, MIX_STAGE_ROWS, d), F32),
                          pltpu.SemaphoreType.DMA((STAGE_SLOTS,))],
        compiler_params=_cparams("arbitrary", "arbitrary"),
        name="mix_ffn",
    )(x, y_ret, y_na, proj, proj, mod4, mod4, mod4, mod4, g_post_mix, g_pre_ffn, g_post_ffn,
      w_ret, w_na, w_o, w1, w2)


def _rope_tables(n):
    pos = np.arange(n)
    row = (pos // GRID_W).astype(np.float64)
    colp = (pos % GRID_W).astype(np.float64)
    d_axis = HEAD_DIM // 2
    inv = ROPE_BASE ** (-np.arange(0, d_axis, 2, dtype=np.float64) / d_axis)
    ang = np.concatenate([row[:, None] * inv, colp[:, None] * inv], axis=-1)
    cos, sin = np.cos(ang), np.sin(ang)
    zero = np.zeros_like(sin)
    reps = LANES // HEAD_DIM
    cos_t = np.tile(np.concatenate([cos, cos], axis=-1), (1, reps))
    sin_lo = np.tile(np.concatenate([-sin, zero], axis=-1), (1, reps))
    sin_hi = np.tile(np.concatenate([zero, sin], axis=-1), (1, reps))
    return tuple(jnp.asarray(t, F32) for t in (cos_t, sin_lo, sin_hi))


def kernel(x, c, ctx, c_ctx, w_ada, b_ada, norm_pre_mix, norm_post_mix, norm_pre_ffn, norm_post_ffn,
           w_in, ret_decay_logit, w_ret_out, na_rpb, w_na_out, w_o, w_ff1, w_ff2):
    b, n, d = x.shape
    n_ctx = ctx.shape[1]
    assert w_ada.shape[0] == 1, "single-layer block"
    assert n % NA_Q == 0 and n % PROJ_TM == 0 and n % MIX_TM == 0 and n % RET_L == 0
    assert n_ctx % RET_L == 0 and d == D_MODEL and 7 * MIX_W + GATE_COLS == w_in.shape[2]

    pad = (-(b + 1)) % 8
    c_rows = jnp.concatenate([c, c_ctx[None, :], jnp.zeros((pad, d), F32)], axis=0)
    mod = _ada(c_rows, w_ada[0], b_ada[0])
    mod4 = mod.reshape(c_rows.shape[0], 6, 1, d)

    g_pre_mix = norm_pre_mix[0].reshape(1, d)
    proj, nvt = _proj_latent(x, mod4, g_pre_mix, w_in[0], _rope_tables(n))
    cproj = _proj_context(ctx.reshape(b * n_ctx, d), mod4, b, g_pre_mix, w_in[0])
    cproj = cproj.reshape(b, n_ctx, 4 * MIX_W)

    logit = ret_decay_logit[0].astype(F32)
    y_ret = _retention(proj, cproj, jnp.repeat(logit, HEAD_DIM, axis=1), jnp.repeat(logit, RET_L, axis=1))

    pats = _na_patterns(n // GRID_W)
    y_na = _neighborhood(proj, nvt, cproj, _na_toeplitz(na_rpb[0].astype(F32), pats), pats)

    return _mix_ffn(x, y_ret, y_na, proj, mod4,
                    norm_post_mix[0].reshape(1, d), norm_pre_ffn[0].reshape(1, d),
                    norm_post_ffn[0].reshape(1, d),
                    w_ret_out[0], w_na_out[0], w_o[0], w_ff1[0], w_ff2[0])
```

```python
import functools
import math

import jax
import jax.numpy as jnp
import numpy as np
from jax import lax
from jax.experimental import pallas as pl
from jax.experimental.pallas import tpu as pltpu

F32 = jnp.float32
BF16 = jnp.bfloat16

D_MODEL = 1024
GRID_W = 64
HEADS = 8
HEAD_DIM = 64
MIX_W = HEADS * HEAD_DIM
NA_WIN_H = 8
NA_WIN_W = 16
ROPE_BASE = 10000.0
EPS = 1e-6
NEG_INF = -1e30
K_SCALE = HEAD_DIM ** -0.5
LOG2E = math.log2(math.e)

LANES = 128
V7X_VMEM_BYTES = 64 * 2 ** 20
VMEM_LIMIT = V7X_VMEM_BYTES - 8 * 2 ** 20

HEAD_GROUP = 4
GROUP_W = HEAD_GROUP * HEAD_DIM
N_GROUPS = HEADS // HEAD_GROUP
PROJ_TM = 1024
PROJ_SPLIT = 2
PROJ_STAGE_ROWS = 256
STAGE_SLOTS = 8
MIX_TM = 512
MIX_STAGE_ROWS = 256
FF_CHUNK = 1024
RET_L = 256
NA_ROWS = 4
NA_KROWS = 12
NA_Q = NA_ROWS * GRID_W
NA_K = NA_KROWS * GRID_W
RPB_ROWS = 2 * NA_WIN_H - 1
RPB_COLS = 2 * NA_WIN_W - 1

GATE_COLS = 2 * D_MODEL
COL_RQ, COL_RK, COL_RV, COL_RG, COL_NQ, COL_NK = (GATE_COLS // GROUP_W + i * N_GROUPS for i in range(6))
PROJ_COLS = GATE_COLS + 6 * MIX_W
CCOL_RK, CCOL_RV, CCOL_NK, CCOL_NV = (i * N_GROUPS for i in range(4))

_NT = (((1,), (1,)), ((), ()))
_TN = (((0,), (0,)), ((), ()))


def _cparams(*sem):
    return pltpu.CompilerParams(dimension_semantics=sem, vmem_limit_bytes=VMEM_LIMIT)


def _resident(shape):
    nd = len(shape)
    return pl.BlockSpec(shape, lambda *_: (0,) * nd, pipeline_mode=pl.Buffered(1))


def _stream_cast(tasks, stage_ref, sems):
    slots, chunk, width = stage_ref.shape
    jobs = [task + (r0,) for task in tasks for r0 in range(0, task[0].shape[0], chunk)]

    def copy(j):
        src, src_col, _, _, _, r0 = jobs[j]
        window = src.at[pl.ds(r0, chunk), pl.ds(src_col, width)]
        return pltpu.make_async_copy(window, stage_ref.at[j % slots], sems.at[j % slots])

    for j in range(min(slots - 1, len(jobs))):
        copy(j).start()
    for j in range(len(jobs)):
        if j + slots - 1 < len(jobs):
            copy(j + slots - 1).start()
        copy(j).wait()
        _, _, dst, dst_col, transpose, r0 = jobs[j]
        if transpose:
            dst[pl.ds(dst_col, width), pl.ds(r0, chunk)] = stage_ref[j % slots].T.astype(BF16)
        else:
            dst[pl.ds(r0, chunk), pl.ds(dst_col, width)] = stage_ref[j % slots].astype(BF16)


def _rms(x):
    return x * lax.rsqrt(jnp.mean(x * x, axis=-1, keepdims=True) + EPS)


def _silu(x):
    return x * jax.nn.sigmoid(x)


def _head_masks():
    lane_head = lax.broadcasted_iota(jnp.int32, (1, GROUP_W), 1) // HEAD_DIM
    sel = [lane_head == h for h in range(HEAD_GROUP)]
    return sel, [m.astype(BF16) for m in sel]


def _ada_kernel(c_ref, w_ref, b_ref, o_ref):
    a = _silu(c_ref[...])
    w = w_ref[...]
    a_hi = a.astype(BF16)
    a_lo = (a - a_hi.astype(F32)).astype(BF16)
    w_hi = w.astype(BF16)
    w_lo = (w - w_hi.astype(F32)).astype(BF16)
    rows = a.shape[0]
    top = jnp.dot(jnp.concatenate([a_hi, a_lo], axis=0), w_hi, preferred_element_type=F32)
    o_ref[...] = (top[:rows] + top[rows:] + jnp.dot(a_hi, w_lo, preferred_element_type=F32)) + b_ref[...]


def _ada(c_rows, w_ada, b_ada):
    rows, d = c_rows.shape
    cols = w_ada.shape[1]
    tn = d
    return pl.pallas_call(
        _ada_kernel,
        grid=(cols // tn,),
        in_specs=[pl.BlockSpec((rows, d), lambda j: (0, 0)),
                  pl.BlockSpec((d, tn), lambda j: (0, j)),
                  pl.BlockSpec((1, tn), lambda j: (0, j))],
        out_specs=pl.BlockSpec((rows, tn), lambda j: (0, j)),
        out_shape=jax.ShapeDtypeStruct((rows, cols), F32),
        compiler_params=_cparams("arbitrary"),
        name="ada_mod",
    )(c_rows, w_ada, b_ada.reshape(1, cols))


def _rope_slab(t, cos, sin_lo, sin_hi):
    return (t * cos + pltpu.roll(t, 3 * LANES // 4, axis=1) * sin_lo
            + pltpu.roll(t, LANES // 4, axis=1) * sin_hi)


def _modulated_norm(x_ref, sh_ref, sc_ref, g_ref):
    return ((_rms(x_ref[...]) * g_ref[...]) * (1.0 + sc_ref[...]) + sh_ref[...]).astype(BF16)


def _proj_latent_kernel(x_ref, sh_ref, sc_ref, g_ref, w_hbm, cos_ref, slo_ref, shi_ref,
                        o_ref, vt_ref, w_ref, wvt_ref, stage_ref, sems):
    @pl.when((pl.program_id(0) == 0) & (pl.program_id(1) == 0))
    def _load_weights():
        width = stage_ref.shape[2]
        tasks = [(w_hbm, 7 * MIX_W + off, w_ref, off, False) for off in range(0, GATE_COLS, width)]
        tasks += [(w_hbm, off, w_ref, GATE_COLS + off, False) for off in range(0, 6 * MIX_W, width)]
        tasks += [(w_hbm, 6 * MIX_W + off, wvt_ref, off, True) for off in range(0, MIX_W, width)]
        _stream_cast(tasks, stage_ref, sems)

    rope_lo, rope_hi = COL_RQ * GROUP_W, COL_RV * GROUP_W
    col_scale = {COL_RK * GROUP_W: K_SCALE, COL_NQ * GROUP_W: K_SCALE * LOG2E}
    tm = x_ref.shape[0]
    for r0 in range(0, tm, tm // PROJ_SPLIT):
        rows = slice(r0, r0 + tm // PROJ_SPLIT)
        x = x_ref[rows, :]
        h = ((_rms(x) * g_ref[...]) * (1.0 + sc_ref[...]) + sh_ref[...]).astype(BF16)
        for c0 in range(0, PROJ_COLS, MIX_W):
            r = jnp.dot(h, w_ref[:, c0:c0 + MIX_W], preferred_element_type=F32)
            if rope_lo <= c0 < rope_hi:
                cos, slo, shi = cos_ref[rows, :], slo_ref[rows, :], shi_ref[rows, :]
                r = jnp.concatenate(
                    [_rope_slab(r[:, j:j + LANES], cos, slo, shi) for j in range(0, MIX_W, LANES)], axis=1)
            if c0 in col_scale:
                r = r * col_scale[c0]
            o_ref[rows, c0:c0 + MIX_W] = r.astype(BF16)
        vt_ref[:, rows] = lax.dot_general(wvt_ref[...], h, _NT, preferred_element_type=F32).astype(BF16)


def _proj_latent(x, mod4, g, w_in, rope_tabs):
    b, n, d = x.shape
    tm = PROJ_TM
    mod_spec = lambda k: pl.BlockSpec((None, None, 1, d), lambda i, bb: (bb, k, 0, 0))
    tab_spec = pl.BlockSpec((tm, LANES), lambda i, bb: (i, 0))
    return pl.pallas_call(
        _proj_latent_kernel,
        grid=(n // tm, b),
        in_specs=[pl.BlockSpec((None, tm, d), lambda i, bb: (bb, i, 0)),
                  mod_spec(0), mod_spec(1), _resident((1, d)), pl.BlockSpec(memory_space=pl.ANY),
                  tab_spec, tab_spec, tab_spec],
        out_specs=[pl.BlockSpec((None, tm, PROJ_COLS), lambda i, bb: (bb, i, 0)),
                   pl.BlockSpec((None, MIX_W, tm), lambda i, bb: (bb, 0, i))],
        out_shape=[jax.ShapeDtypeStruct((b, n, PROJ_COLS), BF16),
                   jax.ShapeDtypeStruct((b, MIX_W, n), BF16)],
        scratch_shapes=[pltpu.VMEM((d, PROJ_COLS), BF16), pltpu.VMEM((MIX_W, d), BF16),
                        pltpu.VMEM((STAGE_SLOTS, PROJ_STAGE_ROWS, MIX_W), F32),
                        pltpu.SemaphoreType.DMA((STAGE_SLOTS,))],
        compiler_params=_cparams("arbitrary", "arbitrary"),
        name="proj_latent",
    )(x, mod4, mod4, g, w_in, *rope_tabs)


def _proj_context_kernel(x_ref, sh_ref, sc_ref, g_ref, w_hbm, o_ref, w_ref, stage_ref, sems):
    @pl.when(pl.program_id(0) == 0)
    def _load_weights():
        width = stage_ref.shape[2]
        tasks = [(w_hbm, MIX_W + off, w_ref, off, False) for off in range(0, 2 * MIX_W, width)]
        tasks += [(w_hbm, 5 * MIX_W + off, w_ref, 2 * MIX_W + off, False) for off in range(0, 2 * MIX_W, width)]
        _stream_cast(tasks, stage_ref, sems)

    h = _modulated_norm(x_ref, sh_ref, sc_ref, g_ref)
    for c0 in range(0, o_ref.shape[-1], MIX_W):
        r = jnp.dot(h, w_ref[:, c0:c0 + MIX_W], preferred_element_type=F32)
        if c0 == CCOL_RK * GROUP_W:
            r = r * K_SCALE
        o_ref[:, c0:c0 + MIX_W] = r.astype(BF16)


def _proj_context(ctx_rows, mod4, ctx_row, g, w_in):
    rows, d = ctx_rows.shape
    cols = 4 * MIX_W
    tm = min(PROJ_TM, rows)
    mod_spec = lambda k: pl.BlockSpec((None, None, 1, d), lambda i: (ctx_row, k, 0, 0))
    return pl.pallas_call(
        _proj_context_kernel,
        grid=(rows // tm,),
        in_specs=[pl.BlockSpec((tm, d), lambda i: (i, 0)),
                  mod_spec(0), mod_spec(1), _resident((1, d)), pl.BlockSpec(memory_space=pl.ANY)],
        out_specs=pl.BlockSpec((tm, cols), lambda i: (i, 0)),
        out_shape=jax.ShapeDtypeStruct((rows, cols), BF16),
        scratch_shapes=[pltpu.VMEM((d, cols), BF16), pltpu.VMEM((STAGE_SLOTS, PROJ_STAGE_ROWS, MIX_W), F32),
                        pltpu.SemaphoreType.DMA((STAGE_SLOTS,))],
        compiler_params=_cparams("arbitrary"),
        name="proj_context",
    )(ctx_rows, mod4, mod4, g, w_in)


def _log_sigmoid(x):
    return jnp.minimum(x, 0.0) - jnp.log1p(jnp.exp(-jnp.abs(x)))


def _ret_kernel(q_ref, k_ref, v_ref, rg_ref, ck_ref, cv_ref, lg_lane_ref, lg_chunk_ref, o_ref,
                acc_ref, dec_ref, sf_ref, sb_ref):
    L = RET_L
    n = q_ref.shape[0]
    n_ctx = ck_ref.shape[0]
    nc, ncc = n // L, n_ctx // L

    lg = _log_sigmoid(lg_lane_ref[...])
    lgf, lgb = lg[0:1, :], lg[1:2, :]
    pos = lax.broadcasted_iota(jnp.int32, (L, 1), 0).astype(F32)
    qw_f = jnp.exp(lgf * (pos + 1.0))
    kw_f = jnp.exp(lgf * (L - 1.0 - pos))
    qw_b = jnp.exp(lgb * (L - pos))
    kw_b = jnp.exp(lgb * pos)
    gl_f = jnp.exp(lgf * float(L))
    gl_b = jnp.exp(lgb * float(L))

    lane_head = lax.broadcasted_iota(jnp.int32, (1, GROUP_W), 1) // HEAD_DIM
    row_head = lax.broadcasted_iota(jnp.int32, (GROUP_W, 1), 0) // HEAD_DIM
    block_diag = row_head == lane_head
    head_mean = jnp.where(block_diag, 1.0 / HEAD_DIM, 0.0).astype(BF16)
    _, head_mask = _head_masks()

    lgc = _log_sigmoid(lg_chunk_ref[...])
    diff = pos - lax.broadcasted_iota(jnp.int32, (1, L), 1).astype(F32)
    for h in range(HEAD_GROUP):
        f = jnp.exp(lgc[0:1, h * L:(h + 1) * L] * jnp.maximum(diff, 0.0))
        bk = jnp.exp(lgc[1:2, h * L:(h + 1) * L] * jnp.maximum(-diff, 0.0))
        dec_ref[h * L:(h + 1) * L, :] = jnp.where(diff >= 0.0, f, bk)

    def state_update(s_ref, k, v, kw, gl):
        kd = (k.astype(F32) * kw).astype(BF16)
        contrib = lax.dot_general(kd, v, _TN, preferred_element_type=F32)
        s_ref[...] = s_ref[...] * gl + jnp.where(block_diag, contrib, 0.0)

    sf_ref[...] = jnp.zeros_like(sf_ref)
    sb_ref[...] = jnp.zeros_like(sb_ref)

    def ctx_fwd(c, carry):
        rows = pl.ds(pl.multiple_of(c * L, L), L)
        state_update(sf_ref, ck_ref[rows, :], cv_ref[rows, :], kw_f, gl_f)
        return carry

    def ctx_bwd(i, carry):
        rows = pl.ds(pl.multiple_of((ncc - 1 - i) * L, L), L)
        state_update(sb_ref, ck_ref[rows, :], cv_ref[rows, :], kw_b, gl_b)
        return carry

    lax.fori_loop(0, ncc, ctx_fwd, 0)
    lax.fori_loop(0, ncc, ctx_bwd, 0)

    def fwd_chunk(c, carry):
        rows = pl.ds(pl.multiple_of(c * L, L), L)
        q, k, v = q_ref[rows, :], k_ref[rows, :], v_ref[rows, :]
        acc_ref[rows, :] = jnp.dot(q, sf_ref[...].astype(BF16), preferred_element_type=F32) * qw_f
        state_update(sf_ref, k, v, kw_f, gl_f)
        return carry

    lax.fori_loop(0, nc, fwd_chunk, 0, unroll=16)

    def bwd_chunk(i, carry):
        rows = pl.ds(pl.multiple_of((nc - 1 - i) * L, L), L)
        q, k, v = q_ref[rows, :], k_ref[rows, :], v_ref[rows, :]
        q_heads = jnp.concatenate([q * head_mask[h] for h in range(HEAD_GROUP)], axis=0)
        s = lax.dot_general(q_heads, k, _NT, preferred_element_type=F32)
        s = (s * dec_ref[...]).astype(BF16)
        s_cat = jnp.concatenate([s[h * L:(h + 1) * L, :] for h in range(HEAD_GROUP)], axis=1)
        v_heads = jnp.concatenate([v * head_mask[h] for h in range(HEAD_GROUP)], axis=0)
        o = jnp.dot(s_cat, v_heads, preferred_element_type=F32) + acc_ref[rows, :]
        o = o + jnp.dot(q, sb_ref[...].astype(BF16), preferred_element_type=F32) * qw_b
        sq = o * o
        sq_hi = sq.astype(BF16)
        sq_lo = (sq - sq_hi.astype(F32)).astype(BF16)
        ms = (jnp.dot(sq_hi, head_mean, preferred_element_type=F32)
              + jnp.dot(sq_lo, head_mean, preferred_element_type=F32))
        y = o * lax.rsqrt(ms + EPS) * _silu(rg_ref[rows, :].astype(F32))
        o_ref[rows, :] = y.astype(BF16)
        state_update(sb_ref, k, v, kw_b, gl_b)
        return carry

    lax.fori_loop(0, nc, bwd_chunk, 0, unroll=16)


def _retention(proj, cproj, lg_lane, lg_chunk):
    b, n, _ = proj.shape
    n_ctx = cproj.shape[1]
    L = RET_L
    col = lambda c0: pl.BlockSpec((None, n, GROUP_W), lambda bb, g: (bb, 0, c0 + g))
    ccol = lambda c0: pl.BlockSpec((None, n_ctx, GROUP_W), lambda bb, g: (bb, 0, c0 + g))
    return pl.pallas_call(
        _ret_kernel,
        grid=(b, N_GROUPS),
        in_specs=[col(COL_RQ), col(COL_RK), col(COL_RV), col(COL_RG), ccol(CCOL_RK), ccol(CCOL_RV),
                  pl.BlockSpec((2, GROUP_W), lambda bb, g: (0, g)),
                  pl.BlockSpec((2, HEAD_GROUP * L), lambda bb, g: (0, g))],
        out_specs=pl.BlockSpec((None, n, GROUP_W), lambda bb, g: (bb, 0, g)),
        out_shape=jax.ShapeDtypeStruct((b, n, MIX_W), BF16),
        scratch_shapes=[pltpu.VMEM((n, GROUP_W), F32),
                        pltpu.VMEM((HEAD_GROUP * L, L), F32),
                        pltpu.VMEM((GROUP_W, GROUP_W), F32),
                        pltpu.VMEM((GROUP_W, GROUP_W), F32)],
        compiler_params=_cparams("arbitrary", "arbitrary"),
        name="retention",
    )(proj, proj, proj, proj, cproj, cproj, lg_lane, lg_chunk)


def _na_patterns(rows):
    nblk = rows // NA_ROWS
    assert rows % NA_ROWS == 0 and nblk >= 3 and rows >= NA_KROWS
    pats = []
    for qb in (0, 1, nblk - 1):
        kb = _na_key_start(qb, rows)
        qr = qb * NA_ROWS + np.arange(NA_ROWS)
        r0 = np.clip(qr - NA_WIN_H // 2, 0, rows - NA_WIN_H)
        assert np.all(r0 >= kb) and np.all(r0 + NA_WIN_H <= kb + NA_KROWS)
        pats.append((tuple(int(v) for v in r0 - kb), int(kb - qb * NA_ROWS + NA_WIN_H - 1)))
    for qb in range(1, nblk - 1):
        kb = _na_key_start(qb, rows)
        qr = qb * NA_ROWS + np.arange(NA_ROWS)
        r0 = np.clip(qr - NA_WIN_H // 2, 0, rows - NA_WIN_H)
        assert (tuple(int(v) for v in r0 - kb), int(kb - qb * NA_ROWS + NA_WIN_H - 1)) == pats[1]
    return pats


def _na_key_start(qb, rows):
    assert (NA_WIN_H // 2) % NA_ROWS == 0 and NA_KROWS % NA_ROWS == 0
    return min(max(qb * NA_ROWS - NA_WIN_H // 2, 0), rows - NA_KROWS)


def _na_slot_range(pats):
    lo = min(off - (NA_ROWS - 1) for _, off in pats)
    hi = max(off + NA_KROWS - 1 for _, off in pats)
    return lo, hi


def _na_kernel(q_ref, k_ref, vt_ref, ck_ref, cv_ref, tz_ref, o_ref,
               bias_ref, cvt_ref, s0_ref, s1_ref, m0_ref, m1_ref, *, pats):
    n = q_ref.shape[0]
    n_ctx = ck_ref.shape[0]
    rows = n // GRID_W
    nblk = rows // NA_ROWS
    _, head_mask = _head_masks()
    _, dr_hi = _na_slot_range(pats)
    n_keys = NA_K + n_ctx

    @pl.when(pl.program_id(1) == 0)
    def _build_bias():
        key = lax.broadcasted_iota(jnp.int32, (NA_K, NA_Q), 0)
        qry = lax.broadcasted_iota(jnp.int32, (NA_K, NA_Q), 1)
        j, kc = key // GRID_W, key % GRID_W
        i, qc = qry // GRID_W, qry % GRID_W
        c0 = jnp.clip(qc - NA_WIN_W // 2, 0, GRID_W - NA_WIN_W)
        col_ok = (kc >= c0) & (kc < c0 + NA_WIN_W)
        for p, (r0, off) in enumerate(pats):
            lo = jnp.zeros_like(i)
            for ii in range(NA_ROWS):
                lo = jnp.where(i == ii, r0[ii], lo)
            ok = col_ok & (j >= lo) & (j < lo + NA_WIN_H)
            for h in range(HEAD_GROUP):
                t = tz_ref[h]
                for jj in range(NA_KROWS):
                    slot = dr_hi - (jj + off)
                    tile = t[:, slot * GRID_W:slot * GRID_W + NA_Q] * LOG2E
                    rs = slice(jj * GRID_W, (jj + 1) * GRID_W)
                    bias_ref[p, rs, h * NA_Q:(h + 1) * NA_Q] = jnp.where(ok[rs, :], tile, NEG_INF)

    cvt_ref[...] = cv_ref[...].astype(F32).T.astype(BF16)
    ones_rows = jnp.ones((2 * 8, n_keys), BF16)

    def block_rows(qb):
        kb = jnp.clip(qb * NA_ROWS - NA_WIN_H // 2, 0, rows - NA_KROWS)
        q_rows = pl.ds(pl.multiple_of(qb * NA_Q, NA_Q), NA_Q)
        k_tok = pl.ds(pl.multiple_of(kb * GRID_W, NA_Q), NA_K)
        return q_rows, k_tok

    def scores(qb, s_ref, m_ref):
        q_rows, k_tok = block_rows(qb)
        pat = jnp.where(qb == 0, 0, jnp.where(qb == nblk - 1, 2, 1))
        q = q_ref[q_rows, :]
        q_heads = jnp.concatenate([q * head_mask[h] for h in range(HEAD_GROUP)], axis=0)
        s_ref[0:NA_K, :] = (lax.dot_general(k_ref[k_tok, :], q_heads, _NT, preferred_element_type=F32)
                            + bias_ref[pat])
        s_ref[NA_K:n_keys, :] = lax.dot_general(ck_ref[...], q_heads, _NT, preferred_element_type=F32)
        m_ref[...] = jnp.max(s_ref[...], axis=0, keepdims=True)

    def attend(qb, s_ref, m_ref):
        q_rows, k_tok = block_rows(qb)
        p = jnp.exp2(s_ref[...] - m_ref[...]).astype(BF16)
        outs = []
        for h in range(HEAD_GROUP):
            hs = slice(h * HEAD_DIM, (h + 1) * HEAD_DIM)
            v_aug = jnp.concatenate(
                [jnp.concatenate([vt_ref[hs, k_tok], cvt_ref[hs, :]], axis=1), ones_rows], axis=0)
            oh = jnp.dot(v_aug, p[:, h * NA_Q:(h + 1) * NA_Q], preferred_element_type=F32)
            outs.append(oh[0:HEAD_DIM, :] / oh[HEAD_DIM:HEAD_DIM + 1, :])
        o_ref[q_rows, :] = jnp.concatenate(outs, axis=0).T.astype(BF16)

    assert nblk % 2 == 0
    scores(0, s0_ref, m0_ref)

    def pair(t, carry):
        qb = 2 * t
        scores(qb + 1, s1_ref, m1_ref)
        attend(qb, s0_ref, m0_ref)
        scores(jnp.minimum(qb + 2, nblk - 1), s0_ref, m0_ref)
        attend(qb + 1, s1_ref, m1_ref)
        return carry

    lax.fori_loop(0, nblk // 2, pair, 0, unroll=4)


def _na_toeplitz(rpb, pats):
    dr_lo, dr_hi = _na_slot_range(pats)
    slots = dr_hi - dr_lo + 1
    slots += (-slots) % (LANES // GRID_W)
    row_sel = np.zeros((RPB_ROWS, slots), np.float32)
    for s in range(slots):
        if 0 <= dr_hi - s < RPB_ROWS:
            row_sel[dr_hi - s, s] = 1.0
    col = np.arange(GRID_W)
    dc = col[:, None] - col[None, :] + NA_WIN_W - 1
    col_sel = (dc[None, :, :] == np.arange(RPB_COLS)[:, None, None]).astype(np.float32)
    t = jnp.einsum('hab,as,bkq->hksq', rpb, row_sel, col_sel, precision=lax.Precision.HIGHEST)
    return t.reshape(rpb.shape[0], GRID_W, slots * GRID_W)


def _neighborhood(proj, vt, cproj, tz, pats):
    b, n, _ = proj.shape
    n_ctx = cproj.shape[1]
    col = lambda c0: pl.BlockSpec((None, n, GROUP_W), lambda g, bb: (bb, 0, c0 + g))
    ccol = lambda c0: pl.BlockSpec((None, n_ctx, GROUP_W), lambda g, bb: (bb, 0, c0 + g))
    return pl.pallas_call(
        functools.partial(_na_kernel, pats=pats),
        grid=(N_GROUPS, b),
        in_specs=[col(COL_NQ), col(COL_NK),
                  pl.BlockSpec((None, GROUP_W, n), lambda g, bb: (bb, g, 0)),
                  ccol(CCOL_NK), ccol(CCOL_NV),
                  pl.BlockSpec((HEAD_GROUP,) + tz.shape[1:], lambda g, bb: (g, 0, 0))],
        out_specs=pl.BlockSpec((None, n, GROUP_W), lambda g, bb: (bb, 0, g)),
        out_shape=jax.ShapeDtypeStruct((b, n, MIX_W), BF16),
        scratch_shapes=[pltpu.VMEM((len(pats), NA_K, HEAD_GROUP * NA_Q), F32),
                        pltpu.VMEM((GROUP_W, n_ctx), BF16),
                        pltpu.VMEM((NA_K + n_ctx, HEAD_GROUP * NA_Q), F32),
                        pltpu.VMEM((NA_K + n_ctx, HEAD_GROUP * NA_Q), F32),
                        pltpu.VMEM((1, HEAD_GROUP * NA_Q), F32),
                        pltpu.VMEM((1, HEAD_GROUP * NA_Q), F32)],
        compiler_params=_cparams("arbitrary", "arbitrary"),
        name="neighborhood",
    )(proj, proj, vt, cproj, cproj, tz)


def _mix_ffn_kernel(x_ref, yr_ref, yn_ref, gr_ref, gn_ref, gt1_ref, sh2_ref, sc2_ref, gt2_ref,
                    g_post_mix_ref, g_pre_ffn_ref, g_post_ffn_ref,
                    w_ret_hbm, w_na_hbm, w_o_hbm, w1_hbm, w2_hbm, o_ref,
                    w_ret_ref, w_na_ref, w_o_ref, w1_ref, w2_ref, stage_ref, sems):
    @pl.when((pl.program_id(0) == 0) & (pl.program_id(1) == 0))
    def _load_weights():
        width = stage_ref.shape[2]
        pairs = ((w_ret_hbm, w_ret_ref), (w_na_hbm, w_na_ref), (w_o_hbm, w_o_ref),
                 (w1_hbm, w1_ref), (w2_hbm, w2_ref))
        tasks = [(src, off, dst, off, False) for src, dst in pairs for off in range(0, src.shape[1], width)]
        _stream_cast(tasks, stage_ref, sems)

    tm = x_ref.shape[0]
    d_ff = w1_ref.shape[1]
    a = jnp.dot(yr_ref[...], w_ret_ref[...], preferred_element_type=F32)
    bb = jnp.dot(yn_ref[...], w_na_ref[...], preferred_element_type=F32)
    y = (jax.nn.sigmoid(gr_ref[...].astype(F32)) * a + jax.nn.sigmoid(gn_ref[...].astype(F32)) * bb)
    y = jnp.dot(y.astype(BF16), w_o_ref[...], preferred_element_type=F32)
    x1 = x_ref[...] + gt1_ref[...] * (_rms(y) * g_post_mix_ref[...])
    h2 = ((_rms(x1) * g_pre_ffn_ref[...]) * (1.0 + sc2_ref[...]) + sh2_ref[...]).astype(BF16)
    f = jnp.zeros((tm, x_ref.shape[1]), F32)
    for c0 in range(0, d_ff, FF_CHUNK):
        u = jnp.dot(h2, w1_ref[:, c0:c0 + FF_CHUNK], preferred_element_type=F32)
        u = jnp.square(jnp.maximum(u, 0.0)).astype(BF16)
        f = f + jnp.dot(u, w2_ref[c0:c0 + FF_CHUNK, :], preferred_element_type=F32)
    o_ref[...] = x1 + gt2_ref[...] * (_rms(f) * g_post_ffn_ref[...])


def _mix_ffn(x, y_ret, y_na, proj, mod4, g_post_mix, g_pre_ffn, g_post_ffn,
             w_ret, w_na, w_o, w1, w2):
    b, n, d = x.shape
    tm = MIX_TM
    tok = lambda w: pl.BlockSpec((None, tm, w), lambda i, bb: (bb, i, 0))
    mod_spec = lambda k: pl.BlockSpec((None, None, 1, d), lambda i, bb: (bb, k, 0, 0))
    return pl.pallas_call(
        _mix_ffn_kernel,
        grid=(n // tm, b),
        in_specs=[tok(d), tok(MIX_W), tok(MIX_W),
                  pl.BlockSpec((None, tm, d), lambda i, bb: (bb, i, 0)),
                  pl.BlockSpec((None, tm, d), lambda i, bb: (bb, i, 1)),
                  mod_spec(2), mod_spec(3), mod_spec(4), mod_spec(5),
                  _resident((1, d)), _resident((1, d)), _resident((1, d))]
                 + [pl.BlockSpec(memory_space=pl.ANY)] * 5,
        out_specs=tok(d),
        out_shape=jax.ShapeDtypeStruct((b, n, d), F32),
        scratch_shapes=[pltpu.VMEM(w.shape, BF16) for w in (w_ret, w_na, w_o, w1, w2)]
                       + [pltpu.VMEM((STAGE_SLOTS, MIX_STAGE_ROWS, d), F32),
                          pltpu.SemaphoreType.DMA((STAGE_SLOTS,))],
        compiler_params=_cparams("arbitrary", "arbitrary"),
        name="mix_ffn",
    )(x, y_ret, y_na, proj, proj, mod4, mod4, mod4, mod4, g_post_mix, g_pre_ffn, g_post_ffn,
      w_ret, w_na, w_o, w1, w2)


def _rope_tables(n):
    pos = np.arange(n)
    row = (pos // GRID_W).astype(np.float64)
    colp = (pos % GRID_W).astype(np.float64)
    d_axis = HEAD_DIM // 2
    inv = ROPE_BASE ** (-np.arange(0, d_axis, 2, dtype=np.float64) / d_axis)
    ang = np.concatenate([row[:, None] * inv, colp[:, None] * inv], axis=-1)
    cos, sin = np.cos(ang), np.sin(ang)
    zero = np.zeros_like(sin)
    reps = LANES // HEAD_DIM
    cos_t = np.tile(np.concatenate([cos, cos], axis=-1), (1, reps))
    sin_lo = np.tile(np.concatenate([-sin, zero], axis=-1), (1, reps))
    sin_hi = np.tile(np.concatenate([zero, sin], axis=-1), (1, reps))
    return tuple(jnp.asarray(t, F32) for t in (cos_t, sin_lo, sin_hi))


def kernel(x, c, ctx, c_ctx, w_ada, b_ada, norm_pre_mix, norm_post_mix, norm_pre_ffn, norm_post_ffn,
           w_in, ret_decay_logit, w_ret_out, na_rpb, w_na_out, w_o, w_ff1, w_ff2):
    b, n, d = x.shape
    n_ctx = ctx.shape[1]
    assert w_ada.shape[0] == 1, "single-layer block"
    assert n % NA_Q == 0 and n % PROJ_TM == 0 and n % MIX_TM == 0 and n % RET_L == 0
    assert n_ctx % RET_L == 0 and d == D_MODEL and 7 * MIX_W + GATE_COLS == w_in.shape[2]

    pad = (-(b + 1)) % 8
    c_rows = jnp.concatenate([c, c_ctx[None, :], jnp.zeros((pad, d), F32)], axis=0)
    mod = _ada(c_rows, w_ada[0], b_ada[0])
    mod4 = mod.reshape(c_rows.shape[0], 6, 1, d)

    g_pre_mix = norm_pre_mix[0].reshape(1, d)
    proj, nvt = _proj_latent(x, mod4, g_pre_mix, w_in[0], _rope_tables(n))
    cproj = _proj_context(ctx.reshape(b * n_ctx, d), mod4, b, g_pre_mix, w_in[0])
    cproj = cproj.reshape(b, n_ctx, 4 * MIX_W)

    logit = ret_decay_logit[0].astype(F32)
    y_ret = _retention(proj, cproj, jnp.repeat(logit, HEAD_DIM, axis=1), jnp.repeat(logit, RET_L, axis=1))

    pats = _na_patterns(n // GRID_W)
    y_na = _neighborhood(proj, nvt, cproj, _na_toeplitz(na_rpb[0].astype(F32), pats), pats)

    return _mix_ffn(x, y_ret, y_na, proj, mod4,
                    norm_post_mix[0].reshape(1, d), norm_pre_ffn[0].reshape(1, d),
                    norm_post_ffn[0].reshape(1, d),
                    w_ret_out[0], w_na_out[0], w_o[0], w_ff1[0], w_ff2[0])
```
